```python
import jax
import jax.numpy as jnp
from jax import lax
import numpy as np

D_MODEL = 2048
BATCH = 2
SEQ = 4096
DEPTH = 4
DEC_BATCH = 8
DEC_SEQ = 1
PAST_LEN = 16384
PAGE_SIZE = 128

N_A_LAYERS = DEPTH // 2
N_B_LAYERS = DEPTH - N_A_LAYERS
HEAD_DIM = 128
ROT_DIM = HEAD_DIM // 4
ROPE_THETA = 500000.0
MEM_LEN = 256
MEM_HEADS = 4
MEM_W = MEM_HEADS * HEAD_DIM
GM_GROUPS = 12
GM_W = GM_GROUPS * HEAD_DIM
CHUNK = 128
NSA_HEADS = 12
NSA_KV = 4
NSA_REP = NSA_HEADS // NSA_KV
NSA_W = NSA_HEADS * HEAD_DIM
N_GATES = 3
CMP_LEN = 32
CMP_STRIDE = 16
SEL_BLOCK = 64
N_SELECT = 16
WINDOW = 512
QB = 128
D_FF = 5632
EPS = 1e-6
NEG = -1e30
FORCED_SCORE = 1e6
SCALE = HEAD_DIM ** -0.5

kernel_name = 'yoco_gmlp_nsa_macaron_decode_step'


def rms_norm(x, g):
    xf = x.astype(jnp.float32)
    y = xf * lax.rsqrt(jnp.mean(xf * xf, axis=-1, keepdims=True) + EPS)
    return (y * g.astype(jnp.float32)).astype(x.dtype)


def layer_norm(x, g, b):
    xf = x.astype(jnp.float32)
    xc = xf - jnp.mean(xf, axis=-1, keepdims=True)
    var = jnp.mean(xc * xc, axis=-1, keepdims=True)
    return (xc * lax.rsqrt(var + EPS) * g.astype(jnp.float32) + b.astype(jnp.float32)).astype(x.dtype)


def rope(x, pos):
    half = ROT_DIM // 2
    inv = ROPE_THETA ** (-jnp.arange(half, dtype=jnp.float32) / half)
    ang = pos.astype(jnp.float32)[:, None] * inv[None, :]
    shape = (ang.shape[0],) + (1,) * (x.ndim - 3) + (half,)
    cos = jnp.cos(ang).reshape(shape)
    sin = jnp.sin(ang).reshape(shape)
    xf = x.astype(jnp.float32)
    x1, x2 = xf[..., :half], xf[..., half:ROT_DIM]
    out = jnp.concatenate([x1 * cos - x2 * sin, x2 * cos + x1 * sin, xf[..., ROT_DIM:]], axis=-1)
    return out.astype(x.dtype)


def swiglu(x, wg, wu, wd):
    return (jax.nn.silu(x @ wg) * (x @ wu)) @ wd


def gqa_attend(q, k, v, mask):
    s = jnp.einsum('...qgrd,...kgd->...qgrk', q, k).astype(jnp.float32) * SCALE
    m = mask[..., :, None, None, :]
    p = jax.nn.softmax(jnp.where(m, s, NEG), axis=-1) * m
    o = jnp.einsum('...qgrk,...kgd->...qgrd', p.astype(v.dtype), v)
    return o, p


def memory_kv(mem, g_mem, w_mem_kv, g_mk):
    B, M, _ = mem.shape
    kv = (rms_norm(mem, g_mem) @ w_mem_kv).reshape(B, M, 2, MEM_HEADS, HEAD_DIM)
    return rms_norm(kv[:, :, 0], g_mk), kv[:, :, 1]


def memory_attend(qm, mk, mv, g_mq):
    B, T, _ = qm.shape
    q = rms_norm(qm.reshape(B, T, MEM_HEADS, 1, HEAD_DIM), g_mq)
    o, _ = gqa_attend(q, mk, mv, jnp.ones((1, 1), dtype=bool))
    return o.reshape(B, T, MEM_W)


def gmlp_spatial(v, w_s, b_s):
    B, T = v.shape[:2]
    nc = -(-T // CHUNK)
    vp = jnp.pad(v, ((0, 0), (0, nc * CHUNK - T), (0, 0), (0, 0))).reshape(B, nc, CHUNK, GM_GROUPS, HEAD_DIM)
    w = jnp.where(jnp.tril(jnp.ones((CHUNK, CHUNK), dtype=bool)), w_s, 0.0)
    out = jnp.einsum('gij,bcjgd->bcigd', w, vp) + b_s.T[None, None, :, :, None]
    return out.reshape(B, nc * CHUNK, GM_GROUPS, HEAD_DIM)[:, :T]


def mixer_a(h, mk, mv, w_in, ln_g, ln_b, w_s, b_s, w_out, g_mq):
    B, T, _ = h.shape
    z = h @ w_in
    uv = jax.nn.gelu(z[..., :2 * GM_W], approximate=False)
    u = uv[..., :GM_W]
    v = layer_norm(uv[..., GM_W:], ln_g, ln_b)
    sv = gmlp_spatial(v.reshape(B, T, GM_GROUPS, HEAD_DIM), w_s, b_s).reshape(B, T, GM_W)
    o_mem = memory_attend(z[..., 2 * GM_W:], mk, mv, g_mq)
    return jnp.concatenate([u * sv, o_mem], axis=-1) @ w_out, v


def nsa_shared_kv(x, pos, g_kv, w_kv, g_ks, g_kw):
    B, T, _ = x.shape
    h = (rms_norm(x, g_kv) @ w_kv).reshape(B, T, 6, NSA_KV, HEAD_DIM)
    k_sel = rope(rms_norm(h[:, :, 2], g_ks), pos)
    k_win = rope(rms_norm(h[:, :, 4], g_kw), pos)
    return h[:, :, 0], h[:, :, 1], k_sel, h[:, :, 3], k_win, h[:, :, 5]


def compress(rows, pos_enc, w1, w2):
    B, L = rows.shape[:2]
    nc = (L - CMP_LEN) // CMP_STRIDE + 1
    ch = rows[:, :(nc + 1) * CMP_STRIDE].reshape(B, nc + 1, CMP_STRIDE, NSA_KV, HEAD_DIM)
    pe = pos_enc.transpose(1, 0, 2)
    h = (jnp.einsum('bmpgd,gpde->bmge', ch[:, :nc] + pe[:CMP_STRIDE], w1[:, :CMP_STRIDE])
         + jnp.einsum('bmpgd,gpde->bmge', ch[:, 1:] + pe[CMP_STRIDE:], w1[:, CMP_STRIDE:]))
    return jnp.einsum('bmge,gef->bmgf', jax.nn.gelu(h, approximate=False), w2)


def cmp_attend(qn, kc, vc, q_pos):
    nc = kc.shape[1]
    end = jnp.arange(nc) * CMP_STRIDE + CMP_LEN - 1
    return gqa_attend(qn, kc, vc, end[None, :] <= q_pos[:, None])


def select_blocks(p_cmp, q_pos, n_sel):
    nc = p_cmp.shape[-1]
    ci = jnp.arange(nc)[:, None] * CMP_STRIDE
    sj = jnp.arange(n_sel)[None, :] * SEL_BLOCK
    cover = ((ci < sj + SEL_BLOCK) & (ci + CMP_LEN > sj)).astype(jnp.float32)
    imp = jnp.einsum('btgrc,cs->btgs', p_cmp, cover)
    j = jnp.arange(n_sel)[None, :]
    cur = (q_pos // SEL_BLOCK)[:, None]
    avail = (j * SEL_BLOCK <= q_pos[:, None])[None, :, None, :]
    forced = ((j == 0) | (j == cur) | (j == cur - 1))[None, :, None, :]
    score = jnp.where(avail, jnp.where(forced, FORCED_SCORE, imp), NEG)
    top, idx = lax.top_k(score, min(N_SELECT, n_sel))
    return idx, top > 0.5 * NEG


def sel_attend(qr, kg, vg, idx, valid, q_pos):
    B, T, G, n = idx.shape
    kpos = idx[..., None] * SEL_BLOCK + jnp.arange(SEL_BLOCK)
    mask = (valid[..., None] & (kpos <= q_pos[None, :, None, None, None])).reshape(B, T, G, n * SEL_BLOCK)
    kf = kg.reshape(B, T, G, n * SEL_BLOCK, HEAD_DIM)
    vf = vg.reshape(B, T, G, n * SEL_BLOCK, HEAD_DIM)
    s = jnp.einsum('btgrd,btgkd->btgrk', qr, kf).astype(jnp.float32) * SCALE
    m = mask[:, :, :, None, :]
    p = jax.nn.softmax(jnp.where(m, s, NEG), axis=-1) * m
    return jnp.einsum('btgrk,btgkd->btgrd', p.astype(vf.dtype), vf)


def nsa_prompt_cmp_sel(qn, qr, kc, vc, k_sel, v_sel):
    B, T = qn.shape[:2]
    ns = T // SEL_BLOCK
    kb = k_sel.reshape(B, ns, SEL_BLOCK, NSA_KV, HEAD_DIM).transpose(0, 3, 1, 2, 4)
    vb = v_sel.reshape(B, ns, SEL_BLOCK, NSA_KV, HEAD_DIM).transpose(0, 3, 1, 2, 4)
    bi = jnp.arange(B)[:, None, None, None]
    gi = jnp.arange(NSA_KV)[None, None, :, None]

    def block(args):
        qn_b, qr_b, pos_b = args
        o_c, p = cmp_attend(qn_b, kc, vc, pos_b)
        idx, valid = select_blocks(p, pos_b, ns)
        o_s = sel_attend(qr_b, kb[bi, gi, idx], vb[bi, gi, idx], idx, valid, pos_b)
        return o_c, o_s

    nb = T // QB
    split = lambda a: a.reshape((B, nb, QB) + a.shape[2:]).swapaxes(0, 1)
    merge = lambda a: a.swapaxes(0, 1).reshape((B, T) + a.shape[3:])
    o_c, o_s = lax.map(block, (split(qn), split(qr), jnp.arange(T, dtype=jnp.int32).reshape(nb, QB)))
    return merge(o_c), merge(o_s)


def nsa_sample_cmp_sel(qn, qr, pos, kc, vc, k_new, v_new, pool_k, pool_v, page_table):
    DB, DS = qn.shape[:2]
    nsp = PAST_LEN // SEL_BLOCK
    nbn = -(-DS // SEL_BLOCK)

    def new_blocks(a):
        a = jnp.pad(a, ((0, 0), (0, nbn * SEL_BLOCK - DS), (0, 0), (0, 0)))
        return a.reshape(DB, nbn, SEL_BLOCK, NSA_KV, HEAD_DIM).transpose(0, 3, 1, 2, 4)

    o_c, p = cmp_attend(qn, kc, vc, pos)
    idx, valid = select_blocks(p, pos, nsp + nbn)
    bi = jnp.arange(DB)[:, None, None, None]
    gi = jnp.arange(NSA_KV)[None, None, :, None]
    in_past = idx < nsp
    start = jnp.minimum(idx, nsp - 1) * SEL_BLOCK
    page = page_table[bi, start // PAGE_SIZE]
    row = (start % PAGE_SIZE)[..., None] + jnp.arange(SEL_BLOCK)
    jn = jnp.clip(idx - nsp, 0, nbn - 1)

    def gather(pool, new):
        from_pool = pool[page[..., None], row, gi[..., None]]
        from_new = new_blocks(new)[bi, gi, jn]
        return jnp.where(in_past[..., None, None], from_pool, from_new)

    o_s = sel_attend(qr, gather(pool_k, k_new), gather(pool_v, v_new), idx, valid, pos)
    return o_c, o_s


def window_prompt(qr, k, v):
    B, T = qr.shape[:2]
    nb = T // QB
    nw = WINDOW // QB

    def band(a):
        ap = jnp.pad(a, ((0, 0), (WINDOW, 0), (0, 0), (0, 0))).reshape(B, nb + nw, QB, NSA_KV, HEAD_DIM)
        return jnp.concatenate([ap[:, i:i + nb] for i in range(nw + 1)], axis=2)

    tq = jnp.arange(T).reshape(nb, QB)[:, :, None]
    kq = ((jnp.arange(nb) * QB - WINDOW)[:, None] + jnp.arange((nw + 1) * QB)[None, :])[:, None, :]
    mask = (kq >= 0) & (kq >= tq - WINDOW) & (kq <= tq)
    o, _ = gqa_attend(qr.reshape(B, nb, QB, NSA_KV, NSA_REP, HEAD_DIM), band(k), band(v), mask)
    return o.reshape(B, T, NSA_KV, NSA_REP, HEAD_DIM)


def window_sample(qr, pos, k_buf, v_buf):
    nk = k_buf.shape[1]
    kpos = PAST_LEN + qr.shape[1] - nk + jnp.arange(nk)
    mask = (kpos[None, :] >= pos[:, None] - WINDOW) & (kpos[None, :] <= pos[:, None])
    o, _ = gqa_attend(qr, k_buf, v_buf, mask)
    return o


def mixer_b(h, pos, mk, mv, w_in, g_q, g_mq, w_out, cmp_sel_fn, win_fn):
    B, T, _ = h.shape
    z = h @ w_in
    qn = rms_norm(z[..., :NSA_W].reshape(B, T, NSA_KV, NSA_REP, HEAD_DIM), g_q)
    qr = rope(qn, pos)
    gates = jax.nn.sigmoid(z[..., NSA_W:NSA_W + N_GATES * NSA_HEADS].astype(jnp.float32))
    gates = gates.reshape(B, T, NSA_KV, NSA_REP, N_GATES).astype(h.dtype)
    o_c, o_s = cmp_sel_fn(qn, qr)
    o_w = win_fn(qr)
    o_nsa = gates[..., 0:1] * o_c + gates[..., 1:2] * o_s + gates[..., 2:3] * o_w
    o_mem = memory_attend(z[..., NSA_W + N_GATES * NSA_HEADS:], mk, mv, g_mq)
    return jnp.concatenate([o_nsa.reshape(B, T, NSA_W), o_mem], axis=-1) @ w_out


def setup_inputs(seed: int = 0) -> dict:
    key = jax.random.key(seed)
    ks = iter(jax.random.split(key, 64))

    def nrm(shape, scale):
        return scale * jax.random.normal(next(ks), shape, jnp.float32)

    def gain(shape):
        return 1.0 + nrm(shape, 0.02)

    n_pages = PAST_LEN // PAGE_SIZE
    n_pool = (5 * DEC_BATCH * n_pages + 3) // 4
    wbuf = min(WINDOW, PAST_LEN)
    perm = jax.random.permutation(next(ks), n_pool)
    page_table = perm[:DEC_BATCH * n_pages].reshape(DEC_BATCH, n_pages).astype(jnp.int32)
    paged = (n_pool, PAGE_SIZE, NSA_KV, HEAD_DIM)
    sd = D_MODEL ** -0.5
    sf = D_FF ** -0.5
    so = (GM_W + MEM_W) ** -0.5
    sc = (CMP_LEN * HEAD_DIM) ** -0.5
    sh = HEAD_DIM ** -0.5
    return {
        'x_prompt': nrm((BATCH, SEQ, D_MODEL), 1.0),
        'x_sample': nrm((DEC_BATCH, DEC_SEQ, D_MODEL), 1.0),
        'mem_prompt': nrm((BATCH, MEM_LEN, D_MODEL), 1.0),
        'cache_mem_k': nrm((DEPTH, DEC_BATCH, MEM_LEN, MEM_HEADS, HEAD_DIM), 1.0),
        'cache_mem_v': nrm((DEPTH, DEC_BATCH, MEM_LEN, MEM_HEADS, HEAD_DIM), 1.0),
        'cache_k_cmp': nrm(paged, 1.0),
        'cache_v_cmp': nrm(paged, 1.0),
        'cache_k_sel': nrm(paged, 1.0),
        'cache_v_sel': nrm(paged, 1.0),
        'state_k_win': nrm((DEC_BATCH, wbuf, NSA_KV, HEAD_DIM), 1.0),
        'state_v_win': nrm((DEC_BATCH, wbuf, NSA_KV, HEAD_DIM), 1.0),
        'page_table': page_table,
        'g_ffn1': gain((DEPTH, D_MODEL)),
        'w_ffn1_gate': nrm((DEPTH, D_MODEL, D_FF), sd),
        'w_ffn1_up': nrm((DEPTH, D_MODEL, D_FF), sd),
        'w_ffn1_down': nrm((DEPTH, D_FF, D_MODEL), sf),
        'g_mix': gain((DEPTH, D_MODEL)),
        'g_ffn2': gain((DEPTH, D_MODEL)),
        'w_ffn2_gate': nrm((DEPTH, D_MODEL, D_FF), sd),
        'w_ffn2_up': nrm((DEPTH, D_MODEL, D_FF), sd),
        'w_ffn2_down': nrm((DEPTH, D_FF, D_MODEL), sf),
        'g_mem': gain((DEPTH, D_MODEL)),
        'w_mem_kv': nrm((DEPTH, D_MODEL, 2 * MEM_W), sd),
        'g_mq': gain((DEPTH, HEAD_DIM)),
        'g_mk': gain((DEPTH, HEAD_DIM)),
        'w_in_a': nrm((N_A_LAYERS, D_MODEL, 2 * GM_W + MEM_W), sd),
        'ln_v_g': gain((N_A_LAYERS, GM_W)),
        'ln_v_b': nrm((N_A_LAYERS, GM_W), 0.02),
        'w_spatial': nrm((N_A_LAYERS, GM_GROUPS, CHUNK, CHUNK), CHUNK ** -0.5),
        'b_spatial': 1.0 + nrm((N_A_LAYERS, GM_GROUPS, CHUNK), 0.1),
        'w_out_a': nrm((N_A_LAYERS, GM_W + MEM_W, D_MODEL), so),
        'g_kv': gain((D_MODEL,)),
        'w_kv': nrm((D_MODEL, 6 * NSA_KV * HEAD_DIM), sd),
        'pos_kc': nrm((NSA_KV, CMP_LEN, HEAD_DIM), 0.1),
        'w1_kc': nrm((NSA_KV, CMP_LEN, HEAD_DIM, HEAD_DIM), sc),
        'w2_kc': nrm((NSA_KV, HEAD_DIM, HEAD_DIM), sh),
        'pos_vc': nrm((NSA_KV, CMP_LEN, HEAD_DIM), 0.1),
        'w1_vc': nrm((NSA_KV, CMP_LEN, HEAD_DIM, HEAD_DIM), sc),
        'w2_vc': nrm((NSA_KV, HEAD_DIM, HEAD_DIM), sh),
        'g_kc': gain((HEAD_DIM,)),
        'g_ks': gain((HEAD_DIM,)),
        'g_kw': gain((HEAD_DIM,)),
        'w_in_b': nrm((N_B_LAYERS, D_MODEL, NSA_W + N_GATES * NSA_HEADS + MEM_W), sd),
        'g_q': gain((N_B_LAYERS, HEAD_DIM)),
        'w_out_b': nrm((N_B_LAYERS, NSA_W + MEM_W, D_MODEL), so),
    }


def reference(x_prompt, x_sample, mem_prompt, cache_mem_k, cache_mem_v,
              cache_k_cmp, cache_v_cmp, cache_k_sel, cache_v_sel,
              state_k_win, state_v_win, page_table,
              g_ffn1, w_ffn1_gate, w_ffn1_up, w_ffn1_down, g_mix,
              g_ffn2, w_ffn2_gate, w_ffn2_up, w_ffn2_down,
              g_mem, w_mem_kv, g_mq, g_mk,
              w_in_a, ln_v_g, ln_v_b, w_spatial, b_spatial, w_out_a,
              g_kv, w_kv, pos_kc, w1_kc, w2_kc, pos_vc, w1_vc, w2_vc, g_kc, g_ks, g_kw,
              w_in_b, g_q, w_out_b):
    T = x_prompt.shape[1]
    DB, DS = x_sample.shape[:2]
    pos_p = jnp.arange(T, dtype=jnp.int32)
    pos_s = PAST_LEN + jnp.arange(DS, dtype=jnp.int32)
    xp, xs = x_prompt, x_sample
    mem_k_list, mem_v_list, gm_v_list = [], [], []

    for l in range(DEPTH):
        if l == N_A_LAYERS:
            kc_p_raw, vc_p_raw, ks_p, vs_p, kw_p, vw_p = nsa_shared_kv(xp, pos_p, g_kv, w_kv, g_ks, g_kw)
            kc_s_raw, vc_s_raw, ks_s, vs_s, kw_s, vw_s = nsa_shared_kv(xs, pos_s, g_kv, w_kv, g_ks, g_kw)
            kc_p = rms_norm(compress(kc_p_raw, pos_kc, w1_kc, w2_kc), g_kc)
            vc_p = compress(vc_p_raw, pos_vc, w1_vc, w2_vc)
            past = lambda pool: pool[page_table].reshape((DB, -1) + pool.shape[2:])
            kc_s = rms_norm(compress(jnp.concatenate([past(cache_k_cmp), kc_s_raw], axis=1), pos_kc, w1_kc, w2_kc), g_kc)
            vc_s = compress(jnp.concatenate([past(cache_v_cmp), vc_s_raw], axis=1), pos_vc, w1_vc, w2_vc)
            kw_buf = jnp.concatenate([state_k_win, kw_s], axis=1)
            vw_buf = jnp.concatenate([state_v_win, vw_s], axis=1)
            cs_p = lambda qn, qr: nsa_prompt_cmp_sel(qn, qr, kc_p, vc_p, ks_p, vs_p)
            cs_s = lambda qn, qr: nsa_sample_cmp_sel(qn, qr, pos_s, kc_s, vc_s, ks_s, vs_s,
                                                     cache_k_sel, cache_v_sel, page_table)
            win_p = lambda qr: window_prompt(qr, kw_p, vw_p)
            win_s = lambda qr: window_sample(qr, pos_s, kw_buf, vw_buf)

        f1 = (w_ffn1_gate[l], w_ffn1_up[l], w_ffn1_down[l])
        xp = xp + 0.5 * swiglu(rms_norm(xp, g_ffn1[l]), *f1)
        xs = xs + 0.5 * swiglu(rms_norm(xs, g_ffn1[l]), *f1)

        mk_p, mv_p = memory_kv(mem_prompt, g_mem[l], w_mem_kv[l], g_mk[l])
        mem_k_list.append(mk_p)
        mem_v_list.append(mv_p)
        hp = rms_norm(xp, g_mix[l])
        hs = rms_norm(xs, g_mix[l])
        if l < N_A_LAYERS:
            aw = (w_in_a[l], ln_v_g[l], ln_v_b[l], w_spatial[l], b_spatial[l], w_out_a[l], g_mq[l])
            o_p, _ = mixer_a(hp, mk_p, mv_p, *aw)
            o_s, v_s = mixer_a(hs, cache_mem_k[l], cache_mem_v[l], *aw)
            gm_v_list.append(v_s)
        else:
            lb = l - N_A_LAYERS
            bw = (w_in_b[lb], g_q[lb], g_mq[l], w_out_b[lb])
            o_p = mixer_b(hp, pos_p, mk_p, mv_p, *bw, cs_p, win_p)
            o_s = mixer_b(hs, pos_s, cache_mem_k[l], cache_mem_v[l], *bw, cs_s, win_s)
        xp = xp + o_p
        xs = xs + o_s

        f2 = (w_ffn2_gate[l], w_ffn2_up[l], w_ffn2_down[l])
        xp = xp + 0.5 * swiglu(rms_norm(xp, g_ffn2[l]), *f2)
        xs = xs + 0.5 * swiglu(rms_norm(xs, g_ffn2[l]), *f2)

    wp = min(WINDOW, T)
    ws = min(WINDOW, PAST_LEN + DS)
    return (xp, xs, jnp.stack(mem_k_list), jnp.stack(mem_v_list),
            kc_p_raw, vc_p_raw, ks_p, vs_p, kw_p[:, -wp:], vw_p[:, -wp:],
            kc_s_raw, vc_s_raw, ks_s, vs_s, kw_buf[:, -ws:], vw_buf[:, -ws:],
            jnp.stack(gm_v_list))
```

```python
from functools import partial

import numpy as np
import jax
import jax.numpy as jnp
from jax import lax
from jax.experimental import pallas as pl
from jax.experimental.pallas import tpu as pltpu

D_MODEL = 2048
HEAD_DIM = 128
ROT_DIM = HEAD_DIM // 4
ROPE_THETA = 500000.0
MEM_LEN = 256
MEM_HEADS = 4
MEM_W = MEM_HEADS * HEAD_DIM
GM_GROUPS = 12
GM_W = GM_GROUPS * HEAD_DIM
CHUNK = 128
NSA_HEADS = 12
NSA_KV = 4
NSA_REP = NSA_HEADS // NSA_KV
NSA_W = NSA_HEADS * HEAD_DIM
KV_W = NSA_KV * HEAD_DIM
N_GATES = 3
CMP_LEN = 32
CMP_STRIDE = 16
SEL_BLOCK = 64
N_SELECT = 16
WINDOW = 512
QB = 128
PAGE_SIZE = 128
EPS = 1e-6
NEG = -1e30
EXCLUDED = -3e38
FORCED_SCORE = 1e6
SCALE = HEAD_DIM ** -0.5

LANE = 128
SUBLANE = 8
VMEM_LIMIT = 56 * 1024 * 1024
PAGES_PER_GROUP = 16
CHUNKS_PER_PAGE = PAGE_SIZE // CMP_STRIDE
SEL_KEY_CHUNK = 512
SEL_CAND_LANES = 3 * LANE
SEL_SHIFT = SEL_BLOCK.bit_length() - 1

BF16 = jnp.bfloat16
F32 = jnp.float32


def _params(*sem):
    return pltpu.CompilerParams(dimension_semantics=sem, vmem_limit_bytes=VMEM_LIMIT)


def _dot(a, b):
    return jnp.dot(a.astype(BF16), b.astype(BF16), preferred_element_type=F32)


def _dot_nt(a, b):
    return lax.dot_general(a.astype(BF16), b.astype(BF16), (((1,), (1,)), ((), ())),
                           preferred_element_type=F32)


def _rms(x, g):
    return x * lax.rsqrt(jnp.mean(x * x, axis=-1, keepdims=True) + EPS) * g


def _gelu(x):
    return 0.5 * x * (1.0 + lax.erf(x * np.float32(np.sqrt(0.5))))


def _rope(x, c, s1, s2):
    half = ROT_DIM // 2
    return x * c + pltpu.roll(x, LANE - half, 1) * s1 + pltpu.roll(x, half, 1) * s2


def _masked_softmax(s, m):
    sm = jnp.where(m, s, NEG)
    mx = jnp.max(sm, axis=-1, keepdims=True)
    e = jnp.where(m, jnp.exp(sm - mx), 0.0)
    den = jnp.sum(e, axis=-1, keepdims=True)
    return e / jnp.where(den > 0.0, den, 1.0)


def _row_tile(m, want):
    return want if m % want == 0 else m


def _ffn_kernel(x_ref, g_ref, wg_ref, wu_ref, wd_ref, o_ref, h_ref):
    @pl.when(pl.program_id(1) == 0)
    def _():
        x = x_ref[...]
        h_ref[...] = _rms(x, g_ref[...]).astype(BF16)
        o_ref[...] = x

    h = h_ref[...]
    a = jnp.dot(h, wg_ref[...].astype(BF16), preferred_element_type=F32)
    b = jnp.dot(h, wu_ref[...].astype(BF16), preferred_element_type=F32)
    t = a * jax.nn.sigmoid(a) * b
    o_ref[...] += 0.5 * _dot(t, wd_ref[...])


def _ffn(x, g, wg, wu, wd):
    m, d = x.shape
    f = wg.shape[1]
    tm = _row_tile(m, 512)
    tf = 512
    return pl.pallas_call(
        _ffn_kernel,
        grid=(m // tm, f // tf),
        in_specs=[
            pl.BlockSpec((tm, d), lambda i, j: (i, 0)),
            pl.BlockSpec((1, d), lambda i, j: (0, 0)),
            pl.BlockSpec((d, tf), lambda i, j: (0, j)),
            pl.BlockSpec((d, tf), lambda i, j: (0, j)),
            pl.BlockSpec((tf, d), lambda i, j: (j, 0)),
        ],
        out_specs=pl.BlockSpec((tm, d), lambda i, j: (i, 0)),
        out_shape=jax.ShapeDtypeStruct((m, d), F32),
        scratch_shapes=[pltpu.VMEM((tm, d), BF16)],
        compiler_params=_params("parallel", "arbitrary"),
        name="ffn",
    )(x, g.reshape(1, d), wg, wu, wd)


def _rms_mm_kernel(x_ref, g_ref, w_ref, o_ref, h_ref):
    @pl.when(pl.program_id(1) == 0)
    def _():
        h_ref[...] = _rms(x_ref[...], g_ref[...]).astype(BF16)

    o_ref[...] = jnp.dot(h_ref[...], w_ref[...].astype(BF16), preferred_element_type=F32)


def _rms_mm(x, g, w):
    m, d = x.shape
    n = w.shape[1]
    tm = _row_tile(m, 512)
    tn = _row_tile(n, 512)
    return pl.pallas_call(
        _rms_mm_kernel,
        grid=(m // tm, n // tn),
        in_specs=[
            pl.BlockSpec((tm, d), lambda i, j: (i, 0)),
            pl.BlockSpec((1, d), lambda i, j: (0, 0)),
            pl.BlockSpec((d, tn), lambda i, j: (0, j)),
        ],
        out_specs=pl.BlockSpec((tm, tn), lambda i, j: (i, j)),
        out_shape=jax.ShapeDtypeStruct((m, n), F32),
        scratch_shapes=[pltpu.VMEM((tm, d), BF16)],
        compiler_params=_params("parallel", "arbitrary"),
        name="rms_mm",
    )(x, g.reshape(1, d), w)


def _out_proj_kernel(a1_ref, a2_ref, w1_ref, w2_ref, x_ref, o_ref):
    o_ref[...] = x_ref[...] + (_dot(a1_ref[...], w1_ref[...]) + _dot(a2_ref[...], w2_ref[...]))


def _out_proj(a1, a2, w, x):
    m, k1 = a1.shape
    k2 = a2.shape[1]
    d = w.shape[1]
    tm = _row_tile(m, 512)
    tn = 512
    return pl.pallas_call(
        _out_proj_kernel,
        grid=(m // tm, d // tn),
        in_specs=[
            pl.BlockSpec((tm, k1), lambda i, j: (i, 0)),
            pl.BlockSpec((tm, k2), lambda i, j: (i, 0)),
            pl.BlockSpec((k1, tn), lambda i, j: (0, j)),
            pl.BlockSpec((k2, tn), lambda i, j: (k1 // k2, j)),
            pl.BlockSpec((tm, tn), lambda i, j: (i, j)),
        ],
        out_specs=pl.BlockSpec((tm, tn), lambda i, j: (i, j)),
        out_shape=jax.ShapeDtypeStruct((m, d), F32),
        compiler_params=_params("parallel", "arbitrary"),
        name="out_proj",
    )(a1, a2, w, w, x)


def _mem_kv_post_kernel(kv_ref, g_ref, k_ref, v_ref):
    g = g_ref[...]
    for h in range(MEM_HEADS):
        sl = slice(h * HEAD_DIM, (h + 1) * HEAD_DIM)
        k_ref[:, sl] = _rms(kv_ref[:, sl], g)
    v_ref[...] = kv_ref[:, MEM_W:]


def _mem_kv_post(kv, g_mk):
    m = kv.shape[0]
    return pl.pallas_call(
        _mem_kv_post_kernel,
        out_shape=(jax.ShapeDtypeStruct((m, MEM_W), F32), jax.ShapeDtypeStruct((m, MEM_W), F32)),
        compiler_params=pltpu.CompilerParams(vmem_limit_bytes=VMEM_LIMIT),
        name="mem_kv_post",
    )(kv, g_mk.reshape(1, HEAD_DIM))


def _mem_attend_kernel(q_ref, g_ref, k_ref, v_ref, o_ref, *, rows):
    g = g_ref[...]
    for h in range(MEM_HEADS):
        sl = slice(h * HEAD_DIM, (h + 1) * HEAD_DIM)
        q = q_ref[0, :, sl]
        if rows < SUBLANE:
            q = jnp.broadcast_to(q[0:1], (SUBLANE, HEAD_DIM))
        q = _rms(q, g)
        s = _dot_nt(q, k_ref[0, :, sl]) * SCALE
        mx = jnp.max(s, axis=-1, keepdims=True)
        e = jnp.exp(s - mx)
        p = e / jnp.sum(e, axis=-1, keepdims=True)
        o = _dot(p, v_ref[0, :, sl])
        o_ref[0, :, sl] = o[:rows]


def _mem_attend(z, col_block, mk, mv, g_mq):
    b, t, _ = z.shape
    tm = _row_tile(t, 512)
    return pl.pallas_call(
        partial(_mem_attend_kernel, rows=tm),
        grid=(b, t // tm),
        in_specs=[
            pl.BlockSpec((1, tm, MEM_W), lambda i, j: (i, j, col_block)),
            pl.BlockSpec((1, HEAD_DIM), lambda i, j: (0, 0)),
            pl.BlockSpec((1, MEM_LEN, MEM_W), lambda i, j: (i, 0, 0)),
            pl.BlockSpec((1, MEM_LEN, MEM_W), lambda i, j: (i, 0, 0)),
        ],
        out_specs=pl.BlockSpec((1, tm, MEM_W), lambda i, j: (i, j, 0)),
        out_shape=jax.ShapeDtypeStruct((b, t, MEM_W), F32),
        compiler_params=_params("parallel", "arbitrary"),
        name="mem_attend",
    )(z, g_mq.reshape(1, HEAD_DIM), mk, mv)


def _layer_norm(x, g, b):
    xc = x - jnp.mean(x, axis=-1, keepdims=True)
    var = jnp.mean(xc * xc, axis=-1, keepdims=True)
    return xc * lax.rsqrt(var + EPS) * g + b


def _gmlp_prompt_kernel(z_ref, lg_ref, lb_ref, ws_ref, bs_ref, o_ref):
    v = _layer_norm(_gelu(z_ref[:, GM_W:]), lg_ref[...], lb_ref[...])
    row = lax.broadcasted_iota(jnp.int32, (CHUNK, CHUNK), 0)
    col = lax.broadcasted_iota(jnp.int32, (CHUNK, CHUNK), 1)
    causal = col <= row
    for g in range(GM_GROUPS):
        sl = slice(g * HEAD_DIM, (g + 1) * HEAD_DIM)
        w = jnp.where(causal, ws_ref[g], 0.0)
        sv = _dot(w, v[:, sl]) + bs_ref[:, g:g + 1]
        o_ref[:, sl] = _gelu(z_ref[:, sl]) * sv


def _gmlp_prompt(z, ln_g, ln_b, w_s, b_s):
    m = z.shape[0]
    return pl.pallas_call(
        _gmlp_prompt_kernel,
        grid=(m // CHUNK,),
        in_specs=[
            pl.BlockSpec((CHUNK, 2 * GM_W), lambda i: (i, 0)),
            pl.BlockSpec((1, GM_W), lambda i: (0, 0)),
            pl.BlockSpec((1, GM_W), lambda i: (0, 0)),
            pl.BlockSpec((GM_GROUPS, CHUNK, CHUNK), lambda i: (0, 0, 0)),
            pl.BlockSpec((CHUNK, GM_GROUPS), lambda i: (0, 0)),
        ],
        out_specs=pl.BlockSpec((CHUNK, GM_W), lambda i: (i, 0)),
        out_shape=jax.ShapeDtypeStruct((m, GM_W), F32),
        compiler_params=_params("parallel"),
        name="gmlp_prompt",
    )(z, ln_g.reshape(1, GM_W), ln_b.reshape(1, GM_W), w_s, b_s.T)


def _gmlp_first_row_kernel(z_ref, lg_ref, lb_ref, w0_ref, b0_ref, o_ref, v_ref):
    v = _layer_norm(_gelu(z_ref[:, GM_W:2 * GM_W]), lg_ref[...], lb_ref[...])
    v_ref[...] = v
    o_ref[...] = _gelu(z_ref[:, :GM_W]) * (v * w0_ref[...] + b0_ref[...])


def _gmlp_first_row(z, ln_g, ln_b, w_s, b_s):
    m = z.shape[0]
    w0 = jnp.repeat(w_s[:, 0, 0], HEAD_DIM).reshape(1, GM_W)
    b0 = jnp.repeat(b_s[:, 0], HEAD_DIM).reshape(1, GM_W)
    return pl.pallas_call(
        _gmlp_first_row_kernel,
        out_shape=(jax.ShapeDtypeStruct((m, GM_W), F32), jax.ShapeDtypeStruct((m, GM_W), F32)),
        compiler_params=pltpu.CompilerParams(vmem_limit_bytes=VMEM_LIMIT),
        name="gmlp_first_row",
    )(z, ln_g.reshape(1, GM_W), ln_b.reshape(1, GM_W), w0, b0)


def _rope_tables(pos):
    half = ROT_DIM // 2
    inv = ROPE_THETA ** (-jnp.arange(half, dtype=F32) / half)
    ang = pos.astype(F32)[:, None] * inv[None, :]
    cos, sin = jnp.cos(ang), jnp.sin(ang)
    t = pos.shape[0]
    one = jnp.ones((t, HEAD_DIM - ROT_DIM), F32)
    zero = jnp.zeros((t, HEAD_DIM - ROT_DIM), F32)
    zh = jnp.zeros((t, half), F32)
    c = jnp.concatenate([cos, cos, one], axis=1)
    s1 = jnp.concatenate([-sin, zh, zero], axis=1)
    s2 = jnp.concatenate([zh, sin, zero], axis=1)
    return jnp.stack([c, s1, s2])


def _nsa_kv_post_kernel(h_ref, rope_ref, gs_ref, gw_ref, kc_ref, vc_ref, ks_ref, vs_ref, kw_ref, vw_ref):
    c, s1, s2 = rope_ref[0], rope_ref[1], rope_ref[2]
    kc_ref[...] = h_ref[:, 0 * KV_W:1 * KV_W]
    vc_ref[...] = h_ref[:, 1 * KV_W:2 * KV_W]
    vs_ref[...] = h_ref[:, 3 * KV_W:4 * KV_W]
    vw_ref[...] = h_ref[:, 5 * KV_W:6 * KV_W]
    for g in range(NSA_KV):
        sl = slice(g * HEAD_DIM, (g + 1) * HEAD_DIM)
        ks_ref[:, sl] = _rope(_rms(h_ref[:, 2 * KV_W + g * HEAD_DIM:2 * KV_W + (g + 1) * HEAD_DIM], gs_ref[...]), c, s1, s2)
        kw_ref[:, sl] = _rope(_rms(h_ref[:, 4 * KV_W + g * HEAD_DIM:4 * KV_W + (g + 1) * HEAD_DIM], gw_ref[...]), c, s1, s2)


def _nsa_kv_post(h, rope_tab, g_ks, g_kw):
    m = h.shape[0]
    tm = _row_tile(m, 512)
    out = jax.ShapeDtypeStruct((m, KV_W), F32)
    ospec = pl.BlockSpec((tm, KV_W), lambda i: (i, 0))
    return pl.pallas_call(
        _nsa_kv_post_kernel,
        grid=(m // tm,),
        in_specs=[
            pl.BlockSpec((tm, 6 * KV_W), lambda i: (i, 0)),
            pl.BlockSpec((3, tm, HEAD_DIM), lambda i: (0, i, 0)),
            pl.BlockSpec((1, HEAD_DIM), lambda i: (0, 0)),
            pl.BlockSpec((1, HEAD_DIM), lambda i: (0, 0)),
        ],
        out_specs=(ospec,) * 6,
        out_shape=(out,) * 6,
        compiler_params=_params("parallel"),
        name="nsa_kv_post",
    )(h, rope_tab, g_ks.reshape(1, HEAD_DIM), g_kw.reshape(1, HEAD_DIM))


def _compress_kernel(pt_ref, page_ref, w1_ref, pe_ref, w2_ref, gk_ref, o_ref, l_scr, a_scr, b_scr,
                     *, n_pages, norm):
    j = pl.program_id(1)
    jj = j % PAGES_PER_GROUP
    n_chunks = n_pages * CHUNKS_PER_PAGE
    group_chunks = PAGES_PER_GROUP * CHUNKS_PER_PAGE

    @pl.when(j == 0)
    def _():
        b_scr[:, n_chunks:, :] = jnp.zeros((NSA_KV, SUBLANE, HEAD_DIM), F32)

    row0 = pl.multiple_of(jj * CHUNKS_PER_PAGE, CHUNKS_PER_PAGE)
    for p in range(CMP_STRIDE):
        for g in range(NSA_KV):
            l_scr[g, pl.ds(row0, CHUNKS_PER_PAGE), p * HEAD_DIM:(p + 1) * HEAD_DIM] = (
                page_ref[0, pl.ds(p * NSA_KV + g, CHUNKS_PER_PAGE, stride=CMP_STRIDE * NSA_KV), :])

    @pl.when(jj == PAGES_PER_GROUP - 1)
    def _():
        base = pl.multiple_of((j // PAGES_PER_GROUP) * group_chunks, group_chunks)
        for g in range(NSA_KV):
            r = _dot(l_scr[g], w1_ref[g])
            a_scr[g, pl.ds(base, group_chunks), :] = r[:, :HEAD_DIM]
            b_scr[g, pl.ds(base, group_chunks), :] = r[:, HEAD_DIM:]

    @pl.when(j == n_pages - 1)
    def _():
        for g in range(NSA_KV):
            t = _dot(pe_ref[g], w1_ref[g])
            bias = t[0:1, :HEAD_DIM] + t[1:2, HEAD_DIM:]
            h = a_scr[g] + b_scr[g, pl.ds(1, n_chunks), :] + bias
            y = _dot(_gelu(h), w2_ref[g])
            if norm:
                y = _rms(y, gk_ref[...])
            o_ref[0, :, g * HEAD_DIM:(g + 1) * HEAD_DIM] = y


def _compress(pool, page_table, pos_enc, w1, w2, g_norm):
    nb, n_pages = page_table.shape
    assert n_pages % PAGES_PER_GROUP == 0
    n_chunks = n_pages * CHUNKS_PER_PAGE
    kdim = CMP_STRIDE * HEAD_DIM
    w1ab = jnp.concatenate([w1[:, :CMP_STRIDE].reshape(NSA_KV, kdim, HEAD_DIM),
                            w1[:, CMP_STRIDE:].reshape(NSA_KV, kdim, HEAD_DIM)], axis=-1).astype(BF16)
    pe = jnp.pad(pos_enc.reshape(NSA_KV, 2, kdim), ((0, 0), (0, SUBLANE - 2), (0, 0)))
    norm = g_norm is not None
    gk = (g_norm if norm else jnp.ones((HEAD_DIM,), F32)).reshape(1, HEAD_DIM)
    grid_spec = pltpu.PrefetchScalarGridSpec(
        num_scalar_prefetch=1,
        grid=(nb, n_pages),
        in_specs=[
            pl.BlockSpec((1, PAGE_SIZE * NSA_KV, HEAD_DIM), lambda b, j, pt: (pt[b, j], 0, 0)),
            pl.BlockSpec((NSA_KV, kdim, 2 * HEAD_DIM), lambda b, j, pt: (0, 0, 0)),
            pl.BlockSpec((NSA_KV, SUBLANE, kdim), lambda b, j, pt: (0, 0, 0)),
            pl.BlockSpec((NSA_KV, HEAD_DIM, HEAD_DIM), lambda b, j, pt: (0, 0, 0)),
            pl.BlockSpec((1, HEAD_DIM), lambda b, j, pt: (0, 0)),
        ],
        out_specs=pl.BlockSpec((1, n_chunks, KV_W), lambda b, j, pt: (b, 0, 0)),
        scratch_shapes=[
            pltpu.VMEM((NSA_KV, PAGES_PER_GROUP * CHUNKS_PER_PAGE, kdim), F32),
            pltpu.VMEM((NSA_KV, n_chunks, HEAD_DIM), F32),
            pltpu.VMEM((NSA_KV, n_chunks + SUBLANE, HEAD_DIM), F32),
        ],
    )
    return pl.pallas_call(
        partial(_compress_kernel, n_pages=n_pages, norm=norm),
        grid_spec=grid_spec,
        out_shape=jax.ShapeDtypeStruct((nb, n_chunks, KV_W), F32),
        compiler_params=_params("parallel", "arbitrary"),
        name="compress",
    )(page_table, pool.reshape(pool.shape[0], PAGE_SIZE * NSA_KV, HEAD_DIM), w1ab, pe, w2, gk)


def _gate_columns(sig, g, rows):
    lane = lax.broadcasted_iota(jnp.int32, (rows, LANE), 1)
    base = g * (NSA_REP * N_GATES)
    return [[jnp.sum(jnp.where(lane == base + r * N_GATES + t, sig, 0.0), axis=1, keepdims=True)
             for t in range(N_GATES)] for r in range(NSA_REP)]


def _top_mask_t(score_t, avail_t, n_rows):
    jrow = lax.broadcasted_iota(jnp.int32, score_t.shape, 0)
    cnt = jnp.zeros(score_t.shape, F32)
    for i in range(n_rows):
        si = score_t[i:i + 1, :]
        beats = (si > score_t) | ((si == score_t) & (jrow > i))
        cnt = cnt + jnp.where(beats, 1.0, 0.0)
    return jnp.where((cnt < N_SELECT) & avail_t, 1.0, 0.0)


def _nsa_prompt_kernel(q_ref, gate_ref, gq_ref, rope_ref, kc_ref, vc_ref, ks_ref, vs_ref, kw_ref, vw_ref,
                       o_ref, m_scr, l_scr, acc_scr, *, n_sel, n_cmp):
    g = pl.program_id(1)
    qb = pl.program_id(2)
    t0 = qb * QB
    rq = NSA_REP * QB
    c, s1, s2 = rope_ref[0], rope_ref[1], rope_ref[2]
    gq = gq_ref[...]
    qn_l, qr_l = [], []
    for r in range(NSA_REP):
        qn_r = _rms(q_ref[0, :, r * HEAD_DIM:(r + 1) * HEAD_DIM], gq)
        qn_l.append(qn_r)
        qr_l.append(_rope(qn_r, c, s1, s2))
    qn = jnp.concatenate(qn_l, axis=0)
    qr = jnp.concatenate(qr_l, axis=0).astype(BF16)
    pos1 = t0 + lax.broadcasted_iota(jnp.int32, (QB, 1), 0)
    pos3 = jnp.concatenate([pos1] * NSA_REP, axis=0)

    s = _dot_nt(qn, kc_ref[0]) * SCALE
    ci = lax.broadcasted_iota(jnp.int32, (1, n_cmp), 1)
    p = _masked_softmax(s, ci * CMP_STRIDE + (CMP_LEN - 1) <= pos3)
    o_c = _dot(p, vc_ref[0])
    psum = p[0:QB] + p[QB:2 * QB] + p[2 * QB:3 * QB]

    ci_l = lax.broadcasted_iota(jnp.int32, (n_sel, n_cmp), 1) * CMP_STRIDE
    sj = lax.broadcasted_iota(jnp.int32, (n_sel, n_cmp), 0) * SEL_BLOCK
    cover_t = jnp.where((ci_l < sj + SEL_BLOCK) & (ci_l + CMP_LEN > sj), 1.0, 0.0)
    imp_t = lax.dot_general(cover_t, psum, (((1,), (1,)), ((), ())),
                            precision=lax.Precision.HIGHEST, preferred_element_type=F32)
    post = t0 + lax.broadcasted_iota(jnp.int32, (n_sel, QB), 1)
    jt = lax.broadcasted_iota(jnp.int32, (n_sel, QB), 0)
    cur = post >> SEL_SHIFT
    avail_t = jt * SEL_BLOCK <= post
    forced_t = (jt == 0) | (jt == cur) | (jt == cur - 1)
    score_t = jnp.where(avail_t, jnp.where(forced_t, FORCED_SCORE, imp_t), NEG)
    sel_t = _top_mask_t(score_t, avail_t, n_sel)
    sel_t = jnp.concatenate([sel_t, jnp.zeros((LANE - n_sel, QB), F32)], axis=0)
    sel = sel_t.T.astype(BF16)

    def flash_init():
        m_scr[...] = jnp.full((rq, 1), NEG, F32)
        l_scr[...] = jnp.zeros((rq, 1), F32)
        acc_scr[...] = jnp.zeros((rq, HEAD_DIM), F32)

    def flash_step(k, v, msk):
        sc = jnp.where(msk, _dot_nt(qr, k) * SCALE, NEG)
        m_old = m_scr[...]
        m_new = jnp.maximum(m_old, jnp.max(sc, axis=-1, keepdims=True))
        alpha = jnp.exp(m_old - m_new)
        e = jnp.where(msk, jnp.exp(sc - m_new), 0.0)
        l_scr[...] = alpha * l_scr[...] + jnp.sum(e, axis=-1, keepdims=True)
        acc_scr[...] = alpha * acc_scr[...] + _dot(e, v)
        m_scr[...] = m_new

    flash_init()
    blocks_per_chunk = SEL_KEY_CHUNK // SEL_BLOCK

    def sel_body(ck, carry):
        k0 = pl.multiple_of(ck * SEL_KEY_CHUNK, SEL_KEY_CHUNK)
        jr = lax.broadcasted_iota(jnp.int32, (LANE, SEL_KEY_CHUNK), 0)
        kl = lax.broadcasted_iota(jnp.int32, (LANE, SEL_KEY_CHUNK), 1)
        expand = jnp.where((kl >> SEL_SHIFT) + ck * blocks_per_chunk == jr, 1.0, 0.0).astype(BF16)
        selm = jnp.dot(sel, expand, preferred_element_type=F32)
        selm3 = jnp.concatenate([selm] * NSA_REP, axis=0)
        kpos = k0 + lax.broadcasted_iota(jnp.int32, (1, SEL_KEY_CHUNK), 1)
        flash_step(ks_ref[0, pl.ds(k0, SEL_KEY_CHUNK), :], vs_ref[0, pl.ds(k0, SEL_KEY_CHUNK), :],
                   (selm3 > 0.5) & (kpos <= pos3))
        return carry

    lax.fori_loop(0, (t0 + QB + SEL_KEY_CHUNK - 1) // SEL_KEY_CHUNK, sel_body, 0)
    o_s = acc_scr[...] / l_scr[...]

    flash_init()
    for w in range(WINDOW // QB + 1):
        start = t0 - WINDOW + w * QB
        k0 = pl.multiple_of(jnp.maximum(start, 0), QB)
        kq = start + lax.broadcasted_iota(jnp.int32, (1, QB), 1)
        flash_step(kw_ref[0, pl.ds(k0, QB), :], vw_ref[0, pl.ds(k0, QB), :],
                   (kq >= 0) & (kq >= pos3 - WINDOW) & (kq <= pos3))
    o_w = acc_scr[...] / l_scr[...]

    gate = _gate_columns(jax.nn.sigmoid(gate_ref[0]), g, QB)
    for r in range(NSA_REP):
        rows = slice(r * QB, (r + 1) * QB)
        o_ref[0, :, r * HEAD_DIM:(r + 1) * HEAD_DIM] = (
            gate[r][0] * o_c[rows] + gate[r][1] * o_s[rows] + gate[r][2] * o_w[rows])


def _nsa_prompt(z, gates, g_q, rope_tab, kc, vc, ks, vs, kw, vw):
    b, t, _ = z.shape
    n_cmp = kc.shape[1]
    qw = NSA_REP * HEAD_DIM
    kv_spec = pl.BlockSpec((1, t, HEAD_DIM), lambda i, g, j: (i, 0, g))
    cmp_spec = pl.BlockSpec((1, n_cmp, HEAD_DIM), lambda i, g, j: (i, 0, g))
    return pl.pallas_call(
        partial(_nsa_prompt_kernel, n_sel=t // SEL_BLOCK, n_cmp=n_cmp),
        grid=(b, NSA_KV, t // QB),
        in_specs=[
            pl.BlockSpec((1, QB, qw), lambda i, g, j: (i, j, g)),
            pl.BlockSpec((1, QB, LANE), lambda i, g, j: (i, j, 0)),
            pl.BlockSpec((1, HEAD_DIM), lambda i, g, j: (0, 0)),
            pl.BlockSpec((3, QB, HEAD_DIM), lambda i, g, j: (0, j, 0)),
            cmp_spec, cmp_spec, kv_spec, kv_spec, kv_spec, kv_spec,
        ],
        out_specs=pl.BlockSpec((1, QB, qw), lambda i, g, j: (i, j, g)),
        out_shape=jax.ShapeDtypeStruct((b, t, NSA_W), F32),
        scratch_shapes=[
            pltpu.VMEM((NSA_REP * QB, 1), F32),
            pltpu.VMEM((NSA_REP * QB, 1), F32),
            pltpu.VMEM((NSA_REP * QB, HEAD_DIM), F32),
        ],
        compiler_params=_params("parallel", "parallel", "arbitrary"),
        name="nsa_prompt",
    )(z, gates, g_q.reshape(1, HEAD_DIM), rope_tab, kc, vc, ks, vs, kw, vw)


def _nsa_sample_select_kernel(q_ref, gq_ref, rope_ref, kc_ref, vc_ref, kw_ref, vw_ref,
                              qr_ref, oc_ref, ow_ref, idx_ref, valid_ref, *, pos, n_sel, n_cmp, n_win):
    n_cand = SEL_CAND_LANES
    c, s1, s2 = rope_ref[0:1], rope_ref[1:2], rope_ref[2:3]
    row = lax.broadcasted_iota(jnp.int32, (SUBLANE, HEAD_DIM), 0)
    q3 = [jnp.broadcast_to(q_ref[0, :, r * HEAD_DIM:(r + 1) * HEAD_DIM], (SUBLANE, HEAD_DIM))
          for r in range(NSA_REP)]
    q8 = jnp.where(row == 0, q3[0], jnp.where(row == 1, q3[1], q3[2]))
    qn = _rms(q8, gq_ref[...])
    qr = _rope(qn, c, s1, s2)
    qr_ref[0, 0] = qr

    s = _dot_nt(qn, kc_ref[0]) * SCALE
    ci = lax.broadcasted_iota(jnp.int32, (1, n_cmp), 1)
    p = _masked_softmax(s, ci * CMP_STRIDE + (CMP_LEN - 1) <= pos)
    oc_ref[0, 0] = _dot(p, vc_ref[0])
    psum = p[0:1] + p[1:2] + p[2:3]

    ci_s = lax.broadcasted_iota(jnp.int32, (n_cmp, n_cand), 0) * CMP_STRIDE
    sj = lax.broadcasted_iota(jnp.int32, (n_cmp, n_cand), 1) * SEL_BLOCK
    cover = jnp.where((ci_s < sj + SEL_BLOCK) & (ci_s + CMP_LEN > sj), 1.0, 0.0)
    imp = jnp.dot(jnp.broadcast_to(psum, (SUBLANE, n_cmp)), cover,
                  precision=lax.Precision.HIGHEST, preferred_element_type=F32)[0:1]
    jl = lax.broadcasted_iota(jnp.int32, (1, n_cand), 1)
    cur = pos // SEL_BLOCK
    avail = jl * SEL_BLOCK <= pos
    forced = (jl == 0) | (jl == cur) | (jl == cur - 1)
    score = jnp.where(avail, jnp.where(forced, FORCED_SCORE, imp), NEG)
    score = jnp.where(jl < n_sel, score, EXCLUDED)

    eye = (lax.broadcasted_iota(jnp.int32, (n_cand, n_cand), 0)
           == lax.broadcasted_iota(jnp.int32, (n_cand, n_cand), 1))
    score_b = jnp.broadcast_to(score, (n_cand, n_cand))
    score_c = jnp.sum(jnp.where(eye, score_b, 0.0), axis=1, keepdims=True)
    il = lax.broadcasted_iota(jnp.int32, (n_cand, n_cand), 1)
    jc = lax.broadcasted_iota(jnp.int32, (n_cand, n_cand), 0)
    beats = (score_b > score_c) | ((score_b == score_c) & (il < jc))
    rank = jnp.sum(jnp.where(beats, 1.0, 0.0), axis=1, keepdims=True)
    kl = lax.broadcasted_iota(jnp.int32, (n_cand, LANE), 1).astype(F32)
    hit = rank == kl
    jcol = lax.broadcasted_iota(jnp.int32, (n_cand, LANE), 0)
    idx_ref[0, 0] = jnp.sum(jnp.where(hit, jcol.astype(F32), 0.0), axis=0, keepdims=True).astype(jnp.int32)
    valid_ref[0, 0] = jnp.sum(jnp.where(hit & (jcol * SEL_BLOCK <= pos), 1.0, 0.0),
                              axis=0, keepdims=True).astype(jnp.int32)

    n_buf = kw_ref.shape[1]
    kpos = pos + 1 - n_win + lax.broadcasted_iota(jnp.int32, (1, n_buf), 1)
    sw = _dot_nt(qr, kw_ref[0]) * SCALE
    pw = _masked_softmax(sw, (kpos >= pos - WINDOW) & (kpos <= pos))
    ow_ref[0, 0] = _dot(pw, vw_ref[0])


def _nsa_sample_gather_kernel(idx_ref, valid_ref, pt_ref, kp_ref, vp_ref, kn_ref, vn_ref, qr_ref, oc_ref, ow_ref,
                              gate_ref, o_ref, m_scr, l_scr, acc_scr, *, pos, n_past):
    b, g, k = pl.program_id(0), pl.program_id(1), pl.program_id(2)
    idx = idx_ref[b, g, k]

    @pl.when(k == 0)
    def _():
        m_scr[...] = jnp.full((SUBLANE, 1), NEG, F32)
        l_scr[...] = jnp.zeros((SUBLANE, 1), F32)
        acc_scr[...] = jnp.zeros((SUBLANE, HEAD_DIM), F32)

    row = lax.broadcasted_iota(jnp.int32, (SEL_BLOCK, HEAD_DIM), 0)
    in_past = idx < n_past
    k_blk = jnp.where(in_past, kp_ref[0], jnp.where(row == 0, kn_ref[0], 0.0))
    v_blk = jnp.where(in_past, vp_ref[0], jnp.where(row == 0, vn_ref[0], 0.0))
    kpos = idx * SEL_BLOCK + lax.broadcasted_iota(jnp.int32, (1, SEL_BLOCK), 1)
    msk = kpos <= jnp.where(valid_ref[b, g, k] > 0, pos, -1)
    sc = jnp.where(msk, _dot_nt(qr_ref[0, 0], k_blk) * SCALE, NEG)
    m_old = m_scr[...]
    m_new = jnp.maximum(m_old, jnp.max(sc, axis=-1, keepdims=True))
    alpha = jnp.exp(m_old - m_new)
    e = jnp.where(msk, jnp.exp(sc - m_new), 0.0)
    l_scr[...] = alpha * l_scr[...] + jnp.sum(e, axis=-1, keepdims=True)
    acc_scr[...] = alpha * acc_scr[...] + _dot(e, v_blk)
    m_scr[...] = m_new

    @pl.when(k == pl.num_programs(2) - 1)
    def _():
        o_s = acc_scr[...] / l_scr[...]
        gate = _gate_columns(jax.nn.sigmoid(gate_ref[0]), g, 1)
        o_c, o_w = oc_ref[0, 0], ow_ref[0, 0]
        for r in range(NSA_REP):
            o_ref[0, :, r * HEAD_DIM:(r + 1) * HEAD_DIM] = (
                gate[r][0] * o_c[r:r + 1] + gate[r][1] * o_s[r:r + 1] + gate[r][2] * o_w[r:r + 1])


def _nsa_sample(zq, zgate, g_q, rope_row, kc, vc, kw_buf, vw_buf, n_win, pool_k, pool_v, k_new, v_new,
                page_table, pos):
    db = zq.shape[0]
    n_cmp = kc.shape[1]
    n_buf = kw_buf.shape[1]
    n_past = page_table.shape[1] * (PAGE_SIZE // SEL_BLOCK)
    n_sel = n_past + 1
    assert n_sel <= SEL_CAND_LANES
    qw = NSA_REP * HEAD_DIM
    vec = jax.ShapeDtypeStruct((db, NSA_KV, SUBLANE, HEAD_DIM), F32)
    ivec = jax.ShapeDtypeStruct((db, NSA_KV, 1, LANE), jnp.int32)
    vspec = pl.BlockSpec((1, 1, SUBLANE, HEAD_DIM), lambda i, g: (i, g, 0, 0))
    ispec = pl.BlockSpec((1, 1, 1, LANE), lambda i, g: (i, g, 0, 0))
    qr, o_c, o_w, idx, valid = pl.pallas_call(
        partial(_nsa_sample_select_kernel, pos=pos, n_sel=n_sel, n_cmp=n_cmp, n_win=n_win),
        grid=(db, NSA_KV),
        in_specs=[
            pl.BlockSpec((1, 1, qw), lambda i, g: (i, 0, g)),
            pl.BlockSpec((1, HEAD_DIM), lambda i, g: (0, 0)),
            pl.BlockSpec((3, HEAD_DIM), lambda i, g: (0, 0)),
            pl.BlockSpec((1, n_cmp, HEAD_DIM), lambda i, g: (i, 0, g)),
            pl.BlockSpec((1, n_cmp, HEAD_DIM), lambda i, g: (i, 0, g)),
            pl.BlockSpec((1, n_buf, HEAD_DIM), lambda i, g: (i, 0, g)),
            pl.BlockSpec((1, n_buf, HEAD_DIM), lambda i, g: (i, 0, g)),
        ],
        out_specs=(vspec, vspec, vspec, ispec, ispec),
        out_shape=(vec, vec, vec, ivec, ivec),
        compiler_params=_params("parallel", "parallel"),
        name="nsa_sample_select",
    )(zq, g_q.reshape(1, HEAD_DIM), rope_row, kc, vc, kw_buf, vw_buf)

    idx = idx[:, :, 0, :N_SELECT]
    valid = valid[:, :, 0, :N_SELECT]
    halves = PAGE_SIZE // SEL_BLOCK

    def pool_block(i, g, k, idx_r, valid_r, pt_r):
        blk = jnp.minimum(idx_r[i, g, k], n_past - 1)
        return (pt_r[i, blk // halves] * halves + blk % halves, 0, g)

    n_pool = pool_k.shape[0]
    gvec = pl.BlockSpec((1, 1, SUBLANE, HEAD_DIM), lambda i, g, k, *_: (i, g, 0, 0))
    new_spec = pl.BlockSpec((1, 1, HEAD_DIM), lambda i, g, k, *_: (i, 0, g))
    grid_spec = pltpu.PrefetchScalarGridSpec(
        num_scalar_prefetch=3,
        grid=(db, NSA_KV, N_SELECT),
        in_specs=[
            pl.BlockSpec((1, SEL_BLOCK, HEAD_DIM), pool_block),
            pl.BlockSpec((1, SEL_BLOCK, HEAD_DIM), pool_block),
            new_spec, new_spec, gvec, gvec, gvec,
            pl.BlockSpec((1, 1, LANE), lambda i, g, k, *_: (i, 0, 0)),
        ],
        out_specs=pl.BlockSpec((1, 1, qw), lambda i, g, k, *_: (i, 0, g)),
        scratch_shapes=[
            pltpu.VMEM((SUBLANE, 1), F32),
            pltpu.VMEM((SUBLANE, 1), F32),
            pltpu.VMEM((SUBLANE, HEAD_DIM), F32),
        ],
    )
    return pl.pallas_call(
        partial(_nsa_sample_gather_kernel, pos=pos, n_past=n_past),
        grid_spec=grid_spec,
        out_shape=jax.ShapeDtypeStruct((db, 1, NSA_W), F32),
        compiler_params=_params("parallel", "parallel", "arbitrary"),
        name="nsa_sample_gather",
    )(idx, valid, page_table,
      pool_k.reshape(n_pool * halves, SEL_BLOCK, KV_W), pool_v.reshape(n_pool * halves, SEL_BLOCK, KV_W),
      k_new, v_new, qr, o_c, o_w, zgate)


def kernel(x_prompt, x_sample, mem_prompt, cache_mem_k, cache_mem_v, cache_k_cmp, cache_v_cmp, cache_k_sel, cache_v_sel, state_k_win, state_v_win, page_table, g_ffn1, w_ffn1_gate, w_ffn1_up, w_ffn1_down, g_mix, g_ffn2, w_ffn2_gate, w_ffn2_up, w_ffn2_down, g_mem, w_mem_kv, g_mq, g_mk, w_in_a, ln_v_g, ln_v_b, w_spatial, b_spatial, w_out_a, g_kv, w_kv, pos_kc, w1_kc, w2_kc, pos_vc, w1_vc, w2_vc, g_kc, g_ks, g_kw, w_in_b, g_q, w_out_b):
    nb, t, d = x_prompt.shape
    db, ds, _ = x_sample.shape
    depth = g_ffn1.shape[0]
    n_a = w_in_a.shape[0]
    past_len = page_table.shape[1] * PAGE_SIZE
    assert ds == 1 and t % PAGE_SIZE == 0
    pos_s = past_len

    xp = x_prompt.reshape(nb * t, d)
    xs = x_sample.reshape(db * ds, d)
    mem = mem_prompt.reshape(nb * MEM_LEN, d)
    mem_k_list, mem_v_list, gm_v_list = [], [], []
    rope_p = _rope_tables(jnp.arange(t, dtype=jnp.int32))
    rope_s = _rope_tables(pos_s + jnp.arange(ds, dtype=jnp.int32))
    rope_p2 = jnp.tile(rope_p, (1, nb, 1))

    for l in range(depth):
        if l == n_a:
            hp = _rms_mm(xp, g_kv, w_kv)
            hs = _rms_mm(xs, g_kv, w_kv)
            kc_p_raw, vc_p_raw, ks_p, vs_p, kw_p, vw_p = _nsa_kv_post(hp, rope_p2, g_ks, g_kw)
            kc_s_raw, vc_s_raw, ks_s, vs_s, kw_s, vw_s = _nsa_kv_post(
                hs, jnp.broadcast_to(rope_s, (3, db, HEAD_DIM)), g_ks, g_kw)
            pages_p = jnp.arange(nb * t // PAGE_SIZE, dtype=jnp.int32).reshape(nb, t // PAGE_SIZE)
            kc_p = _compress(kc_p_raw.reshape(-1, PAGE_SIZE, KV_W), pages_p, pos_kc, w1_kc, w2_kc, g_kc)
            vc_p = _compress(vc_p_raw.reshape(-1, PAGE_SIZE, KV_W), pages_p, pos_vc, w1_vc, w2_vc, None)
            n_pool = cache_k_cmp.shape[0]
            kc_s = _compress(cache_k_cmp.reshape(n_pool, PAGE_SIZE, KV_W), page_table, pos_kc, w1_kc, w2_kc, g_kc)
            vc_s = _compress(cache_v_cmp.reshape(n_pool, PAGE_SIZE, KV_W), page_table, pos_vc, w1_vc, w2_vc, None)
            wbuf = state_k_win.shape[1]
            pad = (-(wbuf + ds)) % SUBLANE
            kw_buf = jnp.concatenate([state_k_win.reshape(db, wbuf, KV_W), kw_s.reshape(db, ds, KV_W),
                                      jnp.zeros((db, pad, KV_W), F32)], axis=1)
            vw_buf = jnp.concatenate([state_v_win.reshape(db, wbuf, KV_W), vw_s.reshape(db, ds, KV_W),
                                      jnp.zeros((db, pad, KV_W), F32)], axis=1)

        xp = _ffn(xp, g_ffn1[l], w_ffn1_gate[l], w_ffn1_up[l], w_ffn1_down[l])
        xs = _ffn(xs, g_ffn1[l], w_ffn1_gate[l], w_ffn1_up[l], w_ffn1_down[l])

        mk_p, mv_p = _mem_kv_post(_rms_mm(mem, g_mem[l], w_mem_kv[l]), g_mk[l])
        mem_k_list.append(mk_p.reshape(nb, MEM_LEN, MEM_HEADS, HEAD_DIM))
        mem_v_list.append(mv_p.reshape(nb, MEM_LEN, MEM_HEADS, HEAD_DIM))
        mk_p = mk_p.reshape(nb, MEM_LEN, MEM_W)
        mv_p = mv_p.reshape(nb, MEM_LEN, MEM_W)
        mk_s = cache_mem_k[l].reshape(db, MEM_LEN, MEM_W)
        mv_s = cache_mem_v[l].reshape(db, MEM_LEN, MEM_W)

        if l < n_a:
            zp = _rms_mm(xp, g_mix[l], w_in_a[l])
            zs = _rms_mm(xs, g_mix[l], w_in_a[l])
            mem_blk = 2 * GM_W // MEM_W
            o1_p = _gmlp_prompt(zp, ln_v_g[l], ln_v_b[l], w_spatial[l], b_spatial[l])
            o1_s, v_s = _gmlp_first_row(zs, ln_v_g[l], ln_v_b[l], w_spatial[l], b_spatial[l])
            gm_v_list.append(v_s.reshape(db, ds, GM_W))
            o2_p = _mem_attend(zp.reshape(nb, t, -1), mem_blk, mk_p, mv_p, g_mq[l])
            o2_s = _mem_attend(zs.reshape(db, ds, -1), mem_blk, mk_s, mv_s, g_mq[l])
            w_out = w_out_a[l]
        else:
            lb = l - n_a
            n_gate = N_GATES * NSA_HEADS
            w_qm = jnp.concatenate([w_in_b[lb][:, :NSA_W], w_in_b[lb][:, NSA_W + n_gate:]], axis=1)
            w_gate = jnp.pad(w_in_b[lb][:, NSA_W:NSA_W + n_gate], ((0, 0), (0, LANE - n_gate)))
            zp = _rms_mm(xp, g_mix[l], w_qm).reshape(nb, t, -1)
            zs = _rms_mm(xs, g_mix[l], w_qm).reshape(db, ds, -1)
            gp = _rms_mm(xp, g_mix[l], w_gate).reshape(nb, t, LANE)
            gs = _rms_mm(xs, g_mix[l], w_gate).reshape(db, ds, LANE)
            o1_p = _nsa_prompt(zp, gp, g_q[lb], rope_p,
                               kc_p, vc_p, ks_p.reshape(nb, t, KV_W), vs_p.reshape(nb, t, KV_W),
                               kw_p.reshape(nb, t, KV_W), vw_p.reshape(nb, t, KV_W)).reshape(nb * t, NSA_W)
            n_pool = cache_k_sel.shape[0]
            o1_s = _nsa_sample(zs, gs, g_q[lb], rope_s[:, 0, :], kc_s, vc_s, kw_buf, vw_buf, wbuf + ds,
                               cache_k_sel.reshape(n_pool, PAGE_SIZE, KV_W),
                               cache_v_sel.reshape(n_pool, PAGE_SIZE, KV_W),
                               ks_s.reshape(db, ds, KV_W), vs_s.reshape(db, ds, KV_W),
                               page_table, pos_s).reshape(db * ds, NSA_W)
            mem_blk = NSA_W // MEM_W
            o2_p = _mem_attend(zp, mem_blk, mk_p, mv_p, g_mq[l])
            o2_s = _mem_attend(zs, mem_blk, mk_s, mv_s, g_mq[l])
            w_out = w_out_b[lb]
        xp = _out_proj(o1_p, o2_p.reshape(nb * t, MEM_W), w_out, xp)
        xs = _out_proj(o1_s, o2_s.reshape(db * ds, MEM_W), w_out, xs)

        xp = _ffn(xp, g_ffn2[l], w_ffn2_gate[l], w_ffn2_up[l], w_ffn2_down[l])
        xs = _ffn(xs, g_ffn2[l], w_ffn2_gate[l], w_ffn2_up[l], w_ffn2_down[l])

    wp = min(WINDOW, t)
    ws = min(WINDOW, past_len + ds)
    kv4 = lambda a, n: a.reshape(n, -1, NSA_KV, HEAD_DIM)
    n_win = wbuf + ds
    return (xp.reshape(nb, t, d), xs.reshape(db, ds, d), jnp.stack(mem_k_list), jnp.stack(mem_v_list),
            kv4(kc_p_raw, nb), kv4(vc_p_raw, nb), kv4(ks_p, nb), kv4(vs_p, nb),
            kv4(kw_p, nb)[:, -wp:], kv4(vw_p, nb)[:, -wp:],
            kv4(kc_s_raw, db), kv4(vc_s_raw, db), kv4(ks_s, db), kv4(vs_s, db),
            kv4(kw_buf[:, :n_win], db)[:, -ws:], kv4(vw_buf[:, :n_win], db)[:, -ws:],
            jnp.stack(gm_v_list))
```

```python
from functools import partial

import numpy as np
import jax
import jax.numpy as jnp
from jax import lax
from jax.experimental import pallas as pl
from jax.experimental.pallas import tpu as pltpu

D_MODEL = 2048
HEAD_DIM = 128
ROT_DIM = HEAD_DIM // 4
ROPE_THETA = 500000.0
MEM_LEN = 256
MEM_HEADS = 4
MEM_W = MEM_HEADS * HEAD_DIM
GM_GROUPS = 12
GM_W = GM_GROUPS * HEAD_DIM
CHUNK = 128
NSA_HEADS = 12
NSA_KV = 4
NSA_REP = NSA_HEADS // NSA_KV
NSA_W = NSA_HEADS * HEAD_DIM
KV_W = NSA_KV * HEAD_DIM
N_GATES = 3
CMP_LEN = 32
CMP_STRIDE = 16
SEL_BLOCK = 64
N_SELECT = 16
WINDOW = 512
QB = 128
PAGE_SIZE = 128
EPS = 1e-6
NEG = -1e30
EXCLUDED = -3e38
FORCED_SCORE = 1e6
SCALE = HEAD_DIM ** -0.5
SCALE_LOG2E = float(SCALE * np.log2(np.e))

LANE = 128
SUBLANE = 8
VMEM_LIMIT = 56 * 1024 * 1024
ROW_TILE = 1024
PAGES_PER_GROUP = 16
PAGES_PER_STEP = 4
CHUNKS_PER_PAGE = PAGE_SIZE // CMP_STRIDE
SEL_KEY_CHUNK = 512
SEL_CAND_LANES = 3 * LANE
SEL_SHIFT = SEL_BLOCK.bit_length() - 1

BF16 = jnp.bfloat16
F32 = jnp.float32


def _params(*sem):
    return pltpu.CompilerParams(dimension_semantics=sem, vmem_limit_bytes=VMEM_LIMIT)


def _dot(a, b):
    return jnp.dot(a.astype(BF16), b.astype(BF16), preferred_element_type=F32)


def _dot_nt(a, b):
    return lax.dot_general(a.astype(BF16), b.astype(BF16), (((1,), (1,)), ((), ())),
                           preferred_element_type=F32)


def _rms(x, g):
    return x * lax.rsqrt(jnp.mean(x * x, axis=-1, keepdims=True) + EPS) * g


def _gelu(x):
    return 0.5 * x * (1.0 + lax.erf(x * np.float32(np.sqrt(0.5))))


def _rope(x, c, s1, s2):
    half = ROT_DIM // 2
    return x * c + pltpu.roll(x, LANE - half, 1) * s1 + pltpu.roll(x, half, 1) * s2


def _masked_softmax2(s2, m):
    sm = jnp.where(m, s2, NEG)
    mx = jnp.max(sm, axis=-1, keepdims=True)
    e = jnp.where(m, jnp.exp2(sm - mx), 0.0)
    den = jnp.sum(e, axis=-1, keepdims=True)
    return e / jnp.where(den > 0.0, den, 1.0)


def _with_ones(v):
    return jnp.concatenate([v.astype(BF16), jnp.ones(v.shape, BF16)], axis=1)


def _row_tile(m, want):
    return want if m % want == 0 else m


def _resident(shape, index_map):
    return pl.BlockSpec(shape, index_map, pipeline_mode=pl.Buffered(1))


def _ffn_kernel(x_ref, g_ref, wg_ref, wu_ref, wd_ref, o_ref, h_ref):
    @pl.when(pl.program_id(1) == 0)
    def _():
        x = x_ref[...]
        h_ref[...] = _rms(x, g_ref[...]).astype(BF16)
        o_ref[...] = x

    h = h_ref[...]
    a = jnp.dot(h, wg_ref[...].astype(BF16), preferred_element_type=F32)
    b = jnp.dot(h, wu_ref[...].astype(BF16), preferred_element_type=F32)
    t = a * jax.nn.sigmoid(a) * b
    o_ref[...] += 0.5 * _dot(t, wd_ref[...])


def _ffn(x, g, wg, wu, wd):
    m, d = x.shape
    f = wg.shape[1]
    tm = _row_tile(m, ROW_TILE)
    tf = 256
    return pl.pallas_call(
        _ffn_kernel,
        grid=(m // tm, f // tf),
        in_specs=[
            _resident((tm, d), lambda i, j: (i, 0)),
            pl.BlockSpec((1, d), lambda i, j: (0, 0)),
            pl.BlockSpec((d, tf), lambda i, j: (0, j)),
            pl.BlockSpec((d, tf), lambda i, j: (0, j)),
            pl.BlockSpec((tf, d), lambda i, j: (j, 0)),
        ],
        out_specs=pl.BlockSpec((tm, d), lambda i, j: (i, 0)),
        out_shape=jax.ShapeDtypeStruct((m, d), F32),
        scratch_shapes=[pltpu.VMEM((tm, d), BF16)],
        compiler_params=_params("parallel", "arbitrary"),
        name="ffn",
    )(x, g.reshape(1, d), wg, wu, wd)


def _rms_mm_kernel(x_ref, g_ref, w_ref, o_ref, h_ref):
    @pl.when(pl.program_id(1) == 0)
    def _():
        h_ref[...] = _rms(x_ref[...], g_ref[...]).astype(BF16)

    o_ref[...] = jnp.dot(h_ref[...], w_ref[...].astype(BF16), preferred_element_type=F32)


def _rms_mm(x, g, w):
    m, d = x.shape
    n = w.shape[1]
    tm = _row_tile(m, ROW_TILE)
    tn = _row_tile(n, 512)
    return pl.pallas_call(
        _rms_mm_kernel,
        grid=(m // tm, n // tn),
        in_specs=[
            _resident((tm, d), lambda i, j: (i, 0)),
            pl.BlockSpec((1, d), lambda i, j: (0, 0)),
            pl.BlockSpec((d, tn), lambda i, j: (0, j)),
        ],
        out_specs=pl.BlockSpec((tm, tn), lambda i, j: (i, j)),
        out_shape=jax.ShapeDtypeStruct((m, n), F32),
        scratch_shapes=[pltpu.VMEM((tm, d), BF16)],
        compiler_params=_params("parallel", "arbitrary"),
        name="rms_mm",
    )(x, g.reshape(1, d), w)


def _out_proj_kernel(a1_ref, a2_ref, w1_ref, w2_ref, x_ref, o_ref):
    o_ref[...] = x_ref[...] + (_dot(a1_ref[...], w1_ref[...]) + _dot(a2_ref[...], w2_ref[...]))


def _out_proj(a1, a2, w, x):
    m, k1 = a1.shape
    k2 = a2.shape[1]
    d = w.shape[1]
    tm = _row_tile(m, ROW_TILE)
    tn = 512
    return pl.pallas_call(
        _out_proj_kernel,
        grid=(m // tm, d // tn),
        in_specs=[
            _resident((tm, k1), lambda i, j: (i, 0)),
            _resident((tm, k2), lambda i, j: (i, 0)),
            pl.BlockSpec((k1, tn), lambda i, j: (0, j)),
            pl.BlockSpec((k2, tn), lambda i, j: (k1 // k2, j)),
            pl.BlockSpec((tm, tn), lambda i, j: (i, j)),
        ],
        out_specs=pl.BlockSpec((tm, tn), lambda i, j: (i, j)),
        out_shape=jax.ShapeDtypeStruct((m, d), F32),
        compiler_params=_params("parallel", "arbitrary"),
        name="out_proj",
    )(a1, a2, w, w, x)


def _mem_kv_post_kernel(kv_ref, g_ref, k_ref, v_ref):
    g = g_ref[...]
    for h in range(MEM_HEADS):
        k_ref[:, h, :] = _rms(kv_ref[:, h * HEAD_DIM:(h + 1) * HEAD_DIM], g)
        v_ref[:, h, :] = kv_ref[:, MEM_W + h * HEAD_DIM:MEM_W + (h + 1) * HEAD_DIM]


def _mem_kv_post(kv, g_mk):
    m = kv.shape[0]
    out = jax.ShapeDtypeStruct((m, MEM_HEADS, HEAD_DIM), F32)
    return pl.pallas_call(
        _mem_kv_post_kernel,
        out_shape=(out, out),
        compiler_params=pltpu.CompilerParams(vmem_limit_bytes=VMEM_LIMIT),
        name="mem_kv_post",
    )(kv, g_mk.reshape(1, HEAD_DIM))


def _mem_attend_kernel(q_ref, g_ref, k_ref, v_ref, o_ref, *, rows):
    g = g_ref[...]
    for h in range(MEM_HEADS):
        sl = slice(h * HEAD_DIM, (h + 1) * HEAD_DIM)
        q = q_ref[0, :, sl]
        if rows < SUBLANE:
            q = jnp.broadcast_to(q[0:1], (SUBLANE, HEAD_DIM))
        q = _rms(q, g) * SCALE_LOG2E
        s2 = _dot_nt(q, k_ref[0, :, h, :])
        e = jnp.exp2(s2 - jnp.max(s2, axis=-1, keepdims=True))
        p = e / jnp.sum(e, axis=-1, keepdims=True)
        o = _dot(p, v_ref[0, :, h, :])
        o_ref[0, :, sl] = o[:rows]


def _mem_attend(z, col_block, mk, mv, g_mq):
    b, t, _ = z.shape
    tm = _row_tile(t, 512)
    kv_spec = pl.BlockSpec((1, MEM_LEN, MEM_HEADS, HEAD_DIM), lambda i, j: (i, 0, 0, 0))
    return pl.pallas_call(
        partial(_mem_attend_kernel, rows=tm),
        grid=(b, t // tm),
        in_specs=[
            pl.BlockSpec((1, tm, MEM_W), lambda i, j: (i, j, col_block)),
            pl.BlockSpec((1, HEAD_DIM), lambda i, j: (0, 0)),
            kv_spec, kv_spec,
        ],
        out_specs=pl.BlockSpec((1, tm, MEM_W), lambda i, j: (i, j, 0)),
        out_shape=jax.ShapeDtypeStruct((b, t, MEM_W), F32),
        compiler_params=_params("parallel", "arbitrary"),
        name="mem_attend",
    )(z, g_mq.reshape(1, HEAD_DIM), mk, mv)


def _layer_norm(x, g, b):
    xc = x - jnp.mean(x, axis=-1, keepdims=True)
    var = jnp.mean(xc * xc, axis=-1, keepdims=True)
    return xc * lax.rsqrt(var + EPS) * g + b


def _gmlp_prompt_kernel(z_ref, lg_ref, lb_ref, ws_ref, bs_ref, o_ref):
    v = _layer_norm(_gelu(z_ref[:, GM_W:]), lg_ref[...], lb_ref[...])
    row = lax.broadcasted_iota(jnp.int32, (CHUNK, CHUNK), 0)
    col = lax.broadcasted_iota(jnp.int32, (CHUNK, CHUNK), 1)
    causal = col <= row
    for g in range(GM_GROUPS):
        sl = slice(g * HEAD_DIM, (g + 1) * HEAD_DIM)
        w = jnp.where(causal, ws_ref[g], 0.0)
        sv = _dot(w, v[:, sl]) + bs_ref[:, g:g + 1]
        o_ref[:, sl] = _gelu(z_ref[:, sl]) * sv


def _gmlp_prompt(z, ln_g, ln_b, w_s, b_s):
    m = z.shape[0]
    return pl.pallas_call(
        _gmlp_prompt_kernel,
        grid=(m // CHUNK,),
        in_specs=[
            pl.BlockSpec((CHUNK, 2 * GM_W), lambda i: (i, 0)),
            pl.BlockSpec((1, GM_W), lambda i: (0, 0)),
            pl.BlockSpec((1, GM_W), lambda i: (0, 0)),
            pl.BlockSpec((GM_GROUPS, CHUNK, CHUNK), lambda i: (0, 0, 0)),
            pl.BlockSpec((CHUNK, GM_GROUPS), lambda i: (0, 0)),
        ],
        out_specs=pl.BlockSpec((CHUNK, GM_W), lambda i: (i, 0)),
        out_shape=jax.ShapeDtypeStruct((m, GM_W), F32),
        compiler_params=_params("parallel"),
        name="gmlp_prompt",
    )(z, ln_g.reshape(1, GM_W), ln_b.reshape(1, GM_W), w_s, b_s.T)


def _gmlp_first_row_kernel(z_ref, lg_ref, lb_ref, w0_ref, b0_ref, o_ref, v_ref):
    v = _layer_norm(_gelu(z_ref[:, GM_W:2 * GM_W]), lg_ref[...], lb_ref[...])
    v_ref[...] = v
    o_ref[...] = _gelu(z_ref[:, :GM_W]) * (v * w0_ref[...] + b0_ref[...])


def _gmlp_first_row(z, ln_g, ln_b, w_s, b_s):
    m = z.shape[0]
    w0 = jnp.repeat(w_s[:, 0, 0], HEAD_DIM).reshape(1, GM_W)
    b0 = jnp.repeat(b_s[:, 0], HEAD_DIM).reshape(1, GM_W)
    return pl.pallas_call(
        _gmlp_first_row_kernel,
        out_shape=(jax.ShapeDtypeStruct((m, GM_W), F32), jax.ShapeDtypeStruct((m, GM_W), F32)),
        compiler_params=pltpu.CompilerParams(vmem_limit_bytes=VMEM_LIMIT),
        name="gmlp_first_row",
    )(z, ln_g.reshape(1, GM_W), ln_b.reshape(1, GM_W), w0, b0)


def _rope_tables(pos):
    half = ROT_DIM // 2
    inv = ROPE_THETA ** (-jnp.arange(half, dtype=F32) / half)
    ang = pos.astype(F32)[:, None] * inv[None, :]
    cos, sin = jnp.cos(ang), jnp.sin(ang)
    t = pos.shape[0]
    one = jnp.ones((t, HEAD_DIM - ROT_DIM), F32)
    zero = jnp.zeros((t, HEAD_DIM - ROT_DIM), F32)
    zh = jnp.zeros((t, half), F32)
    c = jnp.concatenate([cos, cos, one], axis=1)
    s1 = jnp.concatenate([-sin, zh, zero], axis=1)
    s2 = jnp.concatenate([zh, sin, zero], axis=1)
    return jnp.stack([c, s1, s2])


def _nsa_kv_post_kernel(h_ref, rope_ref, gs_ref, gw_ref, kc_ref, vc_ref, ks_ref, vs_ref, kw_ref, vw_ref,
                        ks16_ref, vs16_ref, kw16_ref, vw16_ref):
    c, s1, s2 = rope_ref[0], rope_ref[1], rope_ref[2]

    def part(i, g):
        return h_ref[:, i * KV_W + g * HEAD_DIM:i * KV_W + (g + 1) * HEAD_DIM]

    for g in range(NSA_KV):
        sl = slice(g * HEAD_DIM, (g + 1) * HEAD_DIM)
        kc_ref[:, g, :] = part(0, g)
        vc_ref[:, g, :] = part(1, g)
        ks = _rope(_rms(part(2, g), gs_ref[...]), c, s1, s2)
        kw = _rope(_rms(part(4, g), gw_ref[...]), c, s1, s2)
        ks_ref[:, g, :] = ks
        vs_ref[:, g, :] = part(3, g)
        kw_ref[:, g, :] = kw
        vw_ref[:, g, :] = part(5, g)
        ks16_ref[:, sl] = ks.astype(BF16)
        vs16_ref[:, sl] = part(3, g).astype(BF16)
        kw16_ref[:, sl] = kw.astype(BF16)
        vw16_ref[:, sl] = part(5, g).astype(BF16)


def _nsa_kv_post(h, rope_tab, g_ks, g_kw):
    m = h.shape[0]
    tm = _row_tile(m, 512)
    out = jax.ShapeDtypeStruct((m, NSA_KV, HEAD_DIM), F32)
    ospec = pl.BlockSpec((tm, NSA_KV, HEAD_DIM), lambda i: (i, 0, 0))
    out16 = jax.ShapeDtypeStruct((m, KV_W), BF16)
    ospec16 = pl.BlockSpec((tm, KV_W), lambda i: (i, 0))
    return pl.pallas_call(
        _nsa_kv_post_kernel,
        grid=(m // tm,),
        in_specs=[
            pl.BlockSpec((tm, 6 * KV_W), lambda i: (i, 0)),
            pl.BlockSpec((3, tm, HEAD_DIM), lambda i: (0, i, 0)),
            pl.BlockSpec((1, HEAD_DIM), lambda i: (0, 0)),
            pl.BlockSpec((1, HEAD_DIM), lambda i: (0, 0)),
        ],
        out_specs=(ospec,) * 6 + (ospec16,) * 4,
        out_shape=(out,) * 6 + (out16,) * 4,
        compiler_params=_params("parallel"),
        name="nsa_kv_post",
    )(h, rope_tab, g_ks.reshape(1, HEAD_DIM), g_kw.reshape(1, HEAD_DIM))


def _compress_kernel(pt_ref, *refs, n_pages):
    n_in = 2 * PAGES_PER_STEP
    page_refs = refs[:n_in]
    w1_refs, pe_refs, w2_refs = refs[n_in:n_in + 2], refs[n_in + 2:n_in + 4], refs[n_in + 4:n_in + 6]
    gk_ref = refs[n_in + 6]
    o_refs = refs[n_in + 7:n_in + 9]
    l_scr, a_scr, b_scr = refs[n_in + 9:]
    j = pl.program_id(1)
    steps_per_group = PAGES_PER_GROUP // PAGES_PER_STEP
    jj = j % steps_per_group
    n_chunks = n_pages * CHUNKS_PER_PAGE
    group_chunks = PAGES_PER_GROUP * CHUNKS_PER_PAGE

    @pl.when(j == 0)
    def _():
        b_scr[:, :, n_chunks:, :] = jnp.zeros((2, NSA_KV, SUBLANE, HEAD_DIM), F32)

    for t in range(2):
        for q in range(PAGES_PER_STEP):
            row0 = pl.multiple_of((jj * PAGES_PER_STEP + q) * CHUNKS_PER_PAGE, CHUNKS_PER_PAGE)
            page = page_refs[t * PAGES_PER_STEP + q]
            for p in range(CMP_STRIDE):
                for g in range(NSA_KV):
                    l_scr[t, g, pl.ds(row0, CHUNKS_PER_PAGE), p * HEAD_DIM:(p + 1) * HEAD_DIM] = (
                        page[0, pl.ds(p, CHUNKS_PER_PAGE, stride=CMP_STRIDE), g, :])

    @pl.when(jj == steps_per_group - 1)
    def _():
        base = pl.multiple_of((j // steps_per_group) * group_chunks, group_chunks)
        for t in range(2):
            for g in range(NSA_KV):
                r = _dot(l_scr[t, g], w1_refs[t][g])
                a_scr[t, g, pl.ds(base, group_chunks), :] = r[:, :HEAD_DIM]
                b_scr[t, g, pl.ds(base, group_chunks), :] = r[:, HEAD_DIM:]

    @pl.when(j == pl.num_programs(1) - 1)
    def _():
        for t in range(2):
            for g in range(NSA_KV):
                pw = _dot(pe_refs[t][g], w1_refs[t][g])
                bias = pw[0:1, :HEAD_DIM] + pw[1:2, HEAD_DIM:]
                h = a_scr[t, g] + b_scr[t, g, pl.ds(1, n_chunks), :] + bias
                y = _dot(_gelu(h), w2_refs[t][g])
                if t == 0:
                    y = _rms(y, gk_ref[...])
                o_refs[t][0, :, g * HEAD_DIM:(g + 1) * HEAD_DIM] = y


def _compress(pool_k, pool_v, page_table, pos_k, w1_k, w2_k, pos_v, w1_v, w2_v, g_kc):
    nb, n_pages = page_table.shape
    assert n_pages % PAGES_PER_GROUP == 0
    n_chunks = n_pages * CHUNKS_PER_PAGE
    kdim = CMP_STRIDE * HEAD_DIM

    def w1ab(w1):
        return jnp.concatenate([w1[:, :CMP_STRIDE].reshape(NSA_KV, kdim, HEAD_DIM),
                                w1[:, CMP_STRIDE:].reshape(NSA_KV, kdim, HEAD_DIM)], axis=-1).astype(BF16)

    def pe8(pos_enc):
        return jnp.pad(pos_enc.reshape(NSA_KV, 2, kdim), ((0, 0), (0, SUBLANE - 2), (0, 0)))

    def page_spec(q):
        return pl.BlockSpec((1, PAGE_SIZE, NSA_KV, HEAD_DIM),
                            lambda b, j, pt: (pt[b, j * PAGES_PER_STEP + q], 0, 0, 0))

    const3 = lambda b, j, pt: (0, 0, 0)
    page_specs = [page_spec(q) for q in range(PAGES_PER_STEP)]
    w1_spec = _resident((NSA_KV, kdim, 2 * HEAD_DIM), const3)
    pe_spec = _resident((NSA_KV, SUBLANE, kdim), const3)
    w2_spec = _resident((NSA_KV, HEAD_DIM, HEAD_DIM), const3)
    out = jax.ShapeDtypeStruct((nb, n_chunks, KV_W), F32)
    ospec = pl.BlockSpec((1, n_chunks, KV_W), lambda b, j, pt: (b, 0, 0))
    grid_spec = pltpu.PrefetchScalarGridSpec(
        num_scalar_prefetch=1,
        grid=(nb, n_pages // PAGES_PER_STEP),
        in_specs=page_specs + page_specs + [w1_spec, w1_spec, pe_spec, pe_spec, w2_spec, w2_spec,
                                             pl.BlockSpec((1, HEAD_DIM), lambda b, j, pt: (0, 0))],
        out_specs=(ospec, ospec),
        scratch_shapes=[
            pltpu.VMEM((2, NSA_KV, PAGES_PER_GROUP * CHUNKS_PER_PAGE, kdim), F32),
            pltpu.VMEM((2, NSA_KV, n_chunks, HEAD_DIM), F32),
            pltpu.VMEM((2, NSA_KV, n_chunks + SUBLANE, HEAD_DIM), F32),
        ],
    )
    return pl.pallas_call(
        partial(_compress_kernel, n_pages=n_pages),
        grid_spec=grid_spec,
        out_shape=(out, out),
        compiler_params=_params("parallel", "arbitrary"),
        name="compress",
    )(page_table, *([pool_k] * PAGES_PER_STEP), *([pool_v] * PAGES_PER_STEP),
      w1ab(w1_k), w1ab(w1_v), pe8(pos_k), pe8(pos_v), w2_k, w2_v, g_kc.reshape(1, HEAD_DIM))


def _gate_columns(sig, g, rows):
    lane = lax.broadcasted_iota(jnp.int32, (rows, LANE), 1)
    base = g * (NSA_REP * N_GATES)
    return [[jnp.sum(jnp.where(lane == base + r * N_GATES + t, sig, 0.0), axis=1, keepdims=True)
             for t in range(N_GATES)] for r in range(NSA_REP)]


def _rejected_t(score_t, avail_t, n_rows):
    jrow = lax.broadcasted_iota(jnp.int32, score_t.shape, 0)
    cnt = jnp.zeros(score_t.shape, F32)
    for i in range(n_rows):
        si = score_t[i:i + 1, :]
        beats = (si > score_t) | ((si == score_t) & (jrow > i))
        cnt = cnt + jnp.where(beats, 1.0, 0.0)
    return jnp.where((cnt < N_SELECT) & avail_t, 0.0, 1.0)


def _nsa_prompt_kernel(q_ref, gate_ref, gq_ref, rope_ref, kc_ref, vc_ref, ks_ref, vs_ref, kw_ref, vw_ref,
                       o_ref, m_scr, acc_scr, *, n_sel, n_cmp):
    g = pl.program_id(1)
    qb = pl.program_id(2)
    t0 = qb * QB
    rq = NSA_REP * QB
    c, s1, s2 = rope_ref[0], rope_ref[1], rope_ref[2]
    gq = gq_ref[...]
    qn_l, qr_l = [], []
    for r in range(NSA_REP):
        qn_r = _rms(q_ref[0, :, r * HEAD_DIM:(r + 1) * HEAD_DIM], gq)
        qn_l.append(qn_r * SCALE_LOG2E)
        qr_l.append(_rope(qn_r, c, s1, s2) * SCALE_LOG2E)
    qn = jnp.concatenate(qn_l, axis=0).astype(BF16)
    qr = jnp.concatenate(qr_l, axis=0).astype(BF16)
    pos1 = t0 + lax.broadcasted_iota(jnp.int32, (QB, 1), 0)
    pos3 = jnp.concatenate([pos1] * NSA_REP, axis=0)

    ci = lax.broadcasted_iota(jnp.int32, (1, n_cmp), 1)
    p = _masked_softmax2(_dot_nt(qn, kc_ref[0]), ci * CMP_STRIDE + (CMP_LEN - 1) <= pos3)
    o_c = _dot(p, vc_ref[0])
    psum = p[0:QB] + p[QB:2 * QB] + p[2 * QB:3 * QB]

    ci_l = lax.broadcasted_iota(jnp.int32, (n_sel, n_cmp), 1) * CMP_STRIDE
    sj = lax.broadcasted_iota(jnp.int32, (n_sel, n_cmp), 0) * SEL_BLOCK
    cover_t = jnp.where((ci_l < sj + SEL_BLOCK) & (ci_l + CMP_LEN > sj), 1.0, 0.0)
    imp_t = lax.dot_general(cover_t, psum, (((1,), (1,)), ((), ())),
                            precision=lax.Precision.HIGHEST, preferred_element_type=F32)
    post = t0 + lax.broadcasted_iota(jnp.int32, (n_sel, QB), 1)
    jt = lax.broadcasted_iota(jnp.int32, (n_sel, QB), 0)
    cur = post >> SEL_SHIFT
    avail_t = jt * SEL_BLOCK <= post
    forced_t = (jt == 0) | (jt == cur) | (jt == cur - 1)
    score_t = jnp.where(avail_t, jnp.where(forced_t, FORCED_SCORE, imp_t), NEG)
    rej_t = _rejected_t(score_t, avail_t, n_sel)
    rej_t = jnp.concatenate([rej_t, jnp.zeros((LANE - n_sel, QB), F32)], axis=0)
    rej = rej_t.T.astype(BF16)

    m_scr[...] = jnp.full((rq, 1), NEG, F32)
    acc_scr[...] = jnp.zeros((rq, 2 * HEAD_DIM), F32)
    blocks_per_chunk = SEL_KEY_CHUNK // SEL_BLOCK

    def sel_step(ck, causal):
        k0 = pl.multiple_of(ck * SEL_KEY_CHUNK, SEL_KEY_CHUNK)
        jr = lax.broadcasted_iota(jnp.int32, (LANE, SEL_KEY_CHUNK), 0)
        kl = lax.broadcasted_iota(jnp.int32, (LANE, SEL_KEY_CHUNK), 1)
        expand = jnp.where((kl >> SEL_SHIFT) + ck * blocks_per_chunk == jr, NEG, 0.0).astype(BF16)
        bias = jnp.dot(rej, expand, preferred_element_type=F32)
        sc = _dot_nt(qr, ks_ref[0, pl.ds(k0, SEL_KEY_CHUNK), :]) + jnp.concatenate([bias] * NSA_REP, axis=0)
        if causal:
            kpos = k0 + lax.broadcasted_iota(jnp.int32, (1, SEL_KEY_CHUNK), 1)
            sc = jnp.where(kpos <= pos3, sc, NEG)
        m_old = m_scr[...]
        m_new = jnp.maximum(m_old, jnp.max(sc, axis=-1, keepdims=True))
        e = jnp.exp2(sc - m_new)
        acc_scr[...] = (jnp.exp2(m_old - m_new) * acc_scr[...]
                        + _dot(e, _with_ones(vs_ref[0, pl.ds(k0, SEL_KEY_CHUNK), :])))
        m_scr[...] = m_new

    n_chunks = (t0 + QB + SEL_KEY_CHUNK - 1) // SEL_KEY_CHUNK

    def sel_body(ck, carry):
        sel_step(ck, False)
        return carry

    lax.fori_loop(0, n_chunks - 1, sel_body, 0)
    sel_step(n_chunks - 1, True)
    acc = acc_scr[...]
    o_s = acc[:, :HEAD_DIM] / acc[:, HEAD_DIM:HEAD_DIM + 1]

    band = WINDOW + QB
    k0 = pl.multiple_of(jnp.maximum(t0 - WINDOW, 0), QB)
    kq = k0 + lax.broadcasted_iota(jnp.int32, (1, band), 1)
    sw = jnp.where((kq >= pos3 - WINDOW) & (kq <= pos3), _dot_nt(qr, kw_ref[0, pl.ds(k0, band), :]), NEG)
    ew = jnp.exp2(sw - jnp.max(sw, axis=-1, keepdims=True))
    accw = _dot(ew, _with_ones(vw_ref[0, pl.ds(k0, band), :]))
    o_w = accw[:, :HEAD_DIM] / accw[:, HEAD_DIM:HEAD_DIM + 1]

    gate = _gate_columns(jax.nn.sigmoid(gate_ref[0]), g, QB)
    for r in range(NSA_REP):
        rows = slice(r * QB, (r + 1) * QB)
        o_ref[0, :, r * HEAD_DIM:(r + 1) * HEAD_DIM] = (
            gate[r][0] * o_c[rows] + gate[r][1] * o_s[rows] + gate[r][2] * o_w[rows])


def _nsa_prompt(z, gates, g_q, rope_tab, kc, vc, ks, vs, kw, vw):
    b, t, _ = z.shape
    n_cmp = kc.shape[1]
    assert t >= WINDOW + QB
    qw = NSA_REP * HEAD_DIM
    kv_spec = pl.BlockSpec((1, t, HEAD_DIM), lambda i, g, j: (i, 0, g))
    cmp_spec = pl.BlockSpec((1, n_cmp, HEAD_DIM), lambda i, g, j: (i, 0, g))
    return pl.pallas_call(
        partial(_nsa_prompt_kernel, n_sel=t // SEL_BLOCK, n_cmp=n_cmp),
        grid=(b, NSA_KV, t // QB),
        in_specs=[
            pl.BlockSpec((1, QB, qw), lambda i, g, j: (i, j, g)),
            pl.BlockSpec((1, QB, LANE), lambda i, g, j: (i, j, 0)),
            pl.BlockSpec((1, HEAD_DIM), lambda i, g, j: (0, 0)),
            pl.BlockSpec((3, QB, HEAD_DIM), lambda i, g, j: (0, j, 0)),
            cmp_spec, cmp_spec, kv_spec, kv_spec, kv_spec, kv_spec,
        ],
        out_specs=pl.BlockSpec((1, QB, qw), lambda i, g, j: (i, j, g)),
        out_shape=jax.ShapeDtypeStruct((b, t, NSA_W), F32),
        scratch_shapes=[
            pltpu.VMEM((NSA_REP * QB, 1), F32),
            pltpu.VMEM((NSA_REP * QB, 2 * HEAD_DIM), F32),
        ],
        compiler_params=_params("parallel", "parallel", "arbitrary"),
        name="nsa_prompt",
    )(z, gates, g_q.reshape(1, HEAD_DIM), rope_tab, kc, vc, ks, vs, kw, vw)


def _nsa_sample_select_kernel(q_ref, gq_ref, rope_ref, kc_ref, vc_ref, kw_ref, vw_ref,
                              qr_ref, oc_ref, ow_ref, idx_ref, valid_ref, **static):
    for g in range(NSA_KV):
        sl = slice(g * HEAD_DIM, (g + 1) * HEAD_DIM)
        q3 = [q_ref[0, :, (g * NSA_REP + r) * HEAD_DIM:(g * NSA_REP + r + 1) * HEAD_DIM] for r in range(NSA_REP)]
        qr_ref[0, g], oc_ref[0, g], ow_ref[0, g], idx_ref[0, g], valid_ref[0, g] = _select_group(
            q3, gq_ref[...], rope_ref, kc_ref[0, :, sl], vc_ref[0, :, sl], kw_ref[0, :, g, :], vw_ref[0, :, g, :],
            **static)


def _select_group(q3, gq, rope_ref, kc, vc, kw, vw, *, pos, n_sel, n_cmp, n_win):
    n_cand = SEL_CAND_LANES
    c, s1, s2 = rope_ref[0:1], rope_ref[1:2], rope_ref[2:3]
    row = lax.broadcasted_iota(jnp.int32, (SUBLANE, HEAD_DIM), 0)
    q3 = [jnp.broadcast_to(q, (SUBLANE, HEAD_DIM)) for q in q3]
    q8 = jnp.where(row == 0, q3[0], jnp.where(row == 1, q3[1], q3[2]))
    qn = _rms(q8, gq)
    qr = _rope(qn, c, s1, s2) * SCALE_LOG2E

    ci = lax.broadcasted_iota(jnp.int32, (1, n_cmp), 1)
    p = _masked_softmax2(_dot_nt(qn * SCALE_LOG2E, kc), ci * CMP_STRIDE + (CMP_LEN - 1) <= pos)
    o_c = _dot(p, vc)
    psum = p[0:1] + p[1:2] + p[2:3]

    ci_s = lax.broadcasted_iota(jnp.int32, (n_cmp, n_cand), 0) * CMP_STRIDE
    sj = lax.broadcasted_iota(jnp.int32, (n_cmp, n_cand), 1) * SEL_BLOCK
    cover = jnp.where((ci_s < sj + SEL_BLOCK) & (ci_s + CMP_LEN > sj), 1.0, 0.0)
    imp = jnp.dot(jnp.broadcast_to(psum, (SUBLANE, n_cmp)), cover,
                  precision=lax.Precision.HIGHEST, preferred_element_type=F32)[0:1]
    jl = lax.broadcasted_iota(jnp.int32, (1, n_cand), 1)
    cur = pos // SEL_BLOCK
    avail = jl * SEL_BLOCK <= pos
    forced = (jl == 0) | (jl == cur) | (jl == cur - 1)
    score = jnp.where(avail, jnp.where(forced, FORCED_SCORE, imp), NEG)
    score = jnp.where(jl < n_sel, score, EXCLUDED)

    eye = (lax.broadcasted_iota(jnp.int32, (n_cand, n_cand), 0)
           == lax.broadcasted_iota(jnp.int32, (n_cand, n_cand), 1))
    score_b = jnp.broadcast_to(score, (n_cand, n_cand))
    score_c = jnp.sum(jnp.where(eye, score_b, 0.0), axis=1, keepdims=True)
    il = lax.broadcasted_iota(jnp.int32, (n_cand, n_cand), 1)
    jc = lax.broadcasted_iota(jnp.int32, (n_cand, n_cand), 0)
    beats = (score_b > score_c) | ((score_b == score_c) & (il < jc))
    rank = jnp.sum(jnp.where(beats, 1.0, 0.0), axis=1, keepdims=True)
    kl = lax.broadcasted_iota(jnp.int32, (n_cand, LANE), 1).astype(F32)
    hit = rank == kl
    jcol = lax.broadcasted_iota(jnp.int32, (n_cand, LANE), 0)
    idx = jnp.sum(jnp.where(hit, jcol.astype(F32), 0.0), axis=0, keepdims=True).astype(jnp.int32)
    valid = jnp.sum(jnp.where(hit & (jcol * SEL_BLOCK <= pos), 1.0, 0.0),
                    axis=0, keepdims=True).astype(jnp.int32)

    n_buf = kw.shape[0]
    kpos = pos + 1 - n_win + lax.broadcasted_iota(jnp.int32, (1, n_buf), 1)
    pw = _masked_softmax2(_dot_nt(qr, kw), (kpos >= pos - WINDOW) & (kpos <= pos))
    return qr, o_c, _dot(pw, vw), idx, valid


def _nsa_sample_gather_kernel(idx_ref, valid_ref, pt_ref, *refs, pos, n_past):
    kp_refs, vp_refs = refs[:NSA_KV], refs[NSA_KV:2 * NSA_KV]
    kn_ref, vn_ref, qr_ref, oc_ref, ow_ref, gate_ref, o_ref, m_scr, l_scr, acc_scr = refs[2 * NSA_KV:]
    b, k = pl.program_id(0), pl.program_id(1)

    @pl.when(k == 0)
    def _():
        m_scr[...] = jnp.full(m_scr.shape, NEG, F32)
        l_scr[...] = jnp.zeros(l_scr.shape, F32)
        acc_scr[...] = jnp.zeros(acc_scr.shape, F32)

    row = lax.broadcasted_iota(jnp.int32, (SEL_BLOCK, HEAD_DIM), 0)
    for g in range(NSA_KV):
        idx = idx_ref[b, g, k]
        in_past = idx < n_past
        k_blk = jnp.where(in_past, kp_refs[g][0, :, g, :], jnp.where(row == 0, kn_ref[0, :, g, :], 0.0))
        v_blk = jnp.where(in_past, vp_refs[g][0, :, g, :], jnp.where(row == 0, vn_ref[0, :, g, :], 0.0))
        kpos = idx * SEL_BLOCK + lax.broadcasted_iota(jnp.int32, (1, SEL_BLOCK), 1)
        msk = kpos <= jnp.where(valid_ref[b, g, k] > 0, pos, -1)
        sc = jnp.where(msk, _dot_nt(qr_ref[0, g], k_blk), NEG)
        m_old = m_scr[g]
        m_new = jnp.maximum(m_old, jnp.max(sc, axis=-1, keepdims=True))
        alpha = jnp.exp2(m_old - m_new)
        e = jnp.where(msk, jnp.exp2(sc - m_new), 0.0)
        l_scr[g] = alpha * l_scr[g] + jnp.sum(e, axis=-1, keepdims=True)
        acc_scr[g] = alpha * acc_scr[g] + _dot(e, v_blk)
        m_scr[g] = m_new

    @pl.when(k == pl.num_programs(1) - 1)
    def _():
        sig = jax.nn.sigmoid(gate_ref[0])
        for g in range(NSA_KV):
            o_s = acc_scr[g] / l_scr[g]
            gate = _gate_columns(sig, g, 1)
            o_c, o_w = oc_ref[0, g], ow_ref[0, g]
            for r in range(NSA_REP):
                h = g * NSA_REP + r
                o_ref[0, :, h * HEAD_DIM:(h + 1) * HEAD_DIM] = (
                    gate[r][0] * o_c[r:r + 1] + gate[r][1] * o_s[r:r + 1] + gate[r][2] * o_w[r:r + 1])


def _nsa_sample(zq, zgate, g_q, rope_row, kc, vc, kw_buf, vw_buf, n_win, pool_k, pool_v, k_new, v_new,
                page_table, pos):
    db = zq.shape[0]
    n_cmp = kc.shape[1]
    n_buf = kw_buf.shape[1]
    n_past = page_table.shape[1] * (PAGE_SIZE // SEL_BLOCK)
    n_sel = n_past + 1
    assert n_sel <= SEL_CAND_LANES
    qw = NSA_REP * HEAD_DIM
    vec = jax.ShapeDtypeStruct((db, NSA_KV, SUBLANE, HEAD_DIM), F32)
    ivec = jax.ShapeDtypeStruct((db, NSA_KV, 1, LANE), jnp.int32)
    vspec = pl.BlockSpec((1, NSA_KV, SUBLANE, HEAD_DIM), lambda i: (i, 0, 0, 0))
    ispec = pl.BlockSpec((1, NSA_KV, 1, LANE), lambda i: (i, 0, 0, 0))
    cmp_spec = pl.BlockSpec((1, n_cmp, KV_W), lambda i: (i, 0, 0))
    win_spec = pl.BlockSpec((1, n_buf, NSA_KV, HEAD_DIM), lambda i: (i, 0, 0, 0))
    qr, o_c, o_w, idx, valid = pl.pallas_call(
        partial(_nsa_sample_select_kernel, pos=pos, n_sel=n_sel, n_cmp=n_cmp, n_win=n_win),
        grid=(db,),
        in_specs=[
            pl.BlockSpec((1, 1, NSA_W), lambda i: (i, 0, 0)),
            pl.BlockSpec((1, HEAD_DIM), lambda i: (0, 0)),
            pl.BlockSpec((3, HEAD_DIM), lambda i: (0, 0)),
            cmp_spec, cmp_spec, win_spec, win_spec,
        ],
        out_specs=(vspec, vspec, vspec, ispec, ispec),
        out_shape=(vec, vec, vec, ivec, ivec),
        compiler_params=_params("parallel"),
        name="nsa_sample_select",
    )(zq, g_q.reshape(1, HEAD_DIM), rope_row, kc, vc, kw_buf, vw_buf)

    idx = idx[:, :, 0, :N_SELECT]
    valid = valid[:, :, 0, :N_SELECT]
    halves = PAGE_SIZE // SEL_BLOCK

    def pool_spec(g):
        def pool_block(i, k, idx_r, valid_r, pt_r):
            blk = jnp.minimum(idx_r[i, g, k], n_past - 1)
            return (pt_r[i, blk // halves] * halves + blk % halves, 0, 0, 0)
        return pl.BlockSpec((1, SEL_BLOCK, NSA_KV, HEAD_DIM), pool_block)

    n_pool = pool_k.shape[0]
    pool_specs = [pool_spec(g) for g in range(NSA_KV)]
    gvec = pl.BlockSpec((1, NSA_KV, SUBLANE, HEAD_DIM), lambda i, k, *_: (i, 0, 0, 0))
    new_spec = pl.BlockSpec((1, 1, NSA_KV, HEAD_DIM), lambda i, k, *_: (i, 0, 0, 0))
    grid_spec = pltpu.PrefetchScalarGridSpec(
        num_scalar_prefetch=3,
        grid=(db, N_SELECT),
        in_specs=pool_specs + pool_specs + [
            new_spec, new_spec, gvec, gvec, gvec,
            pl.BlockSpec((1, 1, LANE), lambda i, k, *_: (i, 0, 0)),
        ],
        out_specs=pl.BlockSpec((1, 1, NSA_W), lambda i, k, *_: (i, 0, 0)),
        scratch_shapes=[
            pltpu.VMEM((NSA_KV, SUBLANE, 1), F32),
            pltpu.VMEM((NSA_KV, SUBLANE, 1), F32),
            pltpu.VMEM((NSA_KV, SUBLANE, HEAD_DIM), F32),
        ],
    )
    pool_k2 = pool_k.reshape(n_pool * halves, SEL_BLOCK, NSA_KV, HEAD_DIM)
    pool_v2 = pool_v.reshape(n_pool * halves, SEL_BLOCK, NSA_KV, HEAD_DIM)
    return pl.pallas_call(
        partial(_nsa_sample_gather_kernel, pos=pos, n_past=n_past),
        grid_spec=grid_spec,
        out_shape=jax.ShapeDtypeStruct((db, 1, NSA_W), F32),
        compiler_params=_params("parallel", "arbitrary"),
        name="nsa_sample_gather",
    )(idx, valid, page_table, *([pool_k2] * NSA_KV), *([pool_v2] * NSA_KV),
      k_new, v_new, qr, o_c, o_w, zgate)


def kernel(x_prompt, x_sample, mem_prompt, cache_mem_k, cache_mem_v, cache_k_cmp, cache_v_cmp, cache_k_sel, cache_v_sel, state_k_win, state_v_win, page_table, g_ffn1, w_ffn1_gate, w_ffn1_up, w_ffn1_down, g_mix, g_ffn2, w_ffn2_gate, w_ffn2_up, w_ffn2_down, g_mem, w_mem_kv, g_mq, g_mk, w_in_a, ln_v_g, ln_v_b, w_spatial, b_spatial, w_out_a, g_kv, w_kv, pos_kc, w1_kc, w2_kc, pos_vc, w1_vc, w2_vc, g_kc, g_ks, g_kw, w_in_b, g_q, w_out_b):
    nb, t, d = x_prompt.shape
    db, ds, _ = x_sample.shape
    depth = g_ffn1.shape[0]
    n_a = w_in_a.shape[0]
    past_len = page_table.shape[1] * PAGE_SIZE
    assert ds == 1 and t % PAGE_SIZE == 0
    pos_s = past_len

    xp = x_prompt.reshape(nb * t, d)
    xs = x_sample.reshape(db * ds, d)
    mem = mem_prompt.reshape(nb * MEM_LEN, d)
    mem_k_list, mem_v_list, gm_v_list = [], [], []
    rope_p = _rope_tables(jnp.arange(t, dtype=jnp.int32))
    rope_s = _rope_tables(pos_s + jnp.arange(ds, dtype=jnp.int32))
    rope_p2 = jnp.tile(rope_p, (1, nb, 1))
    kv4 = lambda a, n: a.reshape(n, -1, NSA_KV, HEAD_DIM)

    for l in range(depth):
        if l == n_a:
            hp = _rms_mm(xp, g_kv, w_kv)
            hs = _rms_mm(xs, g_kv, w_kv)
            post_p = _nsa_kv_post(hp, rope_p2, g_ks, g_kw)
            kc_p_raw, vc_p_raw, ks_p, vs_p, kw_p, vw_p = [kv4(a, nb) for a in post_p[:6]]
            kv16_p = [a.reshape(nb, t, KV_W) for a in post_p[6:]]
            post_s = _nsa_kv_post(hs, jnp.broadcast_to(rope_s, (3, db, HEAD_DIM)), g_ks, g_kw)
            kc_s_raw, vc_s_raw, ks_s, vs_s, kw_s, vw_s = [kv4(a, db) for a in post_s[:6]]
            pages_p = jnp.arange(nb * t // PAGE_SIZE, dtype=jnp.int32).reshape(nb, t // PAGE_SIZE)
            cmp_w = (pos_kc, w1_kc, w2_kc, pos_vc, w1_vc, w2_vc, g_kc)
            kc_p, vc_p = _compress(kc_p_raw.reshape(-1, PAGE_SIZE, NSA_KV, HEAD_DIM),
                                   vc_p_raw.reshape(-1, PAGE_SIZE, NSA_KV, HEAD_DIM), pages_p, *cmp_w)
            kc_s, vc_s = _compress(cache_k_cmp, cache_v_cmp, page_table, *cmp_w)
            wbuf = state_k_win.shape[1]
            n_win = wbuf + ds
            pad = jnp.zeros((db, (-n_win) % SUBLANE, NSA_KV, HEAD_DIM), F32)
            kw_buf = jnp.concatenate([state_k_win, kw_s, pad], axis=1)
            vw_buf = jnp.concatenate([state_v_win, vw_s, pad], axis=1)

        xp = _ffn(xp, g_ffn1[l], w_ffn1_gate[l], w_ffn1_up[l], w_ffn1_down[l])
        xs = _ffn(xs, g_ffn1[l], w_ffn1_gate[l], w_ffn1_up[l], w_ffn1_down[l])

        mk_p, mv_p = _mem_kv_post(_rms_mm(mem, g_mem[l], w_mem_kv[l]), g_mk[l])
        mk_p = mk_p.reshape(nb, MEM_LEN, MEM_HEADS, HEAD_DIM)
        mv_p = mv_p.reshape(nb, MEM_LEN, MEM_HEADS, HEAD_DIM)
        mem_k_list.append(mk_p)
        mem_v_list.append(mv_p)
        mk_s, mv_s = cache_mem_k[l], cache_mem_v[l]

        if l < n_a:
            zp = _rms_mm(xp, g_mix[l], w_in_a[l])
            zs = _rms_mm(xs, g_mix[l], w_in_a[l])
            mem_blk = 2 * GM_W // MEM_W
            o1_p = _gmlp_prompt(zp, ln_v_g[l], ln_v_b[l], w_spatial[l], b_spatial[l])
            o1_s, v_s = _gmlp_first_row(zs, ln_v_g[l], ln_v_b[l], w_spatial[l], b_spatial[l])
            gm_v_list.append(v_s.reshape(db, ds, GM_W))
            o2_p = _mem_attend(zp.reshape(nb, t, -1), mem_blk, mk_p, mv_p, g_mq[l])
            o2_s = _mem_attend(zs.reshape(db, ds, -1), mem_blk, mk_s, mv_s, g_mq[l])
            w_out = w_out_a[l]
        else:
            lb = l - n_a
            n_gate = N_GATES * NSA_HEADS
            w_qm = jnp.concatenate([w_in_b[lb][:, :NSA_W], w_in_b[lb][:, NSA_W + n_gate:]], axis=1)
            w_gate = jnp.pad(w_in_b[lb][:, NSA_W:NSA_W + n_gate], ((0, 0), (0, LANE - n_gate)))
            zp = _rms_mm(xp, g_mix[l], w_qm).reshape(nb, t, -1)
            zs = _rms_mm(xs, g_mix[l], w_qm).reshape(db, ds, -1)
            gp = _rms_mm(xp, g_mix[l], w_gate).reshape(nb, t, LANE)
            gs = _rms_mm(xs, g_mix[l], w_gate).reshape(db, ds, LANE)
            o1_p = _nsa_prompt(zp, gp, g_q[lb], rope_p, kc_p, vc_p, *kv16_p).reshape(nb * t, NSA_W)
            o1_s = _nsa_sample(zs, gs, g_q[lb], rope_s[:, 0, :], kc_s, vc_s, kw_buf, vw_buf, n_win,
                               cache_k_sel, cache_v_sel, ks_s, vs_s, page_table, pos_s).reshape(db * ds, NSA_W)
            mem_blk = NSA_W // MEM_W
            o2_p = _mem_attend(zp, mem_blk, mk_p, mv_p, g_mq[l])
            o2_s = _mem_attend(zs, mem_blk, mk_s, mv_s, g_mq[l])
            w_out = w_out_b[lb]
        xp = _out_proj(o1_p, o2_p.reshape(nb * t, MEM_W), w_out, xp)
        xs = _out_proj(o1_s, o2_s.reshape(db * ds, MEM_W), w_out, xs)

        xp = _ffn(xp, g_ffn2[l], w_ffn2_gate[l], w_ffn2_up[l], w_ffn2_down[l])
        xs = _ffn(xs, g_ffn2[l], w_ffn2_gate[l], w_ffn2_up[l], w_ffn2_down[l])

    wp = min(WINDOW, t)
    ws = min(WINDOW, past_len + ds)
    return (xp.reshape(nb, t, d), xs.reshape(db, ds, d), jnp.stack(mem_k_list), jnp.stack(mem_v_list),
            kc_p_raw, vc_p_raw, ks_p, vs_p, kw_p[:, -wp:], vw_p[:, -wp:],
            kc_s_raw, vc_s_raw, ks_s, vs_s, kw_buf[:, n_win - ws:n_win], vw_buf[:, n_win - ws:n_win],
            jnp.stack(gm_v_list))
```

```python
from functools import partial

import numpy as np
import jax
import jax.numpy as jnp
from jax import lax
from jax.experimental import pallas as pl
from jax.experimental.pallas import tpu as pltpu

D_MODEL = 2048
HEAD_DIM = 128
ROT_DIM = HEAD_DIM // 4
ROPE_THETA = 500000.0
MEM_LEN = 256
MEM_HEADS = 4
MEM_W = MEM_HEADS * HEAD_DIM
GM_GROUPS = 12
GM_W = GM_GROUPS * HEAD_DIM
CHUNK = 128
NSA_HEADS = 12
NSA_KV = 4
NSA_REP = NSA_HEADS // NSA_KV
NSA_W = NSA_HEADS * HEAD_DIM
KV_W = NSA_KV * HEAD_DIM
N_GATES = 3
CMP_LEN = 32
CMP_STRIDE = 16
SEL_BLOCK = 64
N_SELECT = 16
WINDOW = 512
QB = 256
PAGE_SIZE = 128
EPS = 1e-6
NEG = -1e30
EXCLUDED = -3e38
FORCED_SCORE = 1e6
SCALE = HEAD_DIM ** -0.5
SCALE_LOG2E = float(SCALE * np.log2(np.e))

LANE = 128
SUBLANE = 8
VMEM_LIMIT = 56 * 1024 * 1024
ROW_TILE = 1024
PAGES_PER_GROUP = 16
PAGES_PER_STEP = 4
CHUNKS_PER_PAGE = PAGE_SIZE // CMP_STRIDE
SEL_KEY_CHUNK = 512
SEL_CAND_LANES = 3 * LANE
SEL_SHIFT = SEL_BLOCK.bit_length() - 1

BF16 = jnp.bfloat16
F32 = jnp.float32


def _params(*sem):
    return pltpu.CompilerParams(dimension_semantics=sem, vmem_limit_bytes=VMEM_LIMIT)


def _dot(a, b):
    return jnp.dot(a.astype(BF16), b.astype(BF16), preferred_element_type=F32)


def _dot_nt(a, b):
    return lax.dot_general(a.astype(BF16), b.astype(BF16), (((1,), (1,)), ((), ())),
                           preferred_element_type=F32)


def _rms(x, g):
    return x * lax.rsqrt(jnp.mean(x * x, axis=-1, keepdims=True) + EPS) * g


def _gelu(x):
    return 0.5 * x * (1.0 + lax.erf(x * np.float32(np.sqrt(0.5))))


def _rope(x, c, s1, s2):
    half = ROT_DIM // 2
    return x * c + pltpu.roll(x, LANE - half, 1) * s1 + pltpu.roll(x, half, 1) * s2


def _masked_softmax2(s2, m):
    sm = jnp.where(m, s2, NEG)
    mx = jnp.max(sm, axis=-1, keepdims=True)
    e = jnp.where(m, jnp.exp2(sm - mx), 0.0)
    den = jnp.sum(e, axis=-1, keepdims=True)
    return e / jnp.where(den > 0.0, den, 1.0)


def _with_ones(v):
    return jnp.concatenate([v.astype(BF16), jnp.ones(v.shape, BF16)], axis=1)


def _row_tile(m, want):
    return want if m % want == 0 else m


def _resident(shape, index_map):
    return pl.BlockSpec(shape, index_map, pipeline_mode=pl.Buffered(1))


def _ffn_kernel(x_ref, g_ref, wg_ref, wu_ref, wd_ref, o_ref, h_ref):
    @pl.when(pl.program_id(1) == 0)
    def _():
        x = x_ref[...]
        h_ref[...] = _rms(x, g_ref[...]).astype(BF16)
        o_ref[...] = x

    h = h_ref[...]
    a = jnp.dot(h, wg_ref[...].astype(BF16), preferred_element_type=F32)
    b = jnp.dot(h, wu_ref[...].astype(BF16), preferred_element_type=F32)
    t = a * jax.nn.sigmoid(a) * b
    o_ref[...] += 0.5 * _dot(t, wd_ref[...])


def _ffn(x, g, wg, wu, wd, l):
    m, d = x.shape
    f = wg.shape[2]
    tm = _row_tile(m, ROW_TILE)
    tf = 512
    return pl.pallas_call(
        _ffn_kernel,
        grid=(m // tm, f // tf),
        in_specs=[
            _resident((tm, d), lambda i, j: (i, 0)),
            pl.BlockSpec((1, d), lambda i, j: (0, 0)),
            pl.BlockSpec((None, d, tf), lambda i, j: (l, 0, j)),
            pl.BlockSpec((None, d, tf), lambda i, j: (l, 0, j)),
            pl.BlockSpec((None, tf, d), lambda i, j: (l, j, 0)),
        ],
        out_specs=_resident((tm, d), lambda i, j: (i, 0)),
        out_shape=jax.ShapeDtypeStruct((m, d), F32),
        scratch_shapes=[pltpu.VMEM((tm, d), BF16)],
        compiler_params=_params("parallel", "arbitrary"),
        name="ffn",
    )(x, g[l].reshape(1, d), wg, wu, wd)


def _rms_mm_kernel(x_ref, g_ref, w_ref, o_ref, h_ref):
    @pl.when(pl.program_id(1) == 0)
    def _():
        h_ref[...] = _rms(x_ref[...], g_ref[...]).astype(BF16)

    o_ref[...] = jnp.dot(h_ref[...], w_ref[...].astype(BF16), preferred_element_type=F32)


def _layer_weight_spec(w, l, rows, cols, index_map):
    if w.ndim == 2:
        return pl.BlockSpec((rows, cols), index_map)
    return pl.BlockSpec((None, rows, cols), lambda i, j: (l,) + index_map(i, j))


def _rms_mm(x, g, w, l=None):
    m, d = x.shape
    n = w.shape[-1]
    tm = _row_tile(m, ROW_TILE)
    tn = _row_tile(n, 512)
    return pl.pallas_call(
        _rms_mm_kernel,
        grid=(m // tm, n // tn),
        in_specs=[
            _resident((tm, d), lambda i, j: (i, 0)),
            pl.BlockSpec((1, d), lambda i, j: (0, 0)),
            _layer_weight_spec(w, l, d, tn, lambda i, j: (0, j)),
        ],
        out_specs=pl.BlockSpec((tm, tn), lambda i, j: (i, j)),
        out_shape=jax.ShapeDtypeStruct((m, n), F32),
        scratch_shapes=[pltpu.VMEM((tm, d), BF16)],
        compiler_params=_params("parallel", "arbitrary"),
        name="rms_mm",
    )(x, g.reshape(1, d), w)


def _out_proj_kernel(a1_ref, a2_ref, w1_ref, w2_ref, x_ref, o_ref):
    o_ref[...] = x_ref[...] + (_dot(a1_ref[...], w1_ref[...]) + _dot(a2_ref[...], w2_ref[...]))


def _out_proj(a1, a2, w, l, x):
    m, k1 = a1.shape
    k2 = a2.shape[1]
    d = w.shape[-1]
    tm = _row_tile(m, ROW_TILE)
    tn = 512
    return pl.pallas_call(
        _out_proj_kernel,
        grid=(m // tm, d // tn),
        in_specs=[
            _resident((tm, k1), lambda i, j: (i, 0)),
            _resident((tm, k2), lambda i, j: (i, 0)),
            _layer_weight_spec(w, l, k1, tn, lambda i, j: (0, j)),
            _layer_weight_spec(w, l, k2, tn, lambda i, j: (k1 // k2, j)),
            pl.BlockSpec((tm, tn), lambda i, j: (i, j)),
        ],
        out_specs=pl.BlockSpec((tm, tn), lambda i, j: (i, j)),
        out_shape=jax.ShapeDtypeStruct((m, d), F32),
        compiler_params=_params("parallel", "arbitrary"),
        name="out_proj",
    )(a1, a2, w, w, x)


def _mem_kv_post_kernel(kv_ref, g_ref, k_ref, v_ref):
    g = g_ref[...]
    for h in range(MEM_HEADS):
        k_ref[:, h, :] = _rms(kv_ref[:, h * HEAD_DIM:(h + 1) * HEAD_DIM], g)
        v_ref[:, h, :] = kv_ref[:, MEM_W + h * HEAD_DIM:MEM_W + (h + 1) * HEAD_DIM]


def _mem_kv_post(kv, g_mk):
    m = kv.shape[0]
    out = jax.ShapeDtypeStruct((m, MEM_HEADS, HEAD_DIM), F32)
    return pl.pallas_call(
        _mem_kv_post_kernel,
        out_shape=(out, out),
        compiler_params=pltpu.CompilerParams(vmem_limit_bytes=VMEM_LIMIT),
        name="mem_kv_post",
    )(kv, g_mk.reshape(1, HEAD_DIM))


def _mem_attend_kernel(q_ref, g_ref, k_ref, v_ref, o_ref, *, rows):
    g = g_ref[...]
    for h in range(MEM_HEADS):
        sl = slice(h * HEAD_DIM, (h + 1) * HEAD_DIM)
        q = q_ref[0, :, sl]
        if rows < SUBLANE:
            q = jnp.broadcast_to(q[0:1], (SUBLANE, HEAD_DIM))
        q = _rms(q, g) * SCALE_LOG2E
        s2 = _dot_nt(q, k_ref[0, :, h, :])
        e = jnp.exp2(s2 - jnp.max(s2, axis=-1, keepdims=True))
        p = e / jnp.sum(e, axis=-1, keepdims=True)
        o = _dot(p, v_ref[0, :, h, :])
        o_ref[0, :, sl] = o[:rows]


def _mem_attend(z, col_block, mk, mv, g_mq):
    b, t, _ = z.shape
    tm = _row_tile(t, 512)
    kv_spec = pl.BlockSpec((1, MEM_LEN, MEM_HEADS, HEAD_DIM), lambda i, j: (i, 0, 0, 0))
    return pl.pallas_call(
        partial(_mem_attend_kernel, rows=tm),
        grid=(b, t // tm),
        in_specs=[
            pl.BlockSpec((1, tm, MEM_W), lambda i, j: (i, j, col_block)),
            pl.BlockSpec((1, HEAD_DIM), lambda i, j: (0, 0)),
            kv_spec, kv_spec,
        ],
        out_specs=pl.BlockSpec((1, tm, MEM_W), lambda i, j: (i, j, 0)),
        out_shape=jax.ShapeDtypeStruct((b, t, MEM_W), F32),
        compiler_params=_params("parallel", "arbitrary"),
        name="mem_attend",
    )(z, g_mq.reshape(1, HEAD_DIM), mk, mv)


def _layer_norm(x, g, b):
    xc = x - jnp.mean(x, axis=-1, keepdims=True)
    var = jnp.mean(xc * xc, axis=-1, keepdims=True)
    return xc * lax.rsqrt(var + EPS) * g + b


def _gmlp_prompt_kernel(z_ref, lg_ref, lb_ref, ws_ref, bs_ref, o_ref):
    v = _layer_norm(_gelu(z_ref[:, GM_W:]), lg_ref[...], lb_ref[...])
    row = lax.broadcasted_iota(jnp.int32, (CHUNK, CHUNK), 0)
    col = lax.broadcasted_iota(jnp.int32, (CHUNK, CHUNK), 1)
    causal = col <= row
    for g in range(GM_GROUPS):
        sl = slice(g * HEAD_DIM, (g + 1) * HEAD_DIM)
        w = jnp.where(causal, ws_ref[g], 0.0)
        sv = _dot(w, v[:, sl]) + bs_ref[:, g:g + 1]
        o_ref[:, sl] = _gelu(z_ref[:, sl]) * sv


def _gmlp_prompt(z, ln_g, ln_b, w_s, b_s):
    m = z.shape[0]
    return pl.pallas_call(
        _gmlp_prompt_kernel,
        grid=(m // CHUNK,),
        in_specs=[
            pl.BlockSpec((CHUNK, 2 * GM_W), lambda i: (i, 0)),
            pl.BlockSpec((1, GM_W), lambda i: (0, 0)),
            pl.BlockSpec((1, GM_W), lambda i: (0, 0)),
            pl.BlockSpec((GM_GROUPS, CHUNK, CHUNK), lambda i: (0, 0, 0)),
            pl.BlockSpec((CHUNK, GM_GROUPS), lambda i: (0, 0)),
        ],
        out_specs=pl.BlockSpec((CHUNK, GM_W), lambda i: (i, 0)),
        out_shape=jax.ShapeDtypeStruct((m, GM_W), F32),
        compiler_params=_params("parallel"),
        name="gmlp_prompt",
    )(z, ln_g.reshape(1, GM_W), ln_b.reshape(1, GM_W), w_s, b_s.T)


def _gmlp_first_row_kernel(z_ref, lg_ref, lb_ref, w0_ref, b0_ref, o_ref, v_ref):
    v = _layer_norm(_gelu(z_ref[:, GM_W:2 * GM_W]), lg_ref[...], lb_ref[...])
    v_ref[...] = v
    o_ref[...] = _gelu(z_ref[:, :GM_W]) * (v * w0_ref[...] + b0_ref[...])


def _gmlp_first_row(z, ln_g, ln_b, w_s, b_s):
    m = z.shape[0]
    w0 = jnp.repeat(w_s[:, 0, 0], HEAD_DIM).reshape(1, GM_W)
    b0 = jnp.repeat(b_s[:, 0], HEAD_DIM).reshape(1, GM_W)
    return pl.pallas_call(
        _gmlp_first_row_kernel,
        out_shape=(jax.ShapeDtypeStruct((m, GM_W), F32), jax.ShapeDtypeStruct((m, GM_W), F32)),
        compiler_params=pltpu.CompilerParams(vmem_limit_bytes=VMEM_LIMIT),
        name="gmlp_first_row",
    )(z, ln_g.reshape(1, GM_W), ln_b.reshape(1, GM_W), w0, b0)


def _rope_tables(pos):
    half = ROT_DIM // 2
    inv = ROPE_THETA ** (-jnp.arange(half, dtype=F32) / half)
    ang = pos.astype(F32)[:, None] * inv[None, :]
    cos, sin = jnp.cos(ang), jnp.sin(ang)
    t = pos.shape[0]
    one = jnp.ones((t, HEAD_DIM - ROT_DIM), F32)
    zero = jnp.zeros((t, HEAD_DIM - ROT_DIM), F32)
    zh = jnp.zeros((t, half), F32)
    c = jnp.concatenate([cos, cos, one], axis=1)
    s1 = jnp.concatenate([-sin, zh, zero], axis=1)
    s2 = jnp.concatenate([zh, sin, zero], axis=1)
    return jnp.stack([c, s1, s2])


def _nsa_kv_post_kernel(h_ref, rope_ref, gs_ref, gw_ref, kc_ref, vc_ref, ks_ref, vs_ref, kw_ref, vw_ref,
                        *mxu_refs):
    c, s1, s2 = rope_ref[0], rope_ref[1], rope_ref[2]

    def part(i, g):
        return h_ref[:, i * KV_W + g * HEAD_DIM:i * KV_W + (g + 1) * HEAD_DIM]

    for g in range(NSA_KV):
        sl = slice(g * HEAD_DIM, (g + 1) * HEAD_DIM)
        kc_ref[:, g, :] = part(0, g)
        vc_ref[:, g, :] = part(1, g)
        ks = _rope(_rms(part(2, g), gs_ref[...]), c, s1, s2)
        kw = _rope(_rms(part(4, g), gw_ref[...]), c, s1, s2)
        ks_ref[:, g, :] = ks
        vs_ref[:, g, :] = part(3, g)
        kw_ref[:, g, :] = kw
        vw_ref[:, g, :] = part(5, g)
        if mxu_refs:
            ks16_ref, vst16_ref, kw16_ref, vwt16_ref = mxu_refs
            ks16_ref[:, sl] = ks.astype(BF16)
            kw16_ref[:, sl] = kw.astype(BF16)
            vst16_ref[sl, :] = part(3, g).T.astype(BF16)
            vwt16_ref[sl, :] = part(5, g).T.astype(BF16)


def _nsa_kv_post(h, rope_tab, g_ks, g_kw, mxu_copies):
    m = h.shape[0]
    tm = _row_tile(m, 512)
    out = jax.ShapeDtypeStruct((m, NSA_KV, HEAD_DIM), F32)
    ospec = pl.BlockSpec((tm, NSA_KV, HEAD_DIM), lambda i: (i, 0, 0))
    outs, ospecs = (out,) * 6, (ospec,) * 6
    if mxu_copies:
        k16, v16 = jax.ShapeDtypeStruct((m, KV_W), BF16), jax.ShapeDtypeStruct((KV_W, m), BF16)
        kspec, vspec = pl.BlockSpec((tm, KV_W), lambda i: (i, 0)), pl.BlockSpec((KV_W, tm), lambda i: (0, i))
        outs, ospecs = outs + (k16, v16, k16, v16), ospecs + (kspec, vspec, kspec, vspec)
    return pl.pallas_call(
        _nsa_kv_post_kernel,
        grid=(m // tm,),
        in_specs=[
            pl.BlockSpec((tm, 6 * KV_W), lambda i: (i, 0)),
            pl.BlockSpec((3, tm, HEAD_DIM), lambda i: (0, i, 0)),
            pl.BlockSpec((1, HEAD_DIM), lambda i: (0, 0)),
            pl.BlockSpec((1, HEAD_DIM), lambda i: (0, 0)),
        ],
        out_specs=ospecs,
        out_shape=outs,
        compiler_params=_params("parallel"),
        name="nsa_kv_post",
    )(h, rope_tab, g_ks.reshape(1, HEAD_DIM), g_kw.reshape(1, HEAD_DIM))


def _compress_kernel(pt_ref, *refs, n_pages):
    n_in = 2 * PAGES_PER_STEP
    page_refs = refs[:n_in]
    w1_refs, pe_refs, w2_refs = refs[n_in:n_in + 2], refs[n_in + 2:n_in + 4], refs[n_in + 4:n_in + 6]
    gk_ref = refs[n_in + 6]
    o_refs = refs[n_in + 7:n_in + 9]
    l_scr, a_scr, b_scr = refs[n_in + 9:]
    j = pl.program_id(1)
    steps_per_group = PAGES_PER_GROUP // PAGES_PER_STEP
    jj = j % steps_per_group
    n_chunks = n_pages * CHUNKS_PER_PAGE
    group_chunks = PAGES_PER_GROUP * CHUNKS_PER_PAGE

    @pl.when(j == 0)
    def _():
        b_scr[:, :, n_chunks:, :] = jnp.zeros((2, NSA_KV, SUBLANE, HEAD_DIM), F32)

    for t in range(2):
        for q in range(PAGES_PER_STEP):
            row0 = pl.multiple_of((jj * PAGES_PER_STEP + q) * CHUNKS_PER_PAGE, CHUNKS_PER_PAGE)
            page = page_refs[t * PAGES_PER_STEP + q]
            for p in range(CMP_STRIDE):
                for g in range(NSA_KV):
                    l_scr[t, g, pl.ds(row0, CHUNKS_PER_PAGE), p * HEAD_DIM:(p + 1) * HEAD_DIM] = (
                        page[0, pl.ds(p, CHUNKS_PER_PAGE, stride=CMP_STRIDE), g, :])

    @pl.when(jj == steps_per_group - 1)
    def _():
        base = pl.multiple_of((j // steps_per_group) * group_chunks, group_chunks)
        for t in range(2):
            for g in range(NSA_KV):
                r = _dot(l_scr[t, g], w1_refs[t][g])
                a_scr[t, g, pl.ds(base, group_chunks), :] = r[:, :HEAD_DIM]
                b_scr[t, g, pl.ds(base, group_chunks), :] = r[:, HEAD_DIM:]

    @pl.when(j == pl.num_programs(1) - 1)
    def _():
        for t in range(2):
            for g in range(NSA_KV):
                pw = _dot(pe_refs[t][g], w1_refs[t][g])
                bias = pw[0:1, :HEAD_DIM] + pw[1:2, HEAD_DIM:]
                h = a_scr[t, g] + b_scr[t, g, pl.ds(1, n_chunks), :] + bias
                y = _dot(_gelu(h), w2_refs[t][g])
                if t == 0:
                    y = _rms(y, gk_ref[...])
                o_refs[t][0, :, g * HEAD_DIM:(g + 1) * HEAD_DIM] = y


def _compress(pool_k, pool_v, page_table, pos_k, w1_k, w2_k, pos_v, w1_v, w2_v, g_kc):
    nb, n_pages = page_table.shape
    assert n_pages % PAGES_PER_GROUP == 0
    n_chunks = n_pages * CHUNKS_PER_PAGE
    kdim = CMP_STRIDE * HEAD_DIM

    def w1ab(w1):
        return jnp.concatenate([w1[:, :CMP_STRIDE].reshape(NSA_KV, kdim, HEAD_DIM),
                                w1[:, CMP_STRIDE:].reshape(NSA_KV, kdim, HEAD_DIM)], axis=-1).astype(BF16)

    def pe8(pos_enc):
        return jnp.pad(pos_enc.reshape(NSA_KV, 2, kdim), ((0, 0), (0, SUBLANE - 2), (0, 0)))

    def page_spec(q):
        return pl.BlockSpec((1, PAGE_SIZE, NSA_KV, HEAD_DIM),
                            lambda b, j, pt: (pt[b, j * PAGES_PER_STEP + q], 0, 0, 0))

    const3 = lambda b, j, pt: (0, 0, 0)
    page_specs = [page_spec(q) for q in range(PAGES_PER_STEP)]
    w1_spec = _resident((NSA_KV, kdim, 2 * HEAD_DIM), const3)
    pe_spec = _resident((NSA_KV, SUBLANE, kdim), const3)
    w2_spec = _resident((NSA_KV, HEAD_DIM, HEAD_DIM), const3)
    out = jax.ShapeDtypeStruct((nb, n_chunks, KV_W), F32)
    ospec = pl.BlockSpec((1, n_chunks, KV_W), lambda b, j, pt: (b, 0, 0))
    grid_spec = pltpu.PrefetchScalarGridSpec(
        num_scalar_prefetch=1,
        grid=(nb, n_pages // PAGES_PER_STEP),
        in_specs=page_specs + page_specs + [w1_spec, w1_spec, pe_spec, pe_spec, w2_spec, w2_spec,
                                             pl.BlockSpec((1, HEAD_DIM), lambda b, j, pt: (0, 0))],
        out_specs=(ospec, ospec),
        scratch_shapes=[
            pltpu.VMEM((2, NSA_KV, PAGES_PER_GROUP * CHUNKS_PER_PAGE, kdim), F32),
            pltpu.VMEM((2, NSA_KV, n_chunks, HEAD_DIM), F32),
            pltpu.VMEM((2, NSA_KV, n_chunks + SUBLANE, HEAD_DIM), F32),
        ],
    )
    return pl.pallas_call(
        partial(_compress_kernel, n_pages=n_pages),
        grid_spec=grid_spec,
        out_shape=(out, out),
        compiler_params=_params("parallel", "arbitrary"),
        name="compress",
    )(page_table, *([pool_k] * PAGES_PER_STEP), *([pool_v] * PAGES_PER_STEP),
      w1ab(w1_k), w1ab(w1_v), pe8(pos_k), pe8(pos_v), w2_k, w2_v, g_kc.reshape(1, HEAD_DIM))


def _gate_columns(sig, g, rows):
    lane = lax.broadcasted_iota(jnp.int32, (rows, LANE), 1)
    base = g * (NSA_REP * N_GATES)
    return [[jnp.sum(jnp.where(lane == base + r * N_GATES + t, sig, 0.0), axis=1, keepdims=True)
             for t in range(N_GATES)] for r in range(NSA_REP)]


def _rejected_t(score_t, avail_t, n_rows):
    groups = n_rows // SUBLANE
    rows = [score_t[r * SUBLANE:(r + 1) * SUBLANE] for r in range(groups)]
    sub = lax.broadcasted_iota(jnp.int32, rows[0].shape, 0)
    cnt = [jnp.zeros(rows[0].shape, F32) for _ in range(groups)]
    for i in range(n_rows):
        si = score_t[i:i + 1, :]
        gi, oi = divmod(i, SUBLANE)
        for r in range(groups):
            if r > gi:
                beats = si >= rows[r]
            elif r < gi:
                beats = si > rows[r]
            else:
                beats = (si > rows[r]) | ((si == rows[r]) & (sub > oi))
            cnt[r] = cnt[r] + jnp.where(beats, 1.0, 0.0)
    cnt = jnp.concatenate(cnt, axis=0)
    return jnp.where((cnt < N_SELECT) & avail_t, 0.0, 1.0)


def _nsa_prompt_kernel(q_ref, gate_ref, gq_ref, rope_ref, kc_ref, vc_ref, ks_ref, vst_ref, kw_ref, vwt_ref,
                       o_ref, m_scr, acc_scr, *, n_sel, n_cmp):
    g = pl.program_id(1)
    qb = pl.program_id(2)
    t0 = qb * QB
    rq = NSA_REP * QB
    c, s1, s2 = rope_ref[0], rope_ref[1], rope_ref[2]
    gq = gq_ref[...]
    qn_l, qr_l = [], []
    for r in range(NSA_REP):
        qn_r = _rms(q_ref[0, :, r * HEAD_DIM:(r + 1) * HEAD_DIM], gq)
        qn_l.append(qn_r * SCALE_LOG2E)
        qr_l.append(_rope(qn_r, c, s1, s2) * SCALE_LOG2E)
    qn = jnp.concatenate(qn_l, axis=0).astype(BF16)
    qr = jnp.concatenate(qr_l, axis=0).astype(BF16)
    pos1 = t0 + lax.broadcasted_iota(jnp.int32, (QB, 1), 0)
    pos3 = jnp.concatenate([pos1] * NSA_REP, axis=0)

    ci = lax.broadcasted_iota(jnp.int32, (1, n_cmp), 1)
    p = _masked_softmax2(_dot_nt(qn, kc_ref[0]), ci * CMP_STRIDE + (CMP_LEN - 1) <= pos3)
    o_c = _dot(p, vc_ref[0])
    psum = p[0:QB] + p[QB:2 * QB] + p[2 * QB:3 * QB]

    ci_l = lax.broadcasted_iota(jnp.int32, (n_sel, n_cmp), 1) * CMP_STRIDE
    sj = lax.broadcasted_iota(jnp.int32, (n_sel, n_cmp), 0) * SEL_BLOCK
    cover_t = jnp.where((ci_l < sj + SEL_BLOCK) & (ci_l + CMP_LEN > sj), 1.0, 0.0)
    imp_t = lax.dot_general(cover_t, psum, (((1,), (1,)), ((), ())),
                            precision=lax.Precision.HIGHEST, preferred_element_type=F32)
    post = t0 + lax.broadcasted_iota(jnp.int32, (n_sel, QB), 1)
    jt = lax.broadcasted_iota(jnp.int32, (n_sel, QB), 0)
    cur = post >> SEL_SHIFT
    avail_t = jt * SEL_BLOCK <= post
    forced_t = (jt == 0) | (jt == cur) | (jt == cur - 1)
    score_t = jnp.where(avail_t, jnp.where(forced_t, FORCED_SCORE, imp_t), NEG)
    rej_t = _rejected_t(score_t, avail_t, n_sel)
    rej_t = jnp.concatenate([rej_t, jnp.zeros((LANE - n_sel, QB), F32)], axis=0).astype(BF16)

    pos_l = jnp.concatenate([t0 + lax.broadcasted_iota(jnp.int32, (1, QB), 1)] * NSA_REP, axis=1)

    def values_t(vt):
        return jnp.concatenate([vt, jnp.ones(vt.shape, BF16)], axis=0)

    def normalised(acc_t):
        o_t = acc_t[:HEAD_DIM] / acc_t[HEAD_DIM:HEAD_DIM + 1]
        return [o_t[:, r * QB:(r + 1) * QB].T for r in range(NSA_REP)]

    m_scr[...] = jnp.full((1, rq), NEG, F32)
    acc_scr[...] = jnp.zeros((2 * HEAD_DIM, rq), F32)
    blocks_per_chunk = SEL_KEY_CHUNK // SEL_BLOCK

    def sel_step(ck, causal):
        k0 = pl.multiple_of(ck * SEL_KEY_CHUNK, SEL_KEY_CHUNK)
        kr = lax.broadcasted_iota(jnp.int32, (SEL_KEY_CHUNK, LANE), 0)
        jl = lax.broadcasted_iota(jnp.int32, (SEL_KEY_CHUNK, LANE), 1)
        expand_t = jnp.where((kr >> SEL_SHIFT) + ck * blocks_per_chunk == jl, NEG, 0.0).astype(BF16)
        bias_t = jnp.dot(expand_t, rej_t, preferred_element_type=F32)
        sc = _dot_nt(ks_ref[0, pl.ds(k0, SEL_KEY_CHUNK), :], qr) + jnp.concatenate([bias_t] * NSA_REP, axis=1)
        if causal:
            kpos = k0 + lax.broadcasted_iota(jnp.int32, (SEL_KEY_CHUNK, 1), 0)
            sc = jnp.where(kpos <= pos_l, sc, NEG)
        m_old = m_scr[...]
        m_new = jnp.maximum(m_old, jnp.max(sc, axis=0, keepdims=True))
        e = jnp.exp2(sc - m_new).astype(BF16)
        acc_scr[...] = (jnp.exp2(m_old - m_new) * acc_scr[...]
                        + jnp.dot(values_t(vst_ref[:, pl.ds(k0, SEL_KEY_CHUNK)]), e, preferred_element_type=F32))
        m_scr[...] = m_new

    n_chunks = (t0 + QB + SEL_KEY_CHUNK - 1) // SEL_KEY_CHUNK

    def sel_body(ck, carry):
        sel_step(ck, False)
        return carry

    lax.fori_loop(0, n_chunks - 1, sel_body, 0)
    sel_step(n_chunks - 1, True)
    o_s = normalised(acc_scr[...])

    band = WINDOW + QB
    k0 = pl.multiple_of(jnp.maximum(t0 - WINDOW, 0), QB)
    kq = k0 + lax.broadcasted_iota(jnp.int32, (band, 1), 0)
    sw = jnp.where((kq >= pos_l - WINDOW) & (kq <= pos_l), _dot_nt(kw_ref[0, pl.ds(k0, band), :], qr), NEG)
    ew = jnp.exp2(sw - jnp.max(sw, axis=0, keepdims=True)).astype(BF16)
    o_w = normalised(jnp.dot(values_t(vwt_ref[:, pl.ds(k0, band)]), ew, preferred_element_type=F32))

    gate = _gate_columns(jax.nn.sigmoid(gate_ref[0]), g, QB)
    for r in range(NSA_REP):
        o_ref[0, :, r * HEAD_DIM:(r + 1) * HEAD_DIM] = (
            gate[r][0] * o_c[r * QB:(r + 1) * QB] + gate[r][1] * o_s[r] + gate[r][2] * o_w[r])


def _nsa_prompt(z, gates, g_q, rope_tab, kc, vc, ks, vst, kw, vwt):
    b, t, _ = z.shape
    n_cmp = kc.shape[1]
    assert t >= WINDOW + QB
    qw = NSA_REP * HEAD_DIM
    k_spec = pl.BlockSpec((1, t, HEAD_DIM), lambda i, g, j: (i, 0, g))
    vt_spec = pl.BlockSpec((HEAD_DIM, t), lambda i, g, j: (g, i))
    cmp_spec = pl.BlockSpec((1, n_cmp, HEAD_DIM), lambda i, g, j: (i, 0, g))
    return pl.pallas_call(
        partial(_nsa_prompt_kernel, n_sel=t // SEL_BLOCK, n_cmp=n_cmp),
        grid=(b, NSA_KV, t // QB),
        in_specs=[
            pl.BlockSpec((1, QB, qw), lambda i, g, j: (i, j, g)),
            pl.BlockSpec((1, QB, LANE), lambda i, g, j: (i, j, 0)),
            pl.BlockSpec((1, HEAD_DIM), lambda i, g, j: (0, 0)),
            pl.BlockSpec((3, QB, HEAD_DIM), lambda i, g, j: (0, j, 0)),
            cmp_spec, cmp_spec, k_spec, vt_spec, k_spec, vt_spec,
        ],
        out_specs=pl.BlockSpec((1, QB, qw), lambda i, g, j: (i, j, g)),
        out_shape=jax.ShapeDtypeStruct((b, t, NSA_W), F32),
        scratch_shapes=[
            pltpu.VMEM((1, NSA_REP * QB), F32),
            pltpu.VMEM((2 * HEAD_DIM, NSA_REP * QB), F32),
        ],
        compiler_params=_params("parallel", "parallel", "arbitrary"),
        name="nsa_prompt",
    )(z, gates, g_q.reshape(1, HEAD_DIM), rope_tab, kc, vc, ks, vst, kw, vwt)


def _nsa_sample_select_kernel(q_ref, gq_ref, rope_ref, kc_ref, vc_ref, kw_ref, vw_ref,
                              qr_ref, oc_ref, ow_ref, idx_ref, valid_ref, **static):
    for g in range(NSA_KV):
        sl = slice(g * HEAD_DIM, (g + 1) * HEAD_DIM)
        q3 = [q_ref[0, :, (g * NSA_REP + r) * HEAD_DIM:(g * NSA_REP + r + 1) * HEAD_DIM] for r in range(NSA_REP)]
        qr_ref[0, g], oc_ref[0, g], ow_ref[0, g], idx_ref[0, g], valid_ref[0, g] = _select_group(
            q3, gq_ref[...], rope_ref, kc_ref[0, :, sl], vc_ref[0, :, sl], kw_ref[0, :, g, :], vw_ref[0, :, g, :],
            **static)


def _select_group(q3, gq, rope_ref, kc, vc, kw, vw, *, pos, n_sel, n_cmp, n_win):
    n_cand = SEL_CAND_LANES
    c, s1, s2 = rope_ref[0:1], rope_ref[1:2], rope_ref[2:3]
    row = lax.broadcasted_iota(jnp.int32, (SUBLANE, HEAD_DIM), 0)
    q3 = [jnp.broadcast_to(q, (SUBLANE, HEAD_DIM)) for q in q3]
    q8 = jnp.where(row == 0, q3[0], jnp.where(row == 1, q3[1], q3[2]))
    qn = _rms(q8, gq)
    qr = _rope(qn, c, s1, s2) * SCALE_LOG2E

    ci = lax.broadcasted_iota(jnp.int32, (1, n_cmp), 1)
    p = _masked_softmax2(_dot_nt(qn * SCALE_LOG2E, kc), ci * CMP_STRIDE + (CMP_LEN - 1) <= pos)
    o_c = _dot(p, vc)
    psum = p[0:1] + p[1:2] + p[2:3]

    ci_s = lax.broadcasted_iota(jnp.int32, (n_cmp, n_cand), 0) * CMP_STRIDE
    sj = lax.broadcasted_iota(jnp.int32, (n_cmp, n_cand), 1) * SEL_BLOCK
    cover = jnp.where((ci_s < sj + SEL_BLOCK) & (ci_s + CMP_LEN > sj), 1.0, 0.0)
    imp = jnp.dot(jnp.broadcast_to(psum, (SUBLANE, n_cmp)), cover,
                  precision=lax.Precision.HIGHEST, preferred_element_type=F32)[0:1]
    jl = lax.broadcasted_iota(jnp.int32, (1, n_cand), 1)
    cur = pos // SEL_BLOCK
    avail = jl * SEL_BLOCK <= pos
    forced = (jl == 0) | (jl == cur) | (jl == cur - 1)
    score = jnp.where(avail, jnp.where(forced, FORCED_SCORE, imp), NEG)
    score = jnp.where(jl < n_sel, score, EXCLUDED)

    eye = (lax.broadcasted_iota(jnp.int32, (n_cand, n_cand), 0)
           == lax.broadcasted_iota(jnp.int32, (n_cand, n_cand), 1))
    score_b = jnp.broadcast_to(score, (n_cand, n_cand))
    score_c = jnp.sum(jnp.where(eye, score_b, 0.0), axis=1, keepdims=True)
    il = lax.broadcasted_iota(jnp.int32, (n_cand, n_cand), 1)
    jc = lax.broadcasted_iota(jnp.int32, (n_cand, n_cand), 0)
    beats = (score_b > score_c) | ((score_b == score_c) & (il < jc))
    rank = jnp.sum(jnp.where(beats, 1.0, 0.0), axis=1, keepdims=True)
    kl = lax.broadcasted_iota(jnp.int32, (n_cand, LANE), 1).astype(F32)
    hit = rank == kl
    jcol = lax.broadcasted_iota(jnp.int32, (n_cand, LANE), 0)
    idx = jnp.sum(jnp.where(hit, jcol.astype(F32), 0.0), axis=0, keepdims=True).astype(jnp.int32)
    valid = jnp.sum(jnp.where(hit & (jcol * SEL_BLOCK <= pos), 1.0, 0.0),
                    axis=0, keepdims=True).astype(jnp.int32)

    n_buf = kw.shape[0]
    kpos = pos + 1 - n_win + lax.broadcasted_iota(jnp.int32, (1, n_buf), 1)
    pw = _masked_softmax2(_dot_nt(qr, kw), (kpos >= pos - WINDOW) & (kpos <= pos))
    return qr, o_c, _dot(pw, vw), idx, valid


def _nsa_sample_gather_kernel(idx_ref, valid_ref, pt_ref, *refs, pos, n_past):
    kp_refs, vp_refs = refs[:NSA_KV], refs[NSA_KV:2 * NSA_KV]
    kn_ref, vn_ref, qr_ref, oc_ref, ow_ref, gate_ref, o_ref, m_scr, l_scr, acc_scr = refs[2 * NSA_KV:]
    b, k = pl.program_id(0), pl.program_id(1)

    @pl.when(k == 0)
    def _():
        m_scr[...] = jnp.full(m_scr.shape, NEG, F32)
        l_scr[...] = jnp.zeros(l_scr.shape, F32)
        acc_scr[...] = jnp.zeros(acc_scr.shape, F32)

    row = lax.broadcasted_iota(jnp.int32, (SEL_BLOCK, HEAD_DIM), 0)
    for g in range(NSA_KV):
        idx = idx_ref[b, g, k]
        in_past = idx < n_past
        k_blk = jnp.where(in_past, kp_refs[g][0, :, g, :], jnp.where(row == 0, kn_ref[0, :, g, :], 0.0))
        v_blk = jnp.where(in_past, vp_refs[g][0, :, g, :], jnp.where(row == 0, vn_ref[0, :, g, :], 0.0))
        kpos = idx * SEL_BLOCK + lax.broadcasted_iota(jnp.int32, (1, SEL_BLOCK), 1)
        msk = kpos <= jnp.where(valid_ref[b, g, k] > 0, pos, -1)
        sc = jnp.where(msk, _dot_nt(qr_ref[0, g], k_blk), NEG)
        m_old = m_scr[g]
        m_new = jnp.maximum(m_old, jnp.max(sc, axis=-1, keepdims=True))
        alpha = jnp.exp2(m_old - m_new)
        e = jnp.where(msk, jnp.exp2(sc - m_new), 0.0)
        l_scr[g] = alpha * l_scr[g] + jnp.sum(e, axis=-1, keepdims=True)
        acc_scr[g] = alpha * acc_scr[g] + _dot(e, v_blk)
        m_scr[g] = m_new

    @pl.when(k == pl.num_programs(1) - 1)
    def _():
        sig = jax.nn.sigmoid(gate_ref[0])
        for g in range(NSA_KV):
            o_s = acc_scr[g] / l_scr[g]
            gate = _gate_columns(sig, g, 1)
            o_c, o_w = oc_ref[0, g], ow_ref[0, g]
            for r in range(NSA_REP):
                h = g * NSA_REP + r
                o_ref[0, :, h * HEAD_DIM:(h + 1) * HEAD_DIM] = (
                    gate[r][0] * o_c[r:r + 1] + gate[r][1] * o_s[r:r + 1] + gate[r][2] * o_w[r:r + 1])


def _nsa_sample(zq, zgate, g_q, rope_row, kc, vc, kw_buf, vw_buf, n_win, pool_k, pool_v, k_new, v_new,
                page_table, pos):
    db = zq.shape[0]
    n_cmp = kc.shape[1]
    n_buf = kw_buf.shape[1]
    n_past = page_table.shape[1] * (PAGE_SIZE // SEL_BLOCK)
    n_sel = n_past + 1
    assert n_sel <= SEL_CAND_LANES
    qw = NSA_REP * HEAD_DIM
    vec = jax.ShapeDtypeStruct((db, NSA_KV, SUBLANE, HEAD_DIM), F32)
    ivec = jax.ShapeDtypeStruct((db, NSA_KV, 1, LANE), jnp.int32)
    vspec = pl.BlockSpec((1, NSA_KV, SUBLANE, HEAD_DIM), lambda i: (i, 0, 0, 0))
    ispec = pl.BlockSpec((1, NSA_KV, 1, LANE), lambda i: (i, 0, 0, 0))
    cmp_spec = pl.BlockSpec((1, n_cmp, KV_W), lambda i: (i, 0, 0))
    win_spec = pl.BlockSpec((1, n_buf, NSA_KV, HEAD_DIM), lambda i: (i, 0, 0, 0))
    qr, o_c, o_w, idx, valid = pl.pallas_call(
        partial(_nsa_sample_select_kernel, pos=pos, n_sel=n_sel, n_cmp=n_cmp, n_win=n_win),
        grid=(db,),
        in_specs=[
            pl.BlockSpec((1, 1, NSA_W), lambda i: (i, 0, 0)),
            pl.BlockSpec((1, HEAD_DIM), lambda i: (0, 0)),
            pl.BlockSpec((3, HEAD_DIM), lambda i: (0, 0)),
            cmp_spec, cmp_spec, win_spec, win_spec,
        ],
        out_specs=(vspec, vspec, vspec, ispec, ispec),
        out_shape=(vec, vec, vec, ivec, ivec),
        compiler_params=_params("parallel"),
        name="nsa_sample_select",
    )(zq, g_q.reshape(1, HEAD_DIM), rope_row, kc, vc, kw_buf, vw_buf)

    idx = idx[:, :, 0, :N_SELECT]
    valid = valid[:, :, 0, :N_SELECT]
    halves = PAGE_SIZE // SEL_BLOCK

    def pool_spec(g):
        def pool_block(i, k, idx_r, valid_r, pt_r):
            blk = jnp.minimum(idx_r[i, g, k], n_past - 1)
            return (pt_r[i, blk // halves] * halves + blk % halves, 0, 0, 0)
        return pl.BlockSpec((1, SEL_BLOCK, NSA_KV, HEAD_DIM), pool_block)

    n_pool = pool_k.shape[0]
    pool_specs = [pool_spec(g) for g in range(NSA_KV)]
    gvec = pl.BlockSpec((1, NSA_KV, SUBLANE, HEAD_DIM), lambda i, k, *_: (i, 0, 0, 0))
    new_spec = pl.BlockSpec((1, 1, NSA_KV, HEAD_DIM), lambda i, k, *_: (i, 0, 0, 0))
    grid_spec = pltpu.PrefetchScalarGridSpec(
        num_scalar_prefetch=3,
        grid=(db, N_SELECT),
        in_specs=pool_specs + pool_specs + [
            new_spec, new_spec, gvec, gvec, gvec,
            pl.BlockSpec((1, 1, LANE), lambda i, k, *_: (i, 0, 0)),
        ],
        out_specs=pl.BlockSpec((1, 1, NSA_W), lambda i, k, *_: (i, 0, 0)),
        scratch_shapes=[
            pltpu.VMEM((NSA_KV, SUBLANE, 1), F32),
            pltpu.VMEM((NSA_KV, SUBLANE, 1), F32),
            pltpu.VMEM((NSA_KV, SUBLANE, HEAD_DIM), F32),
        ],
    )
    pool_k2 = pool_k.reshape(n_pool * halves, SEL_BLOCK, NSA_KV, HEAD_DIM)
    pool_v2 = pool_v.reshape(n_pool * halves, SEL_BLOCK, NSA_KV, HEAD_DIM)
    return pl.pallas_call(
        partial(_nsa_sample_gather_kernel, pos=pos, n_past=n_past),
        grid_spec=grid_spec,
        out_shape=jax.ShapeDtypeStruct((db, 1, NSA_W), F32),
        compiler_params=_params("parallel", "arbitrary"),
        name="nsa_sample_gather",
    )(idx, valid, page_table, *([pool_k2] * NSA_KV), *([pool_v2] * NSA_KV),
      k_new, v_new, qr, o_c, o_w, zgate)


def kernel(x_prompt, x_sample, mem_prompt, cache_mem_k, cache_mem_v, cache_k_cmp, cache_v_cmp, cache_k_sel, cache_v_sel, state_k_win, state_v_win, page_table, g_ffn1, w_ffn1_gate, w_ffn1_up, w_ffn1_down, g_mix, g_ffn2, w_ffn2_gate, w_ffn2_up, w_ffn2_down, g_mem, w_mem_kv, g_mq, g_mk, w_in_a, ln_v_g, ln_v_b, w_spatial, b_spatial, w_out_a, g_kv, w_kv, pos_kc, w1_kc, w2_kc, pos_vc, w1_vc, w2_vc, g_kc, g_ks, g_kw, w_in_b, g_q, w_out_b):
    nb, t, d = x_prompt.shape
    db, ds, _ = x_sample.shape
    depth = g_ffn1.shape[0]
    n_a = w_in_a.shape[0]
    past_len = page_table.shape[1] * PAGE_SIZE
    assert ds == 1 and t % PAGE_SIZE == 0
    pos_s = past_len

    xp = x_prompt.reshape(nb * t, d)
    xs = x_sample.reshape(db * ds, d)
    mem = mem_prompt.reshape(nb * MEM_LEN, d)
    mem_k_list, mem_v_list, gm_v_list = [], [], []
    rope_p = _rope_tables(jnp.arange(t, dtype=jnp.int32))
    rope_s = _rope_tables(pos_s + jnp.arange(ds, dtype=jnp.int32))
    rope_p2 = jnp.tile(rope_p, (1, nb, 1))
    kv4 = lambda a, n: a.reshape(n, -1, NSA_KV, HEAD_DIM)

    for l in range(depth):
        if l == n_a:
            hp = _rms_mm(xp, g_kv, w_kv)
            hs = _rms_mm(xs, g_kv, w_kv)
            post_p = _nsa_kv_post(hp, rope_p2, g_ks, g_kw, True)
            kc_p_raw, vc_p_raw, ks_p, vs_p, kw_p, vw_p = [kv4(a, nb) for a in post_p[:6]]
            ks16, vst16, kw16, vwt16 = post_p[6:]
            kv16_p = [ks16.reshape(nb, t, KV_W), vst16, kw16.reshape(nb, t, KV_W), vwt16]
            post_s = _nsa_kv_post(hs, jnp.broadcast_to(rope_s, (3, db, HEAD_DIM)), g_ks, g_kw, False)
            kc_s_raw, vc_s_raw, ks_s, vs_s, kw_s, vw_s = [kv4(a, db) for a in post_s[:6]]
            pages_p = jnp.arange(nb * t // PAGE_SIZE, dtype=jnp.int32).reshape(nb, t // PAGE_SIZE)
            cmp_w = (pos_kc, w1_kc, w2_kc, pos_vc, w1_vc, w2_vc, g_kc)
            kc_p, vc_p = _compress(kc_p_raw.reshape(-1, PAGE_SIZE, NSA_KV, HEAD_DIM),
                                   vc_p_raw.reshape(-1, PAGE_SIZE, NSA_KV, HEAD_DIM), pages_p, *cmp_w)
            kc_s, vc_s = _compress(cache_k_cmp, cache_v_cmp, page_table, *cmp_w)
            wbuf = state_k_win.shape[1]
            n_win = wbuf + ds
            pad = jnp.zeros((db, (-n_win) % SUBLANE, NSA_KV, HEAD_DIM), F32)
            kw_buf = jnp.concatenate([state_k_win, kw_s, pad], axis=1)
            vw_buf = jnp.concatenate([state_v_win, vw_s, pad], axis=1)

        xp = _ffn(xp, g_ffn1, w_ffn1_gate, w_ffn1_up, w_ffn1_down, l)
        xs = _ffn(xs, g_ffn1, w_ffn1_gate, w_ffn1_up, w_ffn1_down, l)

        mk_p, mv_p = _mem_kv_post(_rms_mm(mem, g_mem[l], w_mem_kv, l), g_mk[l])
        mk_p = mk_p.reshape(nb, MEM_LEN, MEM_HEADS, HEAD_DIM)
        mv_p = mv_p.reshape(nb, MEM_LEN, MEM_HEADS, HEAD_DIM)
        mem_k_list.append(mk_p)
        mem_v_list.append(mv_p)
        mk_s, mv_s = cache_mem_k[l], cache_mem_v[l]

        if l < n_a:
            zp = _rms_mm(xp, g_mix[l], w_in_a, l)
            zs = _rms_mm(xs, g_mix[l], w_in_a, l)
            mem_blk = 2 * GM_W // MEM_W
            o1_p = _gmlp_prompt(zp, ln_v_g[l], ln_v_b[l], w_spatial[l], b_spatial[l])
            o1_s, v_s = _gmlp_first_row(zs, ln_v_g[l], ln_v_b[l], w_spatial[l], b_spatial[l])
            gm_v_list.append(v_s.reshape(db, ds, GM_W))
            o2_p = _mem_attend(zp.reshape(nb, t, -1), mem_blk, mk_p, mv_p, g_mq[l])
            o2_s = _mem_attend(zs.reshape(db, ds, -1), mem_blk, mk_s, mv_s, g_mq[l])
            w_out, l_out = w_out_a, l
        else:
            lb = l - n_a
            n_gate = N_GATES * NSA_HEADS
            w_qm = jnp.concatenate([w_in_b[lb][:, :NSA_W], w_in_b[lb][:, NSA_W + n_gate:]], axis=1)
            w_gate = jnp.pad(w_in_b[lb][:, NSA_W:NSA_W + n_gate], ((0, 0), (0, LANE - n_gate)))
            zp = _rms_mm(xp, g_mix[l], w_qm).reshape(nb, t, -1)
            zs = _rms_mm(xs, g_mix[l], w_qm).reshape(db, ds, -1)
            gp = _rms_mm(xp, g_mix[l], w_gate).reshape(nb, t, LANE)
            gs = _rms_mm(xs, g_mix[l], w_gate).reshape(db, ds, LANE)
            o1_p = _nsa_prompt(zp, gp, g_q[lb], rope_p, kc_p, vc_p, *kv16_p).reshape(nb * t, NSA_W)
            o1_s = _nsa_sample(zs, gs, g_q[lb], rope_s[:, 0, :], kc_s, vc_s, kw_buf, vw_buf, n_win,
                               cache_k_sel, cache_v_sel, ks_s, vs_s, page_table, pos_s).reshape(db * ds, NSA_W)
            mem_blk = NSA_W // MEM_W
            o2_p = _mem_attend(zp, mem_blk, mk_p, mv_p, g_mq[l])
            o2_s = _mem_attend(zs, mem_blk, mk_s, mv_s, g_mq[l])
            w_out, l_out = w_out_b, lb
        xp = _out_proj(o1_p, o2_p.reshape(nb * t, MEM_W), w_out, l_out, xp)
        xs = _out_proj(o1_s, o2_s.reshape(db * ds, MEM_W), w_out, l_out, xs)

        xp = _ffn(xp, g_ffn2, w_ffn2_gate, w_ffn2_up, w_ffn2_down, l)
        xs = _ffn(xs, g_ffn2, w_ffn2_gate, w_ffn2_up, w_ffn2_down, l)

    wp = min(WINDOW, t)
    ws = min(WINDOW, past_len + ds)
    return (xp.reshape(nb, t, d), xs.reshape(db, ds, d), jnp.stack(mem_k_list), jnp.stack(mem_v_list),
            kc_p_raw, vc_p_raw, ks_p, vs_p, kw_p[:, -wp:], vw_p[:, -wp:],
            kc_s_raw, vc_s_raw, ks_s, vs_s, kw_buf[:, n_win - ws:n_win], vw_buf[:, n_win - ws:n_win],
            jnp.stack(gm_v_list))
```

```python
from functools import partial

import numpy as np
import jax
import jax.numpy as jnp
from jax import lax
from jax.experimental import pallas as pl
from jax.experimental.pallas import tpu as pltpu

D_MODEL = 2048
HEAD_DIM = 128
ROT_DIM = HEAD_DIM // 4
ROPE_THETA = 500000.0
MEM_LEN = 256
MEM_HEADS = 4
MEM_W = MEM_HEADS * HEAD_DIM
GM_GROUPS = 12
GM_W = GM_GROUPS * HEAD_DIM
CHUNK = 128
NSA_HEADS = 12
NSA_KV = 4
NSA_REP = NSA_HEADS // NSA_KV
NSA_W = NSA_HEADS * HEAD_DIM
KV_W = NSA_KV * HEAD_DIM
N_GATES = 3
CMP_LEN = 32
CMP_STRIDE = 16
SEL_BLOCK = 64
N_SELECT = 16
WINDOW = 512
QB = 512
PAGE_SIZE = 128
EPS = 1e-6
NEG = -1e30
EXCLUDED = -3e38
FORCED_SCORE = 1e6
SCALE = HEAD_DIM ** -0.5
SCALE_LOG2E = float(SCALE * np.log2(np.e))

LANE = 128
SUBLANE = 8
VMEM_LIMIT = 56 * 1024 * 1024
ROW_TILE = 1024
SIDE_ROWS = 16
PAGES_PER_GROUP = 16
PAGES_PER_STEP = 4
CHUNKS_PER_PAGE = PAGE_SIZE // CMP_STRIDE
SEL_KEY_CHUNK = 512
SEL_CAND_LANES = 3 * LANE
SEL_SHIFT = SEL_BLOCK.bit_length() - 1

BF16 = jnp.bfloat16
F32 = jnp.float32


def _params(*sem):
    return pltpu.CompilerParams(dimension_semantics=sem, vmem_limit_bytes=VMEM_LIMIT)


def _dot(a, b):
    return jnp.dot(a.astype(BF16), b.astype(BF16), preferred_element_type=F32)


def _dot_nt(a, b):
    return lax.dot_general(a.astype(BF16), b.astype(BF16), (((1,), (1,)), ((), ())),
                           preferred_element_type=F32)


def _rms(x, g):
    return x * lax.rsqrt(jnp.mean(x * x, axis=-1, keepdims=True) + EPS) * g


def _gelu(x):
    return 0.5 * x * (1.0 + lax.erf(x * np.float32(np.sqrt(0.5))))


def _rope(x, c, s1, s2):
    half = ROT_DIM // 2
    return x * c + pltpu.roll(x, LANE - half, 1) * s1 + pltpu.roll(x, half, 1) * s2


def _masked_softmax2(s2, m):
    sm = jnp.where(m, s2, NEG)
    mx = jnp.max(sm, axis=-1, keepdims=True)
    e = jnp.where(m, jnp.exp2(sm - mx), 0.0)
    den = jnp.sum(e, axis=-1, keepdims=True)
    return e / jnp.where(den > 0.0, den, 1.0)


def _with_ones(v):
    return jnp.concatenate([v.astype(BF16), jnp.ones(v.shape, BF16)], axis=1)


def _row_tile(m, want):
    return want if m % want == 0 else m


def _resident(shape, index_map):
    return pl.BlockSpec(shape, index_map, pipeline_mode=pl.Buffered(1))


def _ffn_kernel(x_ref, xs_ref, g_ref, wg_ref, wu_ref, wd_ref, o_ref, os_ref, h_ref):
    i, j = pl.program_id(0), pl.program_id(1)
    tm = x_ref.shape[0]

    @pl.when(j == 0)
    def _():
        x = x_ref[...]
        h_ref[:tm] = _rms(x, g_ref[...]).astype(BF16)
        h_ref[tm:] = _rms(xs_ref[...], g_ref[...]).astype(BF16)
        o_ref[...] = x

    @pl.when((i == 0) & (j == 0))
    def _():
        os_ref[...] = xs_ref[...]

    h = h_ref[...]
    a = jnp.dot(h, wg_ref[...].astype(BF16), preferred_element_type=F32)
    b = jnp.dot(h, wu_ref[...].astype(BF16), preferred_element_type=F32)
    t = a * jax.nn.sigmoid(a) * b
    y = 0.5 * _dot(t, wd_ref[...])
    o_ref[...] += y[:tm]

    @pl.when(i == 0)
    def _():
        os_ref[...] += y[tm:]


def _ffn(x, xs, g, wg, wu, wd, l):
    m, d = x.shape
    f = wg.shape[2]
    tm = _row_tile(m, ROW_TILE)
    tf = 512
    return pl.pallas_call(
        _ffn_kernel,
        grid=(m // tm, f // tf),
        in_specs=[
            _resident((tm, d), lambda i, j: (i, 0)),
            pl.BlockSpec((SIDE_ROWS, d), lambda i, j: (0, 0)),
            pl.BlockSpec((1, d), lambda i, j: (0, 0)),
            pl.BlockSpec((None, d, tf), lambda i, j: (l, 0, j)),
            pl.BlockSpec((None, d, tf), lambda i, j: (l, 0, j)),
            pl.BlockSpec((None, tf, d), lambda i, j: (l, j, 0)),
        ],
        out_specs=(_resident((tm, d), lambda i, j: (i, 0)), pl.BlockSpec((SIDE_ROWS, d), lambda i, j: (0, 0))),
        out_shape=(jax.ShapeDtypeStruct((m, d), F32), jax.ShapeDtypeStruct((SIDE_ROWS, d), F32)),
        scratch_shapes=[pltpu.VMEM((tm + SIDE_ROWS, d), BF16)],
        compiler_params=_params("arbitrary", "arbitrary"),
        name="ffn",
    )(x, xs, g[l].reshape(1, d), wg, wu, wd)


def _rms_mm_kernel(*refs, tn, side):
    if side:
        x_ref, xs_ref, g_ref, w_ref, o_ref, os_ref, h_ref = refs
    else:
        x_ref, g_ref, w_ref, o_ref, h_ref = refs
    j = pl.program_id(1)
    tm = x_ref.shape[0]

    @pl.when(j == 0)
    def _():
        h_ref[:tm] = _rms(x_ref[...], g_ref[...]).astype(BF16)
        if side:
            h_ref[tm:] = _rms(xs_ref[...], g_ref[...]).astype(BF16)

    y = jnp.dot(h_ref[...], w_ref[:, pl.ds(pl.multiple_of(j * tn, tn), tn)], preferred_element_type=F32)
    o_ref[...] = y[:tm]
    if side:
        os_ref[...] = y[tm:]


def _whole_weight_spec(w, l):
    if w.ndim == 2:
        return _resident(w.shape, lambda i, j: (0, 0))
    return _resident((None,) + w.shape[1:], lambda i, j: (l, 0, 0))


def _rms_mm(x, xs, g, w, l=None):
    m, d = x.shape
    n = w.shape[-1]
    tm = _row_tile(m, ROW_TILE)
    tn = _row_tile(n, 512)
    side = xs is not None
    extra = SIDE_ROWS if side else 0
    x_spec = pl.BlockSpec((tm, d), lambda i, j: (i, 0))
    g_spec = pl.BlockSpec((1, d), lambda i, j: (0, 0))
    side_in = [pl.BlockSpec((SIDE_ROWS, d), lambda i, j: (0, 0))] if side else []
    out, ospec = jax.ShapeDtypeStruct((m, n), F32), pl.BlockSpec((tm, tn), lambda i, j: (i, j))
    if side:
        out = (out, jax.ShapeDtypeStruct((SIDE_ROWS, n), F32))
        ospec = (ospec, pl.BlockSpec((SIDE_ROWS, tn), lambda i, j: (0, j)))
    return pl.pallas_call(
        partial(_rms_mm_kernel, tn=tn, side=side),
        grid=(m // tm, n // tn),
        in_specs=[x_spec] + side_in + [g_spec, _whole_weight_spec(w, l)],
        out_specs=ospec,
        out_shape=out,
        scratch_shapes=[pltpu.VMEM((tm + extra, d), BF16)],
        compiler_params=_params("arbitrary", "arbitrary"),
        name="rms_mm",
    )(x, *([xs] if side else []), g.reshape(1, d), w)


def _out_proj_kernel(a1_ref, a2_ref, s1_ref, s2_ref, w_ref, x_ref, xs_ref, o_ref, os_ref, l1_scr, l2_scr, *, tn):
    j = pl.program_id(1)
    tm, k1 = a1_ref.shape

    @pl.when(j == 0)
    def _():
        l1_scr[:tm] = a1_ref[...]
        l1_scr[tm:] = s1_ref[...]
        l2_scr[:tm] = a2_ref[...]
        l2_scr[tm:] = s2_ref[...]

    cols = pl.ds(pl.multiple_of(j * tn, tn), tn)
    y = (jnp.dot(l1_scr[...], w_ref[:k1, cols], preferred_element_type=F32)
         + jnp.dot(l2_scr[...], w_ref[k1:, cols], preferred_element_type=F32))
    o_ref[...] = x_ref[...] + y[:tm]
    os_ref[...] = xs_ref[...] + y[tm:]


def _out_proj(a1, a2, s1, s2, w, l, x, xs):
    m, k1 = a1.shape
    k2 = a2.shape[1]
    d = w.shape[-1]
    tm = _row_tile(m, ROW_TILE)
    tn = 512
    tile = pl.BlockSpec((tm, tn), lambda i, j: (i, j))
    side_tile = pl.BlockSpec((SIDE_ROWS, tn), lambda i, j: (0, j))
    return pl.pallas_call(
        partial(_out_proj_kernel, tn=tn),
        grid=(m // tm, d // tn),
        in_specs=[
            pl.BlockSpec((tm, k1), lambda i, j: (i, 0)),
            pl.BlockSpec((tm, k2), lambda i, j: (i, 0)),
            pl.BlockSpec((SIDE_ROWS, k1), lambda i, j: (0, 0)),
            pl.BlockSpec((SIDE_ROWS, k2), lambda i, j: (0, 0)),
            _whole_weight_spec(w, l),
            tile, side_tile,
        ],
        out_specs=(tile, side_tile),
        out_shape=(jax.ShapeDtypeStruct((m, d), F32), jax.ShapeDtypeStruct((SIDE_ROWS, d), F32)),
        scratch_shapes=[pltpu.VMEM((tm + SIDE_ROWS, k1), BF16), pltpu.VMEM((tm + SIDE_ROWS, k2), BF16)],
        compiler_params=_params("arbitrary", "arbitrary"),
        name="out_proj",
    )(a1, a2, s1, s2, w, x, xs)


def _mem_kv_post_kernel(kv_ref, g_ref, k_ref, v_ref):
    g = g_ref[...]
    for h in range(MEM_HEADS):
        k_ref[:, h, :] = _rms(kv_ref[:, h * HEAD_DIM:(h + 1) * HEAD_DIM], g)
        v_ref[:, h, :] = kv_ref[:, MEM_W + h * HEAD_DIM:MEM_W + (h + 1) * HEAD_DIM]


def _mem_kv_post(kv, g_mk):
    m = kv.shape[0]
    out = jax.ShapeDtypeStruct((m, MEM_HEADS, HEAD_DIM), F32)
    return pl.pallas_call(
        _mem_kv_post_kernel,
        out_shape=(out, out),
        compiler_params=pltpu.CompilerParams(vmem_limit_bytes=VMEM_LIMIT),
        name="mem_kv_post",
    )(kv, g_mk.reshape(1, HEAD_DIM))


def _mem_attend_kernel(q_ref, g_ref, k_ref, v_ref, o_ref, *, rows):
    g = g_ref[...]
    for h in range(MEM_HEADS):
        sl = slice(h * HEAD_DIM, (h + 1) * HEAD_DIM)
        q = q_ref[0, :, sl]
        if rows < SUBLANE:
            q = jnp.broadcast_to(q[0:1], (SUBLANE, HEAD_DIM))
        q = _rms(q, g) * SCALE_LOG2E
        s2 = _dot_nt(q, k_ref[0, :, h, :])
        e = jnp.exp2(s2 - jnp.max(s2, axis=-1, keepdims=True))
        p = e / jnp.sum(e, axis=-1, keepdims=True)
        o = _dot(p, v_ref[0, :, h, :])
        o_ref[0, :, sl] = o[:rows].astype(o_ref.dtype)


def _mem_attend(z, col_block, mk, mv, g_mq):
    b, t, _ = z.shape
    tm = _row_tile(t, 512)
    kv_spec = pl.BlockSpec((1, MEM_LEN, MEM_HEADS, HEAD_DIM), lambda i, j: (i, 0, 0, 0))
    return pl.pallas_call(
        partial(_mem_attend_kernel, rows=tm),
        grid=(b, t // tm),
        in_specs=[
            pl.BlockSpec((1, tm, MEM_W), lambda i, j: (i, j, col_block)),
            pl.BlockSpec((1, HEAD_DIM), lambda i, j: (0, 0)),
            kv_spec, kv_spec,
        ],
        out_specs=pl.BlockSpec((1, tm, MEM_W), lambda i, j: (i, j, 0)),
        out_shape=jax.ShapeDtypeStruct((b, t, MEM_W), BF16),
        compiler_params=_params("parallel", "arbitrary"),
        name="mem_attend",
    )(z, g_mq.reshape(1, HEAD_DIM), mk, mv)


def _layer_norm(x, g, b):
    xc = x - jnp.mean(x, axis=-1, keepdims=True)
    var = jnp.mean(xc * xc, axis=-1, keepdims=True)
    return xc * lax.rsqrt(var + EPS) * g + b


def _gmlp_prompt_kernel(z_ref, lg_ref, lb_ref, ws_ref, bs_ref, o_ref):
    v = _layer_norm(_gelu(z_ref[:, GM_W:]), lg_ref[...], lb_ref[...])
    row = lax.broadcasted_iota(jnp.int32, (CHUNK, CHUNK), 0)
    col = lax.broadcasted_iota(jnp.int32, (CHUNK, CHUNK), 1)
    causal = col <= row
    for g in range(GM_GROUPS):
        sl = slice(g * HEAD_DIM, (g + 1) * HEAD_DIM)
        w = jnp.where(causal, ws_ref[g], 0.0)
        sv = _dot(w, v[:, sl]) + bs_ref[:, g:g + 1]
        o_ref[:, sl] = (_gelu(z_ref[:, sl]) * sv).astype(o_ref.dtype)


def _gmlp_prompt(z, ln_g, ln_b, w_s, b_s):
    m = z.shape[0]
    return pl.pallas_call(
        _gmlp_prompt_kernel,
        grid=(m // CHUNK,),
        in_specs=[
            pl.BlockSpec((CHUNK, 2 * GM_W), lambda i: (i, 0)),
            pl.BlockSpec((1, GM_W), lambda i: (0, 0)),
            pl.BlockSpec((1, GM_W), lambda i: (0, 0)),
            pl.BlockSpec((GM_GROUPS, CHUNK, CHUNK), lambda i: (0, 0, 0)),
            pl.BlockSpec((CHUNK, GM_GROUPS), lambda i: (0, 0)),
        ],
        out_specs=pl.BlockSpec((CHUNK, GM_W), lambda i: (i, 0)),
        out_shape=jax.ShapeDtypeStruct((m, GM_W), BF16),
        compiler_params=_params("parallel"),
        name="gmlp_prompt",
    )(z, ln_g.reshape(1, GM_W), ln_b.reshape(1, GM_W), w_s, b_s.T)


def _gmlp_first_row_kernel(z_ref, lg_ref, lb_ref, w0_ref, b0_ref, o_ref, v_ref):
    v = _layer_norm(_gelu(z_ref[:, GM_W:2 * GM_W]), lg_ref[...], lb_ref[...])
    v_ref[...] = v
    o_ref[...] = (_gelu(z_ref[:, :GM_W]) * (v * w0_ref[...] + b0_ref[...])).astype(o_ref.dtype)


def _gmlp_first_row(z, ln_g, ln_b, w_s, b_s):
    m = z.shape[0]
    w0 = jnp.repeat(w_s[:, 0, 0], HEAD_DIM).reshape(1, GM_W)
    b0 = jnp.repeat(b_s[:, 0], HEAD_DIM).reshape(1, GM_W)
    return pl.pallas_call(
        _gmlp_first_row_kernel,
        out_shape=(jax.ShapeDtypeStruct((m, GM_W), BF16), jax.ShapeDtypeStruct((m, GM_W), F32)),
        compiler_params=pltpu.CompilerParams(vmem_limit_bytes=VMEM_LIMIT),
        name="gmlp_first_row",
    )(z, ln_g.reshape(1, GM_W), ln_b.reshape(1, GM_W), w0, b0)


def _rope_tables(pos):
    half = ROT_DIM // 2
    inv = ROPE_THETA ** (-jnp.arange(half, dtype=F32) / half)
    ang = pos.astype(F32)[:, None] * inv[None, :]
    cos, sin = jnp.cos(ang), jnp.sin(ang)
    t = pos.shape[0]
    one = jnp.ones((t, HEAD_DIM - ROT_DIM), F32)
    zero = jnp.zeros((t, HEAD_DIM - ROT_DIM), F32)
    zh = jnp.zeros((t, half), F32)
    c = jnp.concatenate([cos, cos, one], axis=1)
    s1 = jnp.concatenate([-sin, zh, zero], axis=1)
    s2 = jnp.concatenate([zh, sin, zero], axis=1)
    return jnp.stack([c, s1, s2])


def _nsa_kv_post_kernel(h_ref, rope_ref, gs_ref, gw_ref, kc_ref, vc_ref, ks_ref, vs_ref, kw_ref, vw_ref,
                        *mxu_refs):
    c, s1, s2 = rope_ref[0], rope_ref[1], rope_ref[2]

    def part(i, g):
        return h_ref[:, i * KV_W + g * HEAD_DIM:i * KV_W + (g + 1) * HEAD_DIM]

    for g in range(NSA_KV):
        sl = slice(g * HEAD_DIM, (g + 1) * HEAD_DIM)
        kc_ref[:, g, :] = part(0, g)
        vc_ref[:, g, :] = part(1, g)
        ks = _rope(_rms(part(2, g), gs_ref[...]), c, s1, s2)
        kw = _rope(_rms(part(4, g), gw_ref[...]), c, s1, s2)
        ks_ref[:, g, :] = ks
        vs_ref[:, g, :] = part(3, g)
        kw_ref[:, g, :] = kw
        vw_ref[:, g, :] = part(5, g)
        if mxu_refs:
            ks16_ref, vst16_ref, kw16_ref, vwt16_ref = mxu_refs
            ks16_ref[:, sl] = ks.astype(BF16)
            kw16_ref[:, sl] = kw.astype(BF16)
            vst16_ref[sl, :] = part(3, g).T.astype(BF16)
            vwt16_ref[sl, :] = part(5, g).T.astype(BF16)


def _nsa_kv_post(h, rope_tab, g_ks, g_kw, mxu_copies):
    m = h.shape[0]
    tm = _row_tile(m, 512)
    out = jax.ShapeDtypeStruct((m, NSA_KV, HEAD_DIM), F32)
    ospec = pl.BlockSpec((tm, NSA_KV, HEAD_DIM), lambda i: (i, 0, 0))
    outs, ospecs = (out,) * 6, (ospec,) * 6
    if mxu_copies:
        k16, v16 = jax.ShapeDtypeStruct((m, KV_W), BF16), jax.ShapeDtypeStruct((KV_W, m), BF16)
        kspec, vspec = pl.BlockSpec((tm, KV_W), lambda i: (i, 0)), pl.BlockSpec((KV_W, tm), lambda i: (0, i))
        outs, ospecs = outs + (k16, v16, k16, v16), ospecs + (kspec, vspec, kspec, vspec)
    return pl.pallas_call(
        _nsa_kv_post_kernel,
        grid=(m // tm,),
        in_specs=[
            pl.BlockSpec((tm, 6 * KV_W), lambda i: (i, 0)),
            pl.BlockSpec((3, tm, HEAD_DIM), lambda i: (0, i, 0)),
            pl.BlockSpec((1, HEAD_DIM), lambda i: (0, 0)),
            pl.BlockSpec((1, HEAD_DIM), lambda i: (0, 0)),
        ],
        out_specs=ospecs,
        out_shape=outs,
        compiler_params=_params("parallel"),
        name="nsa_kv_post",
    )(h, rope_tab, g_ks.reshape(1, HEAD_DIM), g_kw.reshape(1, HEAD_DIM))


def _compress_kernel(pt_ref, *refs, n_pages):
    n_in = 2 * PAGES_PER_STEP
    page_refs = refs[:n_in]
    w1_refs, pe_refs, w2_refs = refs[n_in:n_in + 2], refs[n_in + 2:n_in + 4], refs[n_in + 4:n_in + 6]
    gk_ref = refs[n_in + 6]
    o_refs = refs[n_in + 7:n_in + 9]
    l_scr, a_scr, b_scr = refs[n_in + 9:]
    j = pl.program_id(1)
    steps_per_group = PAGES_PER_GROUP // PAGES_PER_STEP
    jj = j % steps_per_group
    n_chunks = n_pages * CHUNKS_PER_PAGE
    group_chunks = PAGES_PER_GROUP * CHUNKS_PER_PAGE

    @pl.when(j == 0)
    def _():
        b_scr[:, :, n_chunks:, :] = jnp.zeros((2, NSA_KV, SUBLANE, HEAD_DIM), F32)

    for t in range(2):
        for q in range(PAGES_PER_STEP):
            row0 = pl.multiple_of((jj * PAGES_PER_STEP + q) * CHUNKS_PER_PAGE, CHUNKS_PER_PAGE)
            page = page_refs[t * PAGES_PER_STEP + q]
            for p in range(CMP_STRIDE):
                for g in range(NSA_KV):
                    l_scr[t, g, pl.ds(row0, CHUNKS_PER_PAGE), p * HEAD_DIM:(p + 1) * HEAD_DIM] = (
                        page[0, pl.ds(p, CHUNKS_PER_PAGE, stride=CMP_STRIDE), g, :])

    @pl.when(jj == steps_per_group - 1)
    def _():
        base = pl.multiple_of((j // steps_per_group) * group_chunks, group_chunks)
        for t in range(2):
            for g in range(NSA_KV):
                r = _dot(l_scr[t, g], w1_refs[t][g])
                a_scr[t, g, pl.ds(base, group_chunks), :] = r[:, :HEAD_DIM]
                b_scr[t, g, pl.ds(base, group_chunks), :] = r[:, HEAD_DIM:]

    @pl.when(j == pl.num_programs(1) - 1)
    def _():
        for t in range(2):
            for g in range(NSA_KV):
                pw = _dot(pe_refs[t][g], w1_refs[t][g])
                bias = pw[0:1, :HEAD_DIM] + pw[1:2, HEAD_DIM:]
                h = a_scr[t, g] + b_scr[t, g, pl.ds(1, n_chunks), :] + bias
                y = _dot(_gelu(h), w2_refs[t][g])
                if t == 0:
                    y = _rms(y, gk_ref[...])
                o_refs[t][0, :, g * HEAD_DIM:(g + 1) * HEAD_DIM] = y


def _compress(pool_k, pool_v, page_table, pos_k, w1_k, w2_k, pos_v, w1_v, w2_v, g_kc):
    nb, n_pages = page_table.shape
    assert n_pages % PAGES_PER_GROUP == 0
    n_chunks = n_pages * CHUNKS_PER_PAGE
    kdim = CMP_STRIDE * HEAD_DIM

    def w1ab(w1):
        return jnp.concatenate([w1[:, :CMP_STRIDE].reshape(NSA_KV, kdim, HEAD_DIM),
                                w1[:, CMP_STRIDE:].reshape(NSA_KV, kdim, HEAD_DIM)], axis=-1).astype(BF16)

    def pe8(pos_enc):
        return jnp.pad(pos_enc.reshape(NSA_KV, 2, kdim), ((0, 0), (0, SUBLANE - 2), (0, 0)))

    def page_spec(q):
        return pl.BlockSpec((1, PAGE_SIZE, NSA_KV, HEAD_DIM),
                            lambda b, j, pt: (pt[b, j * PAGES_PER_STEP + q], 0, 0, 0))

    const3 = lambda b, j, pt: (0, 0, 0)
    page_specs = [page_spec(q) for q in range(PAGES_PER_STEP)]
    w1_spec = _resident((NSA_KV, kdim, 2 * HEAD_DIM), const3)
    pe_spec = _resident((NSA_KV, SUBLANE, kdim), const3)
    w2_spec = _resident((NSA_KV, HEAD_DIM, HEAD_DIM), const3)
    out = jax.ShapeDtypeStruct((nb, n_chunks, KV_W), F32)
    ospec = pl.BlockSpec((1, n_chunks, KV_W), lambda b, j, pt: (b, 0, 0))
    grid_spec = pltpu.PrefetchScalarGridSpec(
        num_scalar_prefetch=1,
        grid=(nb, n_pages // PAGES_PER_STEP),
        in_specs=page_specs + page_specs + [w1_spec, w1_spec, pe_spec, pe_spec, w2_spec, w2_spec,
                                             pl.BlockSpec((1, HEAD_DIM), lambda b, j, pt: (0, 0))],
        out_specs=(ospec, ospec),
        scratch_shapes=[
            pltpu.VMEM((2, NSA_KV, PAGES_PER_GROUP * CHUNKS_PER_PAGE, kdim), F32),
            pltpu.VMEM((2, NSA_KV, n_chunks, HEAD_DIM), F32),
            pltpu.VMEM((2, NSA_KV, n_chunks + SUBLANE, HEAD_DIM), F32),
        ],
    )
    return pl.pallas_call(
        partial(_compress_kernel, n_pages=n_pages),
        grid_spec=grid_spec,
        out_shape=(out, out),
        compiler_params=_params("parallel", "arbitrary"),
        name="compress",
    )(page_table, *([pool_k] * PAGES_PER_STEP), *([pool_v] * PAGES_PER_STEP),
      w1ab(w1_k), w1ab(w1_v), pe8(pos_k), pe8(pos_v), w2_k, w2_v, g_kc.reshape(1, HEAD_DIM))


def _gate_columns(sig, g, rows):
    lane = lax.broadcasted_iota(jnp.int32, (rows, LANE), 1)
    base = g * (NSA_REP * N_GATES)
    return [[jnp.sum(jnp.where(lane == base + r * N_GATES + t, sig, 0.0), axis=1, keepdims=True)
             for t in range(N_GATES)] for r in range(NSA_REP)]


def _rejected_t(score_t, avail_t, n_rows):
    groups = n_rows // SUBLANE
    rows = [score_t[r * SUBLANE:(r + 1) * SUBLANE] for r in range(groups)]
    sub = lax.broadcasted_iota(jnp.int32, rows[0].shape, 0)
    cnt = [jnp.zeros(rows[0].shape, F32) for _ in range(groups)]
    for i in range(n_rows):
        si = score_t[i:i + 1, :]
        gi, oi = divmod(i, SUBLANE)
        for r in range(groups):
            if r > gi:
                beats = si >= rows[r]
            elif r < gi:
                beats = si > rows[r]
            else:
                beats = (si > rows[r]) | ((si == rows[r]) & (sub > oi))
            cnt[r] = cnt[r] + jnp.where(beats, 1.0, 0.0)
    cnt = jnp.concatenate(cnt, axis=0)
    return jnp.where((cnt < N_SELECT) & avail_t, 0.0, 1.0)


def _nsa_prompt_kernel(q_ref, gate_ref, gq_ref, rope_ref, kc_ref, vc_ref, ks_ref, vst_ref, kw_ref, vwt_ref,
                       o_ref, m_scr, acc_scr, *, n_sel, n_cmp):
    g = pl.program_id(1)
    qb = pl.program_id(2)
    t0 = qb * QB
    rq = NSA_REP * QB
    c, s1, s2 = rope_ref[0], rope_ref[1], rope_ref[2]
    gq = gq_ref[...]
    qn_l, qr_l = [], []
    for r in range(NSA_REP):
        qn_r = _rms(q_ref[0, :, r * HEAD_DIM:(r + 1) * HEAD_DIM], gq)
        qn_l.append(qn_r * SCALE_LOG2E)
        qr_l.append(_rope(qn_r, c, s1, s2) * SCALE_LOG2E)
    qn = jnp.concatenate(qn_l, axis=0).astype(BF16)
    qr = jnp.concatenate(qr_l, axis=0).astype(BF16)
    pos1 = t0 + lax.broadcasted_iota(jnp.int32, (QB, 1), 0)
    pos3 = jnp.concatenate([pos1] * NSA_REP, axis=0)

    ci = lax.broadcasted_iota(jnp.int32, (1, n_cmp), 1)
    p = _masked_softmax2(_dot_nt(qn, kc_ref[0]), ci * CMP_STRIDE + (CMP_LEN - 1) <= pos3)
    o_c = _dot(p, vc_ref[0])
    psum = p[0:QB] + p[QB:2 * QB] + p[2 * QB:3 * QB]

    ci_l = lax.broadcasted_iota(jnp.int32, (n_sel, n_cmp), 1) * CMP_STRIDE
    sj = lax.broadcasted_iota(jnp.int32, (n_sel, n_cmp), 0) * SEL_BLOCK
    cover_t = jnp.where((ci_l < sj + SEL_BLOCK) & (ci_l + CMP_LEN > sj), 1.0, 0.0)
    imp_t = lax.dot_general(cover_t, psum, (((1,), (1,)), ((), ())),
                            precision=lax.Precision.HIGHEST, preferred_element_type=F32)
    post = t0 + lax.broadcasted_iota(jnp.int32, (n_sel, QB), 1)
    jt = lax.broadcasted_iota(jnp.int32, (n_sel, QB), 0)
    cur = post >> SEL_SHIFT
    avail_t = jt * SEL_BLOCK <= post
    forced_t = (jt == 0) | (jt == cur) | (jt == cur - 1)
    score_t = jnp.where(avail_t, jnp.where(forced_t, FORCED_SCORE, imp_t), NEG)
    rej_t = _rejected_t(score_t, avail_t, n_sel)
    rej_t = jnp.concatenate([rej_t, jnp.zeros((LANE - n_sel, QB), F32)], axis=0).astype(BF16)

    pos_l = jnp.concatenate([t0 + lax.broadcasted_iota(jnp.int32, (1, QB), 1)] * NSA_REP, axis=1)

    def values_t(vt):
        return jnp.concatenate([vt, jnp.ones(vt.shape, BF16)], axis=0)

    def normalised(acc_t):
        o_t = acc_t[:HEAD_DIM] / acc_t[HEAD_DIM:HEAD_DIM + 1]
        return [o_t[:, r * QB:(r + 1) * QB].T for r in range(NSA_REP)]

    m_scr[...] = jnp.full((1, rq), NEG, F32)
    acc_scr[...] = jnp.zeros((2 * HEAD_DIM, rq), F32)
    blocks_per_chunk = SEL_KEY_CHUNK // SEL_BLOCK

    def sel_step(ck, causal):
        k0 = pl.multiple_of(ck * SEL_KEY_CHUNK, SEL_KEY_CHUNK)
        kr = lax.broadcasted_iota(jnp.int32, (SEL_KEY_CHUNK, LANE), 0)
        jl = lax.broadcasted_iota(jnp.int32, (SEL_KEY_CHUNK, LANE), 1)
        expand_t = jnp.where((kr >> SEL_SHIFT) + ck * blocks_per_chunk == jl, NEG, 0.0).astype(BF16)
        bias_t = jnp.dot(expand_t, rej_t, preferred_element_type=F32)
        sc = _dot_nt(ks_ref[0, pl.ds(k0, SEL_KEY_CHUNK), :], qr) + jnp.concatenate([bias_t] * NSA_REP, axis=1)
        if causal:
            kpos = k0 + lax.broadcasted_iota(jnp.int32, (SEL_KEY_CHUNK, 1), 0)
            sc = jnp.where(kpos <= pos_l, sc, NEG)
        m_old = m_scr[...]
        m_new = jnp.maximum(m_old, jnp.max(sc, axis=0, keepdims=True))
        e = jnp.exp2(sc - m_new).astype(BF16)
        acc_scr[...] = (jnp.exp2(m_old - m_new) * acc_scr[...]
                        + jnp.dot(values_t(vst_ref[:, pl.ds(k0, SEL_KEY_CHUNK)]), e, preferred_element_type=F32))
        m_scr[...] = m_new

    n_chunks = (t0 + QB + SEL_KEY_CHUNK - 1) // SEL_KEY_CHUNK

    def sel_body(ck, carry):
        sel_step(ck, False)
        return carry

    lax.fori_loop(0, n_chunks - 1, sel_body, 0)
    sel_step(n_chunks - 1, True)
    o_s = normalised(acc_scr[...])

    band = WINDOW + QB
    k0 = pl.multiple_of(jnp.maximum(t0 - WINDOW, 0), QB)
    kq = k0 + lax.broadcasted_iota(jnp.int32, (band, 1), 0)
    sw = jnp.where((kq >= pos_l - WINDOW) & (kq <= pos_l), _dot_nt(kw_ref[0, pl.ds(k0, band), :], qr), NEG)
    ew = jnp.exp2(sw - jnp.max(sw, axis=0, keepdims=True)).astype(BF16)
    o_w = normalised(jnp.dot(values_t(vwt_ref[:, pl.ds(k0, band)]), ew, preferred_element_type=F32))

    gate = _gate_columns(jax.nn.sigmoid(gate_ref[0]), g, QB)
    for r in range(NSA_REP):
        o_ref[0, :, r * HEAD_DIM:(r + 1) * HEAD_DIM] = (
            gate[r][0] * o_c[r * QB:(r + 1) * QB] + gate[r][1] * o_s[r] + gate[r][2] * o_w[r]).astype(o_ref.dtype)


def _nsa_prompt(z, gates, g_q, rope_tab, kc, vc, ks, vst, kw, vwt):
    b, t, _ = z.shape
    n_cmp = kc.shape[1]
    assert t >= WINDOW + QB
    qw = NSA_REP * HEAD_DIM
    k_spec = pl.BlockSpec((1, t, HEAD_DIM), lambda i, g, j: (i, 0, g))
    vt_spec = pl.BlockSpec((HEAD_DIM, t), lambda i, g, j: (g, i))
    cmp_spec = pl.BlockSpec((1, n_cmp, HEAD_DIM), lambda i, g, j: (i, 0, g))
    return pl.pallas_call(
        partial(_nsa_prompt_kernel, n_sel=t // SEL_BLOCK, n_cmp=n_cmp),
        grid=(b, NSA_KV, t // QB),
        in_specs=[
            pl.BlockSpec((1, QB, qw), lambda i, g, j: (i, j, g)),
            pl.BlockSpec((1, QB, LANE), lambda i, g, j: (i, j, 0)),
            pl.BlockSpec((1, HEAD_DIM), lambda i, g, j: (0, 0)),
            pl.BlockSpec((3, QB, HEAD_DIM), lambda i, g, j: (0, j, 0)),
            cmp_spec, cmp_spec, k_spec, vt_spec, k_spec, vt_spec,
        ],
        out_specs=pl.BlockSpec((1, QB, qw), lambda i, g, j: (i, j, g)),
        out_shape=jax.ShapeDtypeStruct((b, t, NSA_W), BF16),
        scratch_shapes=[
            pltpu.VMEM((1, NSA_REP * QB), F32),
            pltpu.VMEM((2 * HEAD_DIM, NSA_REP * QB), F32),
        ],
        compiler_params=_params("parallel", "parallel", "arbitrary"),
        name="nsa_prompt",
    )(z, gates, g_q.reshape(1, HEAD_DIM), rope_tab, kc, vc, ks, vst, kw, vwt)


def _nsa_sample_select_kernel(q_ref, gq_ref, rope_ref, kc_ref, vc_ref, kw_ref, vw_ref,
                              qr_ref, oc_ref, ow_ref, idx_ref, valid_ref, **static):
    for g in range(NSA_KV):
        sl = slice(g * HEAD_DIM, (g + 1) * HEAD_DIM)
        q3 = [q_ref[0, :, (g * NSA_REP + r) * HEAD_DIM:(g * NSA_REP + r + 1) * HEAD_DIM] for r in range(NSA_REP)]
        qr_ref[0, g], oc_ref[0, g], ow_ref[0, g], idx_ref[0, g], valid_ref[0, g] = _select_group(
            q3, gq_ref[...], rope_ref, kc_ref[0, :, sl], vc_ref[0, :, sl], kw_ref[0, :, g, :], vw_ref[0, :, g, :],
            **static)


def _select_group(q3, gq, rope_ref, kc, vc, kw, vw, *, pos, n_sel, n_cmp, n_win):
    n_cand = SEL_CAND_LANES
    c, s1, s2 = rope_ref[0:1], rope_ref[1:2], rope_ref[2:3]
    row = lax.broadcasted_iota(jnp.int32, (SUBLANE, HEAD_DIM), 0)
    q3 = [jnp.broadcast_to(q, (SUBLANE, HEAD_DIM)) for q in q3]
    q8 = jnp.where(row == 0, q3[0], jnp.where(row == 1, q3[1], q3[2]))
    qn = _rms(q8, gq)
    qr = _rope(qn, c, s1, s2) * SCALE_LOG2E

    ci = lax.broadcasted_iota(jnp.int32, (1, n_cmp), 1)
    p = _masked_softmax2(_dot_nt(qn * SCALE_LOG2E, kc), ci * CMP_STRIDE + (CMP_LEN - 1) <= pos)
    o_c = _dot(p, vc)
    psum = p[0:1] + p[1:2] + p[2:3]

    ci_s = lax.broadcasted_iota(jnp.int32, (n_cmp, n_cand), 0) * CMP_STRIDE
    sj = lax.broadcasted_iota(jnp.int32, (n_cmp, n_cand), 1) * SEL_BLOCK
    cover = jnp.where((ci_s < sj + SEL_BLOCK) & (ci_s + CMP_LEN > sj), 1.0, 0.0)
    imp = jnp.dot(jnp.broadcast_to(psum, (SUBLANE, n_cmp)), cover,
                  precision=lax.Precision.HIGHEST, preferred_element_type=F32)[0:1]
    jl = lax.broadcasted_iota(jnp.int32, (1, n_cand), 1)
    cur = pos // SEL_BLOCK
    avail = jl * SEL_BLOCK <= pos
    forced = (jl == 0) | (jl == cur) | (jl == cur - 1)
    score = jnp.where(avail, jnp.where(forced, FORCED_SCORE, imp), NEG)
    score = jnp.where(jl < n_sel, score, EXCLUDED)

    eye = (lax.broadcasted_iota(jnp.int32, (n_cand, n_cand), 0)
           == lax.broadcasted_iota(jnp.int32, (n_cand, n_cand), 1))
    score_b = jnp.broadcast_to(score, (n_cand, n_cand))
    score_c = jnp.sum(jnp.where(eye, score_b, 0.0), axis=1, keepdims=True)
    il = lax.broadcasted_iota(jnp.int32, (n_cand, n_cand), 1)
    jc = lax.broadcasted_iota(jnp.int32, (n_cand, n_cand), 0)
    beats = (score_b > score_c) | ((score_b == score_c) & (il < jc))
    rank = jnp.sum(jnp.where(beats, 1.0, 0.0), axis=1, keepdims=True)
    kl = lax.broadcasted_iota(jnp.int32, (n_cand, LANE), 1).astype(F32)
    hit = rank == kl
    jcol = lax.broadcasted_iota(jnp.int32, (n_cand, LANE), 0)
    idx = jnp.sum(jnp.where(hit, jcol.astype(F32), 0.0), axis=0, keepdims=True).astype(jnp.int32)
    valid = jnp.sum(jnp.where(hit & (jcol * SEL_BLOCK <= pos), 1.0, 0.0),
                    axis=0, keepdims=True).astype(jnp.int32)

    n_buf = kw.shape[0]
    kpos = pos + 1 - n_win + lax.broadcasted_iota(jnp.int32, (1, n_buf), 1)
    pw = _masked_softmax2(_dot_nt(qr, kw), (kpos >= pos - WINDOW) & (kpos <= pos))
    return qr, o_c, _dot(pw, vw), idx, valid


def _nsa_sample_gather_kernel(idx_ref, valid_ref, pt_ref, *refs, pos, n_past):
    kp_refs, vp_refs = refs[:NSA_KV], refs[NSA_KV:2 * NSA_KV]
    kn_ref, vn_ref, qr_ref, oc_ref, ow_ref, gate_ref, o_ref, m_scr, l_scr, acc_scr = refs[2 * NSA_KV:]
    b, k = pl.program_id(0), pl.program_id(1)

    @pl.when(k == 0)
    def _():
        m_scr[...] = jnp.full(m_scr.shape, NEG, F32)
        l_scr[...] = jnp.zeros(l_scr.shape, F32)
        acc_scr[...] = jnp.zeros(acc_scr.shape, F32)

    row = lax.broadcasted_iota(jnp.int32, (SEL_BLOCK, HEAD_DIM), 0)
    for g in range(NSA_KV):
        idx = idx_ref[b, g, k]
        in_past = idx < n_past
        k_blk = jnp.where(in_past, kp_refs[g][0, :, g, :], jnp.where(row == 0, kn_ref[0, :, g, :], 0.0))
        v_blk = jnp.where(in_past, vp_refs[g][0, :, g, :], jnp.where(row == 0, vn_ref[0, :, g, :], 0.0))
        kpos = idx * SEL_BLOCK + lax.broadcasted_iota(jnp.int32, (1, SEL_BLOCK), 1)
        msk = kpos <= jnp.where(valid_ref[b, g, k] > 0, pos, -1)
        sc = jnp.where(msk, _dot_nt(qr_ref[0, g], k_blk), NEG)
        m_old = m_scr[g]
        m_new = jnp.maximum(m_old, jnp.max(sc, axis=-1, keepdims=True))
        alpha = jnp.exp2(m_old - m_new)
        e = jnp.where(msk, jnp.exp2(sc - m_new), 0.0)
        l_scr[g] = alpha * l_scr[g] + jnp.sum(e, axis=-1, keepdims=True)
        acc_scr[g] = alpha * acc_scr[g] + _dot(e, v_blk)
        m_scr[g] = m_new

    @pl.when(k == pl.num_programs(1) - 1)
    def _():
        sig = jax.nn.sigmoid(gate_ref[0])
        for g in range(NSA_KV):
            o_s = acc_scr[g] / l_scr[g]
            gate = _gate_columns(sig, g, 1)
            o_c, o_w = oc_ref[0, g], ow_ref[0, g]
            for r in range(NSA_REP):
                h = g * NSA_REP + r
                o_ref[0, :, h * HEAD_DIM:(h + 1) * HEAD_DIM] = (
                    gate[r][0] * o_c[r:r + 1] + gate[r][1] * o_s[r:r + 1] + gate[r][2] * o_w[r:r + 1]
                ).astype(o_ref.dtype)


def _nsa_sample(zq, zgate, g_q, rope_row, kc, vc, kw_buf, vw_buf, n_win, pool_k, pool_v, k_new, v_new,
                page_table, pos):
    db = zq.shape[0]
    n_cmp = kc.shape[1]
    n_buf = kw_buf.shape[1]
    n_past = page_table.shape[1] * (PAGE_SIZE // SEL_BLOCK)
    n_sel = n_past + 1
    assert n_sel <= SEL_CAND_LANES
    qw = NSA_REP * HEAD_DIM
    vec = jax.ShapeDtypeStruct((db, NSA_KV, SUBLANE, HEAD_DIM), F32)
    ivec = jax.ShapeDtypeStruct((db, NSA_KV, 1, LANE), jnp.int32)
    vspec = pl.BlockSpec((1, NSA_KV, SUBLANE, HEAD_DIM), lambda i: (i, 0, 0, 0))
    ispec = pl.BlockSpec((1, NSA_KV, 1, LANE), lambda i: (i, 0, 0, 0))
    cmp_spec = pl.BlockSpec((1, n_cmp, KV_W), lambda i: (i, 0, 0))
    win_spec = pl.BlockSpec((1, n_buf, NSA_KV, HEAD_DIM), lambda i: (i, 0, 0, 0))
    qr, o_c, o_w, idx, valid = pl.pallas_call(
        partial(_nsa_sample_select_kernel, pos=pos, n_sel=n_sel, n_cmp=n_cmp, n_win=n_win),
        grid=(db,),
        in_specs=[
            pl.BlockSpec((1, 1, NSA_W), lambda i: (i, 0, 0)),
            pl.BlockSpec((1, HEAD_DIM), lambda i: (0, 0)),
            pl.BlockSpec((3, HEAD_DIM), lambda i: (0, 0)),
            cmp_spec, cmp_spec, win_spec, win_spec,
        ],
        out_specs=(vspec, vspec, vspec, ispec, ispec),
        out_shape=(vec, vec, vec, ivec, ivec),
        compiler_params=_params("parallel"),
        name="nsa_sample_select",
    )(zq, g_q.reshape(1, HEAD_DIM), rope_row, kc, vc, kw_buf, vw_buf)

    idx = idx[:, :, 0, :N_SELECT]
    valid = valid[:, :, 0, :N_SELECT]
    halves = PAGE_SIZE // SEL_BLOCK

    def pool_spec(g):
        def pool_block(i, k, idx_r, valid_r, pt_r):
            blk = jnp.minimum(idx_r[i, g, k], n_past - 1)
            return (pt_r[i, blk // halves] * halves + blk % halves, 0, 0, 0)
        return pl.BlockSpec((1, SEL_BLOCK, NSA_KV, HEAD_DIM), pool_block)

    n_pool = pool_k.shape[0]
    pool_specs = [pool_spec(g) for g in range(NSA_KV)]
    gvec = pl.BlockSpec((1, NSA_KV, SUBLANE, HEAD_DIM), lambda i, k, *_: (i, 0, 0, 0))
    new_spec = pl.BlockSpec((1, 1, NSA_KV, HEAD_DIM), lambda i, k, *_: (i, 0, 0, 0))
    grid_spec = pltpu.PrefetchScalarGridSpec(
        num_scalar_prefetch=3,
        grid=(db, N_SELECT),
        in_specs=pool_specs + pool_specs + [
            new_spec, new_spec, gvec, gvec, gvec,
            pl.BlockSpec((1, 1, LANE), lambda i, k, *_: (i, 0, 0)),
        ],
        out_specs=pl.BlockSpec((1, 1, NSA_W), lambda i, k, *_: (i, 0, 0)),
        scratch_shapes=[
            pltpu.VMEM((NSA_KV, SUBLANE, 1), F32),
            pltpu.VMEM((NSA_KV, SUBLANE, 1), F32),
            pltpu.VMEM((NSA_KV, SUBLANE, HEAD_DIM), F32),
        ],
    )
    pool_k2 = pool_k.reshape(n_pool * halves, SEL_BLOCK, NSA_KV, HEAD_DIM)
    pool_v2 = pool_v.reshape(n_pool * halves, SEL_BLOCK, NSA_KV, HEAD_DIM)
    return pl.pallas_call(
        partial(_nsa_sample_gather_kernel, pos=pos, n_past=n_past),
        grid_spec=grid_spec,
        out_shape=jax.ShapeDtypeStruct((db, 1, NSA_W), BF16),
        compiler_params=_params("parallel", "arbitrary"),
        name="nsa_sample_gather",
    )(idx, valid, page_table, *([pool_k2] * NSA_KV), *([pool_v2] * NSA_KV),
      k_new, v_new, qr, o_c, o_w, zgate)


def kernel(x_prompt, x_sample, mem_prompt, cache_mem_k, cache_mem_v, cache_k_cmp, cache_v_cmp, cache_k_sel, cache_v_sel, state_k_win, state_v_win, page_table, g_ffn1, w_ffn1_gate, w_ffn1_up, w_ffn1_down, g_mix, g_ffn2, w_ffn2_gate, w_ffn2_up, w_ffn2_down, g_mem, w_mem_kv, g_mq, g_mk, w_in_a, ln_v_g, ln_v_b, w_spatial, b_spatial, w_out_a, g_kv, w_kv, pos_kc, w1_kc, w2_kc, pos_vc, w1_vc, w2_vc, g_kc, g_ks, g_kw, w_in_b, g_q, w_out_b):
    nb, t, d = x_prompt.shape
    db, ds, _ = x_sample.shape
    depth = g_ffn1.shape[0]
    n_a = w_in_a.shape[0]
    past_len = page_table.shape[1] * PAGE_SIZE
    assert ds == 1 and t % PAGE_SIZE == 0 and db * ds <= SIDE_ROWS
    pos_s = past_len
    ns = db * ds
    side = lambda a: jnp.pad(a, ((0, SIDE_ROWS - ns), (0, 0)))

    xp = x_prompt.reshape(nb * t, d)
    xs = side(x_sample.reshape(ns, d))
    mem = mem_prompt.reshape(nb * MEM_LEN, d)
    w_kv16, w_mem_kv16, w_in_a16 = w_kv.astype(BF16), w_mem_kv.astype(BF16), w_in_a.astype(BF16)
    w_out_a16, w_out_b16 = w_out_a.astype(BF16), w_out_b.astype(BF16)
    mem_k_list, mem_v_list, gm_v_list = [], [], []
    rope_p = _rope_tables(jnp.arange(t, dtype=jnp.int32))
    rope_s = _rope_tables(pos_s + jnp.arange(ds, dtype=jnp.int32))
    rope_p2 = jnp.tile(rope_p, (1, nb, 1))
    kv4 = lambda a, n: a.reshape(n, -1, NSA_KV, HEAD_DIM)

    for l in range(depth):
        if l == n_a:
            hp, hs = _rms_mm(xp, xs, g_kv, w_kv16)
            hs = hs[:ns]
            post_p = _nsa_kv_post(hp, rope_p2, g_ks, g_kw, True)
            kc_p_raw, vc_p_raw, ks_p, vs_p, kw_p, vw_p = [kv4(a, nb) for a in post_p[:6]]
            ks16, vst16, kw16, vwt16 = post_p[6:]
            kv16_p = [ks16.reshape(nb, t, KV_W), vst16, kw16.reshape(nb, t, KV_W), vwt16]
            post_s = _nsa_kv_post(hs, jnp.broadcast_to(rope_s, (3, db, HEAD_DIM)), g_ks, g_kw, False)
            kc_s_raw, vc_s_raw, ks_s, vs_s, kw_s, vw_s = [kv4(a, db) for a in post_s[:6]]
            pages_p = jnp.arange(nb * t // PAGE_SIZE, dtype=jnp.int32).reshape(nb, t // PAGE_SIZE)
            cmp_w = (pos_kc, w1_kc, w2_kc, pos_vc, w1_vc, w2_vc, g_kc)
            kc_p, vc_p = _compress(kc_p_raw.reshape(-1, PAGE_SIZE, NSA_KV, HEAD_DIM),
                                   vc_p_raw.reshape(-1, PAGE_SIZE, NSA_KV, HEAD_DIM), pages_p, *cmp_w)
            kc_s, vc_s = _compress(cache_k_cmp, cache_v_cmp, page_table, *cmp_w)
            wbuf = state_k_win.shape[1]
            n_win = wbuf + ds
            pad = jnp.zeros((db, (-n_win) % SUBLANE, NSA_KV, HEAD_DIM), F32)
            kw_buf = jnp.concatenate([state_k_win, kw_s, pad], axis=1)
            vw_buf = jnp.concatenate([state_v_win, vw_s, pad], axis=1)

        xp, xs = _ffn(xp, xs, g_ffn1, w_ffn1_gate, w_ffn1_up, w_ffn1_down, l)

        mk_p, mv_p = _mem_kv_post(_rms_mm(mem, None, g_mem[l], w_mem_kv16, l), g_mk[l])
        mk_p = mk_p.reshape(nb, MEM_LEN, MEM_HEADS, HEAD_DIM)
        mv_p = mv_p.reshape(nb, MEM_LEN, MEM_HEADS, HEAD_DIM)
        mem_k_list.append(mk_p)
        mem_v_list.append(mv_p)
        mk_s, mv_s = cache_mem_k[l], cache_mem_v[l]

        if l < n_a:
            zp, zs = _rms_mm(xp, xs, g_mix[l], w_in_a16, l)
            zs = zs[:ns]
            mem_blk = 2 * GM_W // MEM_W
            o1_p = _gmlp_prompt(zp, ln_v_g[l], ln_v_b[l], w_spatial[l], b_spatial[l])
            o1_s, v_s = _gmlp_first_row(zs, ln_v_g[l], ln_v_b[l], w_spatial[l], b_spatial[l])
            gm_v_list.append(v_s.reshape(db, ds, GM_W))
            o2_p = _mem_attend(zp.reshape(nb, t, -1), mem_blk, mk_p, mv_p, g_mq[l])
            o2_s = _mem_attend(zs.reshape(db, ds, -1), mem_blk, mk_s, mv_s, g_mq[l])
            w_out, l_out = w_out_a16, l
        else:
            lb = l - n_a
            n_gate = N_GATES * NSA_HEADS
            w_qm = jnp.concatenate([w_in_b[lb][:, :NSA_W], w_in_b[lb][:, NSA_W + n_gate:]], axis=1).astype(BF16)
            w_gate = jnp.pad(w_in_b[lb][:, NSA_W:NSA_W + n_gate], ((0, 0), (0, LANE - n_gate))).astype(BF16)
            zp, zs = _rms_mm(xp, xs, g_mix[l], w_qm)
            gp, gs = _rms_mm(xp, xs, g_mix[l], w_gate)
            zp, zs = zp.reshape(nb, t, -1), zs[:ns].reshape(db, ds, -1)
            gp, gs = gp.reshape(nb, t, LANE), gs[:ns].reshape(db, ds, LANE)
            o1_p = _nsa_prompt(zp, gp, g_q[lb], rope_p, kc_p, vc_p, *kv16_p).reshape(nb * t, NSA_W)
            o1_s = _nsa_sample(zs, gs, g_q[lb], rope_s[:, 0, :], kc_s, vc_s, kw_buf, vw_buf, n_win,
                               cache_k_sel, cache_v_sel, ks_s, vs_s, page_table, pos_s).reshape(db * ds, NSA_W)
            mem_blk = NSA_W // MEM_W
            o2_p = _mem_attend(zp, mem_blk, mk_p, mv_p, g_mq[l])
            o2_s = _mem_attend(zs, mem_blk, mk_s, mv_s, g_mq[l])
            w_out, l_out = w_out_b16, lb
        xp, xs = _out_proj(o1_p, o2_p.reshape(nb * t, MEM_W), side(o1_s), side(o2_s.reshape(ns, MEM_W)),
                           w_out, l_out, xp, xs)

        xp, xs = _ffn(xp, xs, g_ffn2, w_ffn2_gate, w_ffn2_up, w_ffn2_down, l)

    wp = min(WINDOW, t)
    ws = min(WINDOW, past_len + ds)
    return (xp.reshape(nb, t, d), xs[:ns].reshape(db, ds, d), jnp.stack(mem_k_list), jnp.stack(mem_v_list),
            kc_p_raw, vc_p_raw, ks_p, vs_p, kw_p[:, -wp:], vw_p[:, -wp:],
            kc_s_raw, vc_s_raw, ks_s, vs_s, kw_buf[:, n_win - ws:n_win], vw_buf[:, n_win - ws:n_win],
            jnp.stack(gm_v_list))
```

```python
from functools import partial

import numpy as np
import jax
import jax.numpy as jnp
from jax import lax
from jax.experimental import pallas as pl
from jax.experimental.pallas import tpu as pltpu

D_MODEL = 2048
HEAD_DIM = 128
ROT_DIM = HEAD_DIM // 4
ROPE_THETA = 500000.0
MEM_LEN = 256
MEM_HEADS = 4
MEM_W = MEM_HEADS * HEAD_DIM
GM_GROUPS = 12
GM_W = GM_GROUPS * HEAD_DIM
CHUNK = 128
NSA_HEADS = 12
NSA_KV = 4
NSA_REP = NSA_HEADS // NSA_KV
NSA_W = NSA_HEADS * HEAD_DIM
KV_W = NSA_KV * HEAD_DIM
N_GATES = 3
CMP_LEN = 32
CMP_STRIDE = 16
SEL_BLOCK = 64
N_SELECT = 16
WINDOW = 512
QB = 512
PAGE_SIZE = 128
EPS = 1e-6
NEG = -1e30
EXCLUDED = -3e38
FORCED_SCORE = 1e6
SCALE = HEAD_DIM ** -0.5
SCALE_LOG2E = float(SCALE * np.log2(np.e))

LANE = 128
SUBLANE = 8
VMEM_LIMIT = 56 * 1024 * 1024
ROW_TILE = 1024
SIDE_ROWS = 16
PAGES_PER_GROUP = 16
PAGES_PER_STEP = 4
CHUNKS_PER_PAGE = PAGE_SIZE // CMP_STRIDE
SEL_KEY_CHUNK = 512
SEL_CAND_LANES = 3 * LANE
SEL_SHIFT = SEL_BLOCK.bit_length() - 1

BF16 = jnp.bfloat16
F32 = jnp.float32


def _params(*sem):
    return pltpu.CompilerParams(dimension_semantics=sem, vmem_limit_bytes=VMEM_LIMIT)


def _dot(a, b):
    return jnp.dot(a.astype(BF16), b.astype(BF16), preferred_element_type=F32)


def _dot_nt(a, b):
    return lax.dot_general(a.astype(BF16), b.astype(BF16), (((1,), (1,)), ((), ())),
                           preferred_element_type=F32)


def _rms(x, g):
    return x * lax.rsqrt(jnp.mean(x * x, axis=-1, keepdims=True) + EPS) * g


def _gelu(x):
    return 0.5 * x * (1.0 + lax.erf(x * np.float32(np.sqrt(0.5))))


def _rope(x, c, s1, s2):
    half = ROT_DIM // 2
    return x * c + pltpu.roll(x, LANE - half, 1) * s1 + pltpu.roll(x, half, 1) * s2


def _masked_softmax2(s2, m):
    sm = jnp.where(m, s2, NEG)
    mx = jnp.max(sm, axis=-1, keepdims=True)
    e = jnp.where(m, jnp.exp2(sm - mx), 0.0)
    den = jnp.sum(e, axis=-1, keepdims=True)
    return e / jnp.where(den > 0.0, den, 1.0)


def _with_ones(v):
    return jnp.concatenate([v.astype(BF16), jnp.ones(v.shape, BF16)], axis=1)


def _row_tile(m, want):
    return want if m % want == 0 else m


def _resident(shape, index_map):
    return pl.BlockSpec(shape, index_map, pipeline_mode=pl.Buffered(1))


def _ffn_kernel(x_ref, xs_ref, g_ref, wg_ref, wu_ref, wd_ref, o_ref, os_ref, h_ref):
    i, j = pl.program_id(0), pl.program_id(1)
    tm = x_ref.shape[0]

    @pl.when(j == 0)
    def _():
        x = x_ref[...]
        h_ref[:tm] = _rms(x, g_ref[...]).astype(BF16)
        h_ref[tm:] = _rms(xs_ref[...], g_ref[...]).astype(BF16)
        o_ref[...] = x

    @pl.when((i == 0) & (j == 0))
    def _():
        os_ref[...] = xs_ref[...]

    h = h_ref[...]
    a = jnp.dot(h, wg_ref[...].astype(BF16), preferred_element_type=F32)
    b = jnp.dot(h, wu_ref[...].astype(BF16), preferred_element_type=F32)
    t = a * jax.nn.sigmoid(a) * b
    y = 0.5 * _dot(t, wd_ref[...])
    o_ref[...] += y[:tm]

    @pl.when(i == 0)
    def _():
        os_ref[...] += y[tm:]


def _ffn(x, xs, g, wg, wu, wd, l):
    m, d = x.shape
    f = wg.shape[2]
    tm = _row_tile(m, ROW_TILE)
    tf = 512
    return pl.pallas_call(
        _ffn_kernel,
        grid=(m // tm, f // tf),
        in_specs=[
            _resident((tm, d), lambda i, j: (i, 0)),
            pl.BlockSpec((SIDE_ROWS, d), lambda i, j: (0, 0)),
            pl.BlockSpec((1, d), lambda i, j: (0, 0)),
            pl.BlockSpec((None, d, tf), lambda i, j: (l, 0, j)),
            pl.BlockSpec((None, d, tf), lambda i, j: (l, 0, j)),
            pl.BlockSpec((None, tf, d), lambda i, j: (l, j, 0)),
        ],
        out_specs=(_resident((tm, d), lambda i, j: (i, 0)), pl.BlockSpec((SIDE_ROWS, d), lambda i, j: (0, 0))),
        out_shape=(jax.ShapeDtypeStruct((m, d), F32), jax.ShapeDtypeStruct((SIDE_ROWS, d), F32)),
        scratch_shapes=[pltpu.VMEM((tm + SIDE_ROWS, d), BF16)],
        compiler_params=_params("arbitrary", "arbitrary"),
        name="ffn",
    )(x, xs, g[l].reshape(1, d), wg, wu, wd)


def _rms_mm_kernel(*refs, tn, side):
    if side:
        x_ref, xs_ref, g_ref, w_ref, o_ref, os_ref, h_ref = refs
    else:
        x_ref, g_ref, w_ref, o_ref, h_ref = refs
    j = pl.program_id(1)
    tm = x_ref.shape[0]

    @pl.when(j == 0)
    def _():
        h_ref[:tm] = _rms(x_ref[...], g_ref[...]).astype(BF16)
        if side:
            h_ref[tm:] = _rms(xs_ref[...], g_ref[...]).astype(BF16)

    y = jnp.dot(h_ref[...], w_ref[:, pl.ds(pl.multiple_of(j * tn, tn), tn)], preferred_element_type=F32)
    o_ref[...] = y[:tm]
    if side:
        os_ref[...] = y[tm:]


def _whole_weight_spec(w, l):
    if w.ndim == 2:
        return _resident(w.shape, lambda i, j: (0, 0))
    return _resident((None,) + w.shape[1:], lambda i, j: (l, 0, 0))


def _rms_mm(x, xs, g, w, l=None):
    m, d = x.shape
    n = w.shape[-1]
    tm = _row_tile(m, ROW_TILE)
    tn = _row_tile(n, 512)
    side = xs is not None
    extra = SIDE_ROWS if side else 0
    x_spec = pl.BlockSpec((tm, d), lambda i, j: (i, 0))
    g_spec = pl.BlockSpec((1, d), lambda i, j: (0, 0))
    side_in = [pl.BlockSpec((SIDE_ROWS, d), lambda i, j: (0, 0))] if side else []
    out, ospec = jax.ShapeDtypeStruct((m, n), F32), pl.BlockSpec((tm, tn), lambda i, j: (i, j))
    if side:
        out = (out, jax.ShapeDtypeStruct((SIDE_ROWS, n), F32))
        ospec = (ospec, pl.BlockSpec((SIDE_ROWS, tn), lambda i, j: (0, j)))
    return pl.pallas_call(
        partial(_rms_mm_kernel, tn=tn, side=side),
        grid=(m // tm, n // tn),
        in_specs=[x_spec] + side_in + [g_spec, _whole_weight_spec(w, l)],
        out_specs=ospec,
        out_shape=out,
        scratch_shapes=[pltpu.VMEM((tm + extra, d), BF16)],
        compiler_params=_params("arbitrary", "arbitrary"),
        name="rms_mm",
    )(x, *([xs] if side else []), g.reshape(1, d), w)


def _out_proj_kernel(a1_ref, a2_ref, s1_ref, s2_ref, w_ref, x_ref, xs_ref, o_ref, os_ref, l1_scr, l2_scr, *, tn):
    j = pl.program_id(1)
    tm, k1 = a1_ref.shape

    @pl.when(j == 0)
    def _():
        l1_scr[:tm] = a1_ref[...]
        l1_scr[tm:] = s1_ref[...]
        l2_scr[:tm] = a2_ref[...]
        l2_scr[tm:] = s2_ref[...]

    cols = pl.ds(pl.multiple_of(j * tn, tn), tn)
    y = (jnp.dot(l1_scr[...], w_ref[:k1, cols], preferred_element_type=F32)
         + jnp.dot(l2_scr[...], w_ref[k1:, cols], preferred_element_type=F32))
    o_ref[...] = x_ref[...] + y[:tm]
    os_ref[...] = xs_ref[...] + y[tm:]


def _out_proj(a1, a2, s1, s2, w, l, x, xs):
    m, k1 = a1.shape
    k2 = a2.shape[1]
    d = w.shape[-1]
    tm = _row_tile(m, ROW_TILE)
    tn = 512
    tile = pl.BlockSpec((tm, tn), lambda i, j: (i, j))
    side_tile = pl.BlockSpec((SIDE_ROWS, tn), lambda i, j: (0, j))
    return pl.pallas_call(
        partial(_out_proj_kernel, tn=tn),
        grid=(m // tm, d // tn),
        in_specs=[
            pl.BlockSpec((tm, k1), lambda i, j: (i, 0)),
            pl.BlockSpec((tm, k2), lambda i, j: (i, 0)),
            pl.BlockSpec((SIDE_ROWS, k1), lambda i, j: (0, 0)),
            pl.BlockSpec((SIDE_ROWS, k2), lambda i, j: (0, 0)),
            _whole_weight_spec(w, l),
            tile, side_tile,
        ],
        out_specs=(tile, side_tile),
        out_shape=(jax.ShapeDtypeStruct((m, d), F32), jax.ShapeDtypeStruct((SIDE_ROWS, d), F32)),
        scratch_shapes=[pltpu.VMEM((tm + SIDE_ROWS, k1), BF16), pltpu.VMEM((tm + SIDE_ROWS, k2), BF16)],
        compiler_params=_params("arbitrary", "arbitrary"),
        name="out_proj",
    )(a1, a2, s1, s2, w, x, xs)


def _mem_kv_post_kernel(kv_ref, g_ref, k_ref, v_ref):
    g = g_ref[...]
    for h in range(MEM_HEADS):
        k_ref[:, h, :] = _rms(kv_ref[:, h * HEAD_DIM:(h + 1) * HEAD_DIM], g)
        v_ref[:, h, :] = kv_ref[:, MEM_W + h * HEAD_DIM:MEM_W + (h + 1) * HEAD_DIM]


def _mem_kv_post(kv, g_mk):
    m = kv.shape[0]
    out = jax.ShapeDtypeStruct((m, MEM_HEADS, HEAD_DIM), F32)
    return pl.pallas_call(
        _mem_kv_post_kernel,
        out_shape=(out, out),
        compiler_params=pltpu.CompilerParams(vmem_limit_bytes=VMEM_LIMIT),
        name="mem_kv_post",
    )(kv, g_mk.reshape(1, HEAD_DIM))


def _mem_attend_kernel(q_ref, g_ref, k_ref, v_ref, o_ref, *, rows):
    g = g_ref[...]
    for h in range(MEM_HEADS):
        sl = slice(h * HEAD_DIM, (h + 1) * HEAD_DIM)
        q = q_ref[0, :, sl]
        if rows < SUBLANE:
            q = jnp.broadcast_to(q[0:1], (SUBLANE, HEAD_DIM))
        q = _rms(q, g) * SCALE_LOG2E
        s2 = _dot_nt(q, k_ref[0, :, h, :])
        e = jnp.exp2(s2 - jnp.max(s2, axis=-1, keepdims=True))
        p = e / jnp.sum(e, axis=-1, keepdims=True)
        o = _dot(p, v_ref[0, :, h, :])
        o_ref[0, :, sl] = o[:rows].astype(o_ref.dtype)


def _mem_attend(z, col_block, mk, mv, g_mq):
    b, t, _ = z.shape
    tm = _row_tile(t, 512)
    kv_spec = pl.BlockSpec((1, MEM_LEN, MEM_HEADS, HEAD_DIM), lambda i, j: (i, 0, 0, 0))
    return pl.pallas_call(
        partial(_mem_attend_kernel, rows=tm),
        grid=(b, t // tm),
        in_specs=[
            pl.BlockSpec((1, tm, MEM_W), lambda i, j: (i, j, col_block)),
            pl.BlockSpec((1, HEAD_DIM), lambda i, j: (0, 0)),
            kv_spec, kv_spec,
        ],
        out_specs=pl.BlockSpec((1, tm, MEM_W), lambda i, j: (i, j, 0)),
        out_shape=jax.ShapeDtypeStruct((b, t, MEM_W), BF16),
        compiler_params=_params("parallel", "arbitrary"),
        name="mem_attend",
    )(z, g_mq.reshape(1, HEAD_DIM), mk, mv)


def _layer_norm(x, g, b):
    xc = x - jnp.mean(x, axis=-1, keepdims=True)
    var = jnp.mean(xc * xc, axis=-1, keepdims=True)
    return xc * lax.rsqrt(var + EPS) * g + b


def _gmlp_prompt_kernel(z_ref, lg_ref, lb_ref, ws_ref, bs_ref, o_ref):
    v = _layer_norm(_gelu(z_ref[:, GM_W:]), lg_ref[...], lb_ref[...])
    row = lax.broadcasted_iota(jnp.int32, (CHUNK, CHUNK), 0)
    col = lax.broadcasted_iota(jnp.int32, (CHUNK, CHUNK), 1)
    causal = col <= row
    for g in range(GM_GROUPS):
        sl = slice(g * HEAD_DIM, (g + 1) * HEAD_DIM)
        w = jnp.where(causal, ws_ref[g], 0.0)
        sv = _dot(w, v[:, sl]) + bs_ref[:, g:g + 1]
        o_ref[:, sl] = (_gelu(z_ref[:, sl]) * sv).astype(o_ref.dtype)


def _gmlp_prompt(z, ln_g, ln_b, w_s, b_s):
    m = z.shape[0]
    return pl.pallas_call(
        _gmlp_prompt_kernel,
        grid=(m // CHUNK,),
        in_specs=[
            pl.BlockSpec((CHUNK, 2 * GM_W), lambda i: (i, 0)),
            pl.BlockSpec((1, GM_W), lambda i: (0, 0)),
            pl.BlockSpec((1, GM_W), lambda i: (0, 0)),
            pl.BlockSpec((GM_GROUPS, CHUNK, CHUNK), lambda i: (0, 0, 0)),
            pl.BlockSpec((CHUNK, GM_GROUPS), lambda i: (0, 0)),
        ],
        out_specs=pl.BlockSpec((CHUNK, GM_W), lambda i: (i, 0)),
        out_shape=jax.ShapeDtypeStruct((m, GM_W), BF16),
        compiler_params=_params("parallel"),
        name="gmlp_prompt",
    )(z, ln_g.reshape(1, GM_W), ln_b.reshape(1, GM_W), w_s, b_s.T)


def _gmlp_first_row_kernel(z_ref, lg_ref, lb_ref, w0_ref, b0_ref, o_ref, v_ref):
    v = _layer_norm(_gelu(z_ref[:, GM_W:2 * GM_W]), lg_ref[...], lb_ref[...])
    v_ref[...] = v
    o_ref[...] = (_gelu(z_ref[:, :GM_W]) * (v * w0_ref[...] + b0_ref[...])).astype(o_ref.dtype)


def _gmlp_first_row(z, ln_g, ln_b, w_s, b_s):
    m = z.shape[0]
    w0 = jnp.repeat(w_s[:, 0, 0], HEAD_DIM).reshape(1, GM_W)
    b0 = jnp.repeat(b_s[:, 0], HEAD_DIM).reshape(1, GM_W)
    return pl.pallas_call(
        _gmlp_first_row_kernel,
        out_shape=(jax.ShapeDtypeStruct((m, GM_W), BF16), jax.ShapeDtypeStruct((m, GM_W), F32)),
        compiler_params=pltpu.CompilerParams(vmem_limit_bytes=VMEM_LIMIT),
        name="gmlp_first_row",
    )(z, ln_g.reshape(1, GM_W), ln_b.reshape(1, GM_W), w0, b0)


def _rope_tables(pos):
    half = ROT_DIM // 2
    inv = ROPE_THETA ** (-jnp.arange(half, dtype=F32) / half)
    ang = pos.astype(F32)[:, None] * inv[None, :]
    cos, sin = jnp.cos(ang), jnp.sin(ang)
    t = pos.shape[0]
    one = jnp.ones((t, HEAD_DIM - ROT_DIM), F32)
    zero = jnp.zeros((t, HEAD_DIM - ROT_DIM), F32)
    zh = jnp.zeros((t, half), F32)
    c = jnp.concatenate([cos, cos, one], axis=1)
    s1 = jnp.concatenate([-sin, zh, zero], axis=1)
    s2 = jnp.concatenate([zh, sin, zero], axis=1)
    return jnp.stack([c, s1, s2])


def _nsa_kv_post_kernel(h_ref, rope_ref, gs_ref, gw_ref, kc_ref, vc_ref, ks_ref, vs_ref, kw_ref, vw_ref,
                        *mxu_refs):
    c, s1, s2 = rope_ref[0], rope_ref[1], rope_ref[2]

    def part(i, g):
        return h_ref[:, i * KV_W + g * HEAD_DIM:i * KV_W + (g + 1) * HEAD_DIM]

    for g in range(NSA_KV):
        sl = slice(g * HEAD_DIM, (g + 1) * HEAD_DIM)
        kc_ref[:, g, :] = part(0, g)
        vc_ref[:, g, :] = part(1, g)
        ks = _rope(_rms(part(2, g), gs_ref[...]), c, s1, s2)
        kw = _rope(_rms(part(4, g), gw_ref[...]), c, s1, s2)
        ks_ref[:, g, :] = ks
        vs_ref[:, g, :] = part(3, g)
        kw_ref[:, g, :] = kw
        vw_ref[:, g, :] = part(5, g)
        if mxu_refs:
            ks16_ref, vst16_ref, kw16_ref, vwt16_ref = mxu_refs
            ks16_ref[:, sl] = ks.astype(BF16)
            kw16_ref[:, sl] = kw.astype(BF16)
            vst16_ref[sl, :] = part(3, g).T.astype(BF16)
            vwt16_ref[sl, :] = part(5, g).T.astype(BF16)


def _nsa_kv_post(h, rope_tab, g_ks, g_kw, mxu_copies):
    m = h.shape[0]
    tm = _row_tile(m, 512)
    out = jax.ShapeDtypeStruct((m, NSA_KV, HEAD_DIM), F32)
    ospec = pl.BlockSpec((tm, NSA_KV, HEAD_DIM), lambda i: (i, 0, 0))
    outs, ospecs = (out,) * 6, (ospec,) * 6
    if mxu_copies:
        k16, v16 = jax.ShapeDtypeStruct((m, KV_W), BF16), jax.ShapeDtypeStruct((KV_W, m), BF16)
        kspec, vspec = pl.BlockSpec((tm, KV_W), lambda i: (i, 0)), pl.BlockSpec((KV_W, tm), lambda i: (0, i))
        outs, ospecs = outs + (k16, v16, k16, v16), ospecs + (kspec, vspec, kspec, vspec)
    return pl.pallas_call(
        _nsa_kv_post_kernel,
        grid=(m // tm,),
        in_specs=[
            pl.BlockSpec((tm, 6 * KV_W), lambda i: (i, 0)),
            pl.BlockSpec((3, tm, HEAD_DIM), lambda i: (0, i, 0)),
            pl.BlockSpec((1, HEAD_DIM), lambda i: (0, 0)),
            pl.BlockSpec((1, HEAD_DIM), lambda i: (0, 0)),
        ],
        out_specs=ospecs,
        out_shape=outs,
        compiler_params=_params("parallel"),
        name="nsa_kv_post",
    )(h, rope_tab, g_ks.reshape(1, HEAD_DIM), g_kw.reshape(1, HEAD_DIM))


def _compress_kernel(pt_ref, *refs, n_pages):
    n_in = 2 * PAGES_PER_STEP
    page_refs = refs[:n_in]
    w1_refs, pe_refs, w2_refs = refs[n_in:n_in + 2], refs[n_in + 2:n_in + 4], refs[n_in + 4:n_in + 6]
    gk_ref = refs[n_in + 6]
    o_refs = refs[n_in + 7:n_in + 9]
    l_scr, a_scr, b_scr = refs[n_in + 9:]
    j = pl.program_id(1)
    steps_per_group = PAGES_PER_GROUP // PAGES_PER_STEP
    jj = j % steps_per_group
    n_chunks = n_pages * CHUNKS_PER_PAGE
    group_chunks = PAGES_PER_GROUP * CHUNKS_PER_PAGE

    @pl.when(j == 0)
    def _():
        b_scr[:, :, n_chunks:, :] = jnp.zeros((2, NSA_KV, SUBLANE, HEAD_DIM), F32)

    for t in range(2):
        for q in range(PAGES_PER_STEP):
            row0 = pl.multiple_of((jj * PAGES_PER_STEP + q) * CHUNKS_PER_PAGE, CHUNKS_PER_PAGE)
            page = page_refs[t * PAGES_PER_STEP + q]
            for p in range(CMP_STRIDE):
                for g in range(NSA_KV):
                    l_scr[t, g, pl.ds(row0, CHUNKS_PER_PAGE), p * HEAD_DIM:(p + 1) * HEAD_DIM] = (
                        page[0, pl.ds(p, CHUNKS_PER_PAGE, stride=CMP_STRIDE), g, :])

    @pl.when(jj == steps_per_group - 1)
    def _():
        base = pl.multiple_of((j // steps_per_group) * group_chunks, group_chunks)
        for t in range(2):
            for g in range(NSA_KV):
                r = _dot(l_scr[t, g], w1_refs[t][g])
                a_scr[t, g, pl.ds(base, group_chunks), :] = r[:, :HEAD_DIM]
                b_scr[t, g, pl.ds(base, group_chunks), :] = r[:, HEAD_DIM:]

    @pl.when(j == pl.num_programs(1) - 1)
    def _():
        for t in range(2):
            for g in range(NSA_KV):
                pw = _dot(pe_refs[t][g], w1_refs[t][g])
                bias = pw[0:1, :HEAD_DIM] + pw[1:2, HEAD_DIM:]
                h = a_scr[t, g] + b_scr[t, g, pl.ds(1, n_chunks), :] + bias
                y = _dot(_gelu(h), w2_refs[t][g])
                if t == 0:
                    y = _rms(y, gk_ref[...])
                o_refs[t][0, :, g * HEAD_DIM:(g + 1) * HEAD_DIM] = y


def _compress(pool_k, pool_v, page_table, pos_k, w1_k, w2_k, pos_v, w1_v, w2_v, g_kc):
    nb, n_pages = page_table.shape
    assert n_pages % PAGES_PER_GROUP == 0
    n_chunks = n_pages * CHUNKS_PER_PAGE
    kdim = CMP_STRIDE * HEAD_DIM

    def w1ab(w1):
        return jnp.concatenate([w1[:, :CMP_STRIDE].reshape(NSA_KV, kdim, HEAD_DIM),
                                w1[:, CMP_STRIDE:].reshape(NSA_KV, kdim, HEAD_DIM)], axis=-1).astype(BF16)

    def pe8(pos_enc):
        return jnp.pad(pos_enc.reshape(NSA_KV, 2, kdim), ((0, 0), (0, SUBLANE - 2), (0, 0)))

    def page_spec(q):
        return pl.BlockSpec((1, PAGE_SIZE, NSA_KV, HEAD_DIM),
                            lambda b, j, pt: (pt[b, j * PAGES_PER_STEP + q], 0, 0, 0))

    const3 = lambda b, j, pt: (0, 0, 0)
    page_specs = [page_spec(q) for q in range(PAGES_PER_STEP)]
    w1_spec = _resident((NSA_KV, kdim, 2 * HEAD_DIM), const3)
    pe_spec = _resident((NSA_KV, SUBLANE, kdim), const3)
    w2_spec = _resident((NSA_KV, HEAD_DIM, HEAD_DIM), const3)
    out = jax.ShapeDtypeStruct((nb, n_chunks, KV_W), F32)
    ospec = pl.BlockSpec((1, n_chunks, KV_W), lambda b, j, pt: (b, 0, 0))
    grid_spec = pltpu.PrefetchScalarGridSpec(
        num_scalar_prefetch=1,
        grid=(nb, n_pages // PAGES_PER_STEP),
        in_specs=page_specs + page_specs + [w1_spec, w1_spec, pe_spec, pe_spec, w2_spec, w2_spec,
                                             pl.BlockSpec((1, HEAD_DIM), lambda b, j, pt: (0, 0))],
        out_specs=(ospec, ospec),
        scratch_shapes=[
            pltpu.VMEM((2, NSA_KV, PAGES_PER_GROUP * CHUNKS_PER_PAGE, kdim), F32),
            pltpu.VMEM((2, NSA_KV, n_chunks, HEAD_DIM), F32),
            pltpu.VMEM((2, NSA_KV, n_chunks + SUBLANE, HEAD_DIM), F32),
        ],
    )
    return pl.pallas_call(
        partial(_compress_kernel, n_pages=n_pages),
        grid_spec=grid_spec,
        out_shape=(out, out),
        compiler_params=_params("parallel", "arbitrary"),
        name="compress",
    )(page_table, *([pool_k] * PAGES_PER_STEP), *([pool_v] * PAGES_PER_STEP),
      w1ab(w1_k), w1ab(w1_v), pe8(pos_k), pe8(pos_v), w2_k, w2_v, g_kc.reshape(1, HEAD_DIM))


def _gate_columns(sig, g, rows):
    lane = lax.broadcasted_iota(jnp.int32, (rows, LANE), 1)
    base = g * (NSA_REP * N_GATES)
    return [[jnp.sum(jnp.where(lane == base + r * N_GATES + t, sig, 0.0), axis=1, keepdims=True)
             for t in range(N_GATES)] for r in range(NSA_REP)]


def _rejected_t(score_t, avail_t, n_rows):
    groups = n_rows // SUBLANE
    rows = [score_t[r * SUBLANE:(r + 1) * SUBLANE] for r in range(groups)]
    sub = lax.broadcasted_iota(jnp.int32, rows[0].shape, 0)
    cnt = [jnp.zeros(rows[0].shape, F32) for _ in range(groups)]
    for i in range(n_rows):
        si = score_t[i:i + 1, :]
        gi, oi = divmod(i, SUBLANE)
        for r in range(groups):
            if r > gi:
                beats = si >= rows[r]
            elif r < gi:
                beats = si > rows[r]
            else:
                beats = (si > rows[r]) | ((si == rows[r]) & (sub > oi))
            cnt[r] = cnt[r] + jnp.where(beats, 1.0, 0.0)
    cnt = jnp.concatenate(cnt, axis=0)
    return jnp.where((cnt < N_SELECT) & avail_t, 0.0, 1.0)


def _nsa_prompt_kernel(q_ref, gate_ref, gq_ref, rope_ref, kc_ref, vc_ref, ks_ref, vst_ref, kw_ref, vwt_ref,
                       o_ref, m_scr, acc_scr, *, n_sel, n_cmp):
    g = pl.program_id(1)
    qb = pl.program_id(2)
    t0 = qb * QB
    rq = NSA_REP * QB
    c, s1, s2 = rope_ref[0], rope_ref[1], rope_ref[2]
    gq = gq_ref[...]
    qn_l, qr_l = [], []
    for r in range(NSA_REP):
        qn_r = _rms(q_ref[0, :, r * HEAD_DIM:(r + 1) * HEAD_DIM], gq)
        qn_l.append(qn_r * SCALE_LOG2E)
        qr_l.append(_rope(qn_r, c, s1, s2) * SCALE_LOG2E)
    qn = jnp.concatenate(qn_l, axis=0).astype(BF16)
    qr = jnp.concatenate(qr_l, axis=0).astype(BF16)
    pos1 = t0 + lax.broadcasted_iota(jnp.int32, (QB, 1), 0)
    pos3 = jnp.concatenate([pos1] * NSA_REP, axis=0)

    ci = lax.broadcasted_iota(jnp.int32, (1, n_cmp), 1)
    p = _masked_softmax2(_dot_nt(qn, kc_ref[0]), ci * CMP_STRIDE + (CMP_LEN - 1) <= pos3)
    o_c = _dot(p, vc_ref[0])
    psum = p[0:QB] + p[QB:2 * QB] + p[2 * QB:3 * QB]

    ci_l = lax.broadcasted_iota(jnp.int32, (n_sel, n_cmp), 1) * CMP_STRIDE
    sj = lax.broadcasted_iota(jnp.int32, (n_sel, n_cmp), 0) * SEL_BLOCK
    cover_t = jnp.where((ci_l < sj + SEL_BLOCK) & (ci_l + CMP_LEN > sj), 1.0, 0.0)
    imp_t = lax.dot_general(cover_t, psum, (((1,), (1,)), ((), ())),
                            precision=lax.Precision.HIGHEST, preferred_element_type=F32)
    post = t0 + lax.broadcasted_iota(jnp.int32, (n_sel, QB), 1)
    jt = lax.broadcasted_iota(jnp.int32, (n_sel, QB), 0)
    cur = post >> SEL_SHIFT
    avail_t = jt * SEL_BLOCK <= post
    forced_t = (jt == 0) | (jt == cur) | (jt == cur - 1)
    score_t = jnp.where(avail_t, jnp.where(forced_t, FORCED_SCORE, imp_t), NEG)
    rej_t = _rejected_t(score_t, avail_t, n_sel)
    rej = jnp.concatenate([rej_t, jnp.zeros((LANE - n_sel, QB), F32)], axis=0).T.astype(BF16)

    pos_q = t0 + lax.broadcasted_iota(jnp.int32, (1, QB), 1)
    per_head = lambda a: jnp.concatenate([a] * NSA_REP, axis=1)
    q_cat = jnp.concatenate([qr, jnp.concatenate([rej] * NSA_REP, axis=0)], axis=1)

    def values_t(vt):
        return jnp.concatenate([vt, jnp.ones(vt.shape, BF16)], axis=0)

    def normalised(acc_t):
        o_t = acc_t[:HEAD_DIM] / acc_t[HEAD_DIM:HEAD_DIM + 1]
        return [o_t[:, r * QB:(r + 1) * QB].T for r in range(NSA_REP)]

    m_scr[...] = jnp.full((1, rq), NEG, F32)
    acc_scr[...] = jnp.zeros((2 * HEAD_DIM, rq), F32)
    blocks_per_chunk = SEL_KEY_CHUNK // SEL_BLOCK

    def sel_step(ck, causal):
        k0 = pl.multiple_of(ck * SEL_KEY_CHUNK, SEL_KEY_CHUNK)
        kr = lax.broadcasted_iota(jnp.int32, (SEL_KEY_CHUNK, LANE), 0)
        jl = lax.broadcasted_iota(jnp.int32, (SEL_KEY_CHUNK, LANE), 1)
        expand_t = jnp.where((kr >> SEL_SHIFT) + ck * blocks_per_chunk == jl, NEG, 0.0).astype(BF16)
        sc = _dot_nt(jnp.concatenate([ks_ref[0, pl.ds(k0, SEL_KEY_CHUNK), :], expand_t], axis=1), q_cat)
        if causal:
            kpos = k0 + lax.broadcasted_iota(jnp.int32, (SEL_KEY_CHUNK, 1), 0)
            sc = sc + per_head(jnp.where(kpos <= pos_q, 0.0, NEG))
        m_old = m_scr[...]
        m_new = jnp.maximum(m_old, jnp.max(sc, axis=0, keepdims=True))
        e = jnp.exp2(sc - m_new).astype(BF16)
        acc_scr[...] = (jnp.exp2(m_old - m_new) * acc_scr[...]
                        + jnp.dot(values_t(vst_ref[:, pl.ds(k0, SEL_KEY_CHUNK)]), e, preferred_element_type=F32))
        m_scr[...] = m_new

    n_chunks = (t0 + QB + SEL_KEY_CHUNK - 1) // SEL_KEY_CHUNK

    def sel_body(ck, carry):
        sel_step(ck, False)
        return carry

    lax.fori_loop(0, n_chunks - 1, sel_body, 0)
    sel_step(n_chunks - 1, True)
    o_s = normalised(acc_scr[...])

    band = WINDOW + QB
    k0 = pl.multiple_of(jnp.maximum(t0 - WINDOW, 0), QB)
    ahead = k0 + lax.broadcasted_iota(jnp.int32, (band, 1), 0) - pos_q
    sw = (_dot_nt(kw_ref[0, pl.ds(k0, band), :], qr)
          + per_head(jnp.where((ahead <= 0) & (ahead >= -WINDOW), 0.0, NEG)))
    ew = jnp.exp2(sw - jnp.max(sw, axis=0, keepdims=True)).astype(BF16)
    o_w = normalised(jnp.dot(values_t(vwt_ref[:, pl.ds(k0, band)]), ew, preferred_element_type=F32))

    gate = _gate_columns(jax.nn.sigmoid(gate_ref[0]), g, QB)
    for r in range(NSA_REP):
        o_ref[0, :, r * HEAD_DIM:(r + 1) * HEAD_DIM] = (
            gate[r][0] * o_c[r * QB:(r + 1) * QB] + gate[r][1] * o_s[r] + gate[r][2] * o_w[r]).astype(o_ref.dtype)


def _nsa_prompt(z, gates, g_q, rope_tab, kc, vc, ks, vst, kw, vwt):
    b, t, _ = z.shape
    n_cmp = kc.shape[1]
    assert t >= WINDOW + QB
    qw = NSA_REP * HEAD_DIM
    k_spec = pl.BlockSpec((1, t, HEAD_DIM), lambda i, g, j: (i, 0, g))
    vt_spec = pl.BlockSpec((HEAD_DIM, t), lambda i, g, j: (g, i))
    cmp_spec = pl.BlockSpec((1, n_cmp, HEAD_DIM), lambda i, g, j: (i, 0, g))
    return pl.pallas_call(
        partial(_nsa_prompt_kernel, n_sel=t // SEL_BLOCK, n_cmp=n_cmp),
        grid=(b, NSA_KV, t // QB),
        in_specs=[
            pl.BlockSpec((1, QB, qw), lambda i, g, j: (i, j, g)),
            pl.BlockSpec((1, QB, LANE), lambda i, g, j: (i, j, 0)),
            pl.BlockSpec((1, HEAD_DIM), lambda i, g, j: (0, 0)),
            pl.BlockSpec((3, QB, HEAD_DIM), lambda i, g, j: (0, j, 0)),
            cmp_spec, cmp_spec, k_spec, vt_spec, k_spec, vt_spec,
        ],
        out_specs=pl.BlockSpec((1, QB, qw), lambda i, g, j: (i, j, g)),
        out_shape=jax.ShapeDtypeStruct((b, t, NSA_W), BF16),
        scratch_shapes=[
            pltpu.VMEM((1, NSA_REP * QB), F32),
            pltpu.VMEM((2 * HEAD_DIM, NSA_REP * QB), F32),
        ],
        compiler_params=_params("parallel", "parallel", "arbitrary"),
        name="nsa_prompt",
    )(z, gates, g_q.reshape(1, HEAD_DIM), rope_tab, kc, vc, ks, vst, kw, vwt)


def _nsa_sample_select_kernel(q_ref, gq_ref, rope_ref, kc_ref, vc_ref, kw_ref, vw_ref,
                              qr_ref, oc_ref, ow_ref, idx_ref, valid_ref, **static):
    for g in range(NSA_KV):
        sl = slice(g * HEAD_DIM, (g + 1) * HEAD_DIM)
        q3 = [q_ref[0, :, (g * NSA_REP + r) * HEAD_DIM:(g * NSA_REP + r + 1) * HEAD_DIM] for r in range(NSA_REP)]
        qr_ref[0, g], oc_ref[0, g], ow_ref[0, g], idx_ref[0, g], valid_ref[0, g] = _select_group(
            q3, gq_ref[...], rope_ref, kc_ref[0, :, sl], vc_ref[0, :, sl], kw_ref[0, :, g, :], vw_ref[0, :, g, :],
            **static)


def _select_group(q3, gq, rope_ref, kc, vc, kw, vw, *, pos, n_sel, n_cmp, n_win):
    n_cand = SEL_CAND_LANES
    c, s1, s2 = rope_ref[0:1], rope_ref[1:2], rope_ref[2:3]
    row = lax.broadcasted_iota(jnp.int32, (SUBLANE, HEAD_DIM), 0)
    q3 = [jnp.broadcast_to(q, (SUBLANE, HEAD_DIM)) for q in q3]
    q8 = jnp.where(row == 0, q3[0], jnp.where(row == 1, q3[1], q3[2]))
    qn = _rms(q8, gq)
    qr = _rope(qn, c, s1, s2) * SCALE_LOG2E

    ci = lax.broadcasted_iota(jnp.int32, (1, n_cmp), 1)
    p = _masked_softmax2(_dot_nt(qn * SCALE_LOG2E, kc), ci * CMP_STRIDE + (CMP_LEN - 1) <= pos)
    o_c = _dot(p, vc)
    psum = p[0:1] + p[1:2] + p[2:3]

    ci_s = lax.broadcasted_iota(jnp.int32, (n_cmp, n_cand), 0) * CMP_STRIDE
    sj = lax.broadcasted_iota(jnp.int32, (n_cmp, n_cand), 1) * SEL_BLOCK
    cover = jnp.where((ci_s < sj + SEL_BLOCK) & (ci_s + CMP_LEN > sj), 1.0, 0.0)
    imp = jnp.dot(jnp.broadcast_to(psum, (SUBLANE, n_cmp)), cover,
                  precision=lax.Precision.HIGHEST, preferred_element_type=F32)[0:1]
    jl = lax.broadcasted_iota(jnp.int32, (1, n_cand), 1)
    cur = pos // SEL_BLOCK
    avail = jl * SEL_BLOCK <= pos
    forced = (jl == 0) | (jl == cur) | (jl == cur - 1)
    score = jnp.where(avail, jnp.where(forced, FORCED_SCORE, imp), NEG)
    score = jnp.where(jl < n_sel, score, EXCLUDED)

    eye = (lax.broadcasted_iota(jnp.int32, (n_cand, n_cand), 0)
           == lax.broadcasted_iota(jnp.int32, (n_cand, n_cand), 1))
    score_b = jnp.broadcast_to(score, (n_cand, n_cand))
    score_c = jnp.sum(jnp.where(eye, score_b, 0.0), axis=1, keepdims=True)
    il = lax.broadcasted_iota(jnp.int32, (n_cand, n_cand), 1)
    jc = lax.broadcasted_iota(jnp.int32, (n_cand, n_cand), 0)
    beats = (score_b > score_c) | ((score_b == score_c) & (il < jc))
    rank = jnp.sum(jnp.where(beats, 1.0, 0.0), axis=1, keepdims=True)
    kl = lax.broadcasted_iota(jnp.int32, (n_cand, LANE), 1).astype(F32)
    hit = rank == kl
    jcol = lax.broadcasted_iota(jnp.int32, (n_cand, LANE), 0)
    idx = jnp.sum(jnp.where(hit, jcol.astype(F32), 0.0), axis=0, keepdims=True).astype(jnp.int32)
    valid = jnp.sum(jnp.where(hit & (jcol * SEL_BLOCK <= pos), 1.0, 0.0),
                    axis=0, keepdims=True).astype(jnp.int32)

    n_buf = kw.shape[0]
    kpos = pos + 1 - n_win + lax.broadcasted_iota(jnp.int32, (1, n_buf), 1)
    pw = _masked_softmax2(_dot_nt(qr, kw), (kpos >= pos - WINDOW) & (kpos <= pos))
    return qr, o_c, _dot(pw, vw), idx, valid


def _nsa_sample_gather_kernel(idx_ref, valid_ref, pt_ref, *refs, pos, n_past):
    n_blk = NSA_KV * N_SELECT
    kp_refs, vp_refs = refs[:n_blk], refs[n_blk:2 * n_blk]
    kn_ref, vn_ref, qr_ref, oc_ref, ow_ref, gate_ref, o_ref = refs[2 * n_blk:]
    b = pl.program_id(0)

    col = lax.broadcasted_iota(jnp.int32, (1, SEL_BLOCK * NSA_KV), 1)
    col_row, col_group = col >> (NSA_KV.bit_length() - 1), col & (NSA_KV - 1)
    sig = jax.nn.sigmoid(gate_ref[0])
    for g in range(NSA_KV):
        scores, masks, values = [], [], []
        for k in range(N_SELECT):
            idx = idx_ref[b, g, k]
            in_past = idx < n_past
            k_blk = jnp.where(in_past, kp_refs[g * N_SELECT + k][0], kn_ref[0])
            values.append(jnp.where(in_past, vp_refs[g * N_SELECT + k][0], vn_ref[0]))
            kpos = idx * SEL_BLOCK + col_row
            msk = (col_group == g) & (kpos <= jnp.where(valid_ref[b, g, k] > 0, pos, -1))
            masks.append(msk)
            scores.append(jnp.where(msk, _dot_nt(qr_ref[0, g], k_blk), NEG))
        mx = scores[0].max(axis=-1, keepdims=True)
        for sc in scores[1:]:
            mx = jnp.maximum(mx, sc.max(axis=-1, keepdims=True))
        den = jnp.zeros((SUBLANE, 1), F32)
        acc = jnp.zeros((SUBLANE, HEAD_DIM), F32)
        for sc, msk, v_blk in zip(scores, masks, values):
            e = jnp.where(msk, jnp.exp2(sc - mx), 0.0)
            den = den + jnp.sum(e, axis=-1, keepdims=True)
            acc = acc + _dot(e, v_blk)
        o_s = acc / den
        gate = _gate_columns(sig, g, 1)
        o_c, o_w = oc_ref[0, g], ow_ref[0, g]
        for r in range(NSA_REP):
            h = g * NSA_REP + r
            o_ref[0, :, h * HEAD_DIM:(h + 1) * HEAD_DIM] = (
                gate[r][0] * o_c[r:r + 1] + gate[r][1] * o_s[r:r + 1] + gate[r][2] * o_w[r:r + 1]
            ).astype(o_ref.dtype)


def _nsa_sample(zq, zgate, g_q, rope_row, kc, vc, kw_buf, vw_buf, n_win, pool_k, pool_v, k_new, v_new,
                page_table, pos):
    db = zq.shape[0]
    n_cmp = kc.shape[1]
    n_buf = kw_buf.shape[1]
    n_past = page_table.shape[1] * (PAGE_SIZE // SEL_BLOCK)
    n_sel = n_past + 1
    assert n_sel <= SEL_CAND_LANES
    qw = NSA_REP * HEAD_DIM
    vec = jax.ShapeDtypeStruct((db, NSA_KV, SUBLANE, HEAD_DIM), F32)
    ivec = jax.ShapeDtypeStruct((db, NSA_KV, 1, LANE), jnp.int32)
    vspec = pl.BlockSpec((1, NSA_KV, SUBLANE, HEAD_DIM), lambda i: (i, 0, 0, 0))
    ispec = pl.BlockSpec((1, NSA_KV, 1, LANE), lambda i: (i, 0, 0, 0))
    cmp_spec = pl.BlockSpec((1, n_cmp, KV_W), lambda i: (i, 0, 0))
    win_spec = pl.BlockSpec((1, n_buf, NSA_KV, HEAD_DIM), lambda i: (i, 0, 0, 0))
    qr, o_c, o_w, idx, valid = pl.pallas_call(
        partial(_nsa_sample_select_kernel, pos=pos, n_sel=n_sel, n_cmp=n_cmp, n_win=n_win),
        grid=(db,),
        in_specs=[
            pl.BlockSpec((1, 1, NSA_W), lambda i: (i, 0, 0)),
            pl.BlockSpec((1, HEAD_DIM), lambda i: (0, 0)),
            pl.BlockSpec((3, HEAD_DIM), lambda i: (0, 0)),
            cmp_spec, cmp_spec, win_spec, win_spec,
        ],
        out_specs=(vspec, vspec, vspec, ispec, ispec),
        out_shape=(vec, vec, vec, ivec, ivec),
        compiler_params=_params("parallel"),
        name="nsa_sample_select",
    )(zq, g_q.reshape(1, HEAD_DIM), rope_row, kc, vc, kw_buf, vw_buf)

    idx = idx[:, :, 0, :N_SELECT]
    valid = valid[:, :, 0, :N_SELECT]
    halves = PAGE_SIZE // SEL_BLOCK

    def pool_spec(g, k):
        def pool_block(i, idx_r, valid_r, pt_r):
            blk = jnp.minimum(idx_r[i, g, k], n_past - 1)
            return (pt_r[i, blk // halves] * halves + blk % halves, 0, 0)
        return pl.BlockSpec((1, SEL_BLOCK * NSA_KV, HEAD_DIM), pool_block)

    n_pool = pool_k.shape[0]
    pool_specs = [pool_spec(g, k) for g in range(NSA_KV) for k in range(N_SELECT)]
    gvec = pl.BlockSpec((1, NSA_KV, SUBLANE, HEAD_DIM), lambda i, *_: (i, 0, 0, 0))
    new_spec = pl.BlockSpec((1, SEL_BLOCK * NSA_KV, HEAD_DIM), lambda i, *_: (i, 0, 0))
    grid_spec = pltpu.PrefetchScalarGridSpec(
        num_scalar_prefetch=3,
        grid=(db,),
        in_specs=pool_specs + pool_specs + [
            new_spec, new_spec, gvec, gvec, gvec,
            pl.BlockSpec((1, 1, LANE), lambda i, *_: (i, 0, 0)),
        ],
        out_specs=pl.BlockSpec((1, 1, NSA_W), lambda i, *_: (i, 0, 0)),
    )
    pool_k2 = pool_k.reshape(n_pool * halves, SEL_BLOCK * NSA_KV, HEAD_DIM)
    pool_v2 = pool_v.reshape(n_pool * halves, SEL_BLOCK * NSA_KV, HEAD_DIM)

    def new_block(a):
        a = jnp.pad(a, ((0, 0), (0, SEL_BLOCK - a.shape[1]), (0, 0), (0, 0)))
        return a.reshape(db, SEL_BLOCK * NSA_KV, HEAD_DIM)

    k_new, v_new = new_block(k_new), new_block(v_new)
    return pl.pallas_call(
        partial(_nsa_sample_gather_kernel, pos=pos, n_past=n_past),
        grid_spec=grid_spec,
        out_shape=jax.ShapeDtypeStruct((db, 1, NSA_W), BF16),
        compiler_params=_params("parallel"),
        name="nsa_sample_gather",
    )(idx, valid, page_table, *([pool_k2] * len(pool_specs)), *([pool_v2] * len(pool_specs)),
      k_new, v_new, qr, o_c, o_w, zgate)


def kernel(x_prompt, x_sample, mem_prompt, cache_mem_k, cache_mem_v, cache_k_cmp, cache_v_cmp, cache_k_sel, cache_v_sel, state_k_win, state_v_win, page_table, g_ffn1, w_ffn1_gate, w_ffn1_up, w_ffn1_down, g_mix, g_ffn2, w_ffn2_gate, w_ffn2_up, w_ffn2_down, g_mem, w_mem_kv, g_mq, g_mk, w_in_a, ln_v_g, ln_v_b, w_spatial, b_spatial, w_out_a, g_kv, w_kv, pos_kc, w1_kc, w2_kc, pos_vc, w1_vc, w2_vc, g_kc, g_ks, g_kw, w_in_b, g_q, w_out_b):
    nb, t, d = x_prompt.shape
    db, ds, _ = x_sample.shape
    depth = g_ffn1.shape[0]
    n_a = w_in_a.shape[0]
    past_len = page_table.shape[1] * PAGE_SIZE
    assert ds == 1 and t % PAGE_SIZE == 0 and db * ds <= SIDE_ROWS
    pos_s = past_len
    ns = db * ds
    side = lambda a: jnp.pad(a, ((0, SIDE_ROWS - ns), (0, 0)))

    xp = x_prompt.reshape(nb * t, d)
    xs = side(x_sample.reshape(ns, d))
    mem = mem_prompt.reshape(nb * MEM_LEN, d)
    w_kv16, w_mem_kv16, w_in_a16 = w_kv.astype(BF16), w_mem_kv.astype(BF16), w_in_a.astype(BF16)
    w_out_a16, w_out_b16 = w_out_a.astype(BF16), w_out_b.astype(BF16)
    mem_k_list, mem_v_list, gm_v_list = [], [], []
    rope_p = _rope_tables(jnp.arange(t, dtype=jnp.int32))
    rope_s = _rope_tables(pos_s + jnp.arange(ds, dtype=jnp.int32))
    rope_p2 = jnp.tile(rope_p, (1, nb, 1))
    kv4 = lambda a, n: a.reshape(n, -1, NSA_KV, HEAD_DIM)

    for l in range(depth):
        if l == n_a:
            hp, hs = _rms_mm(xp, xs, g_kv, w_kv16)
            hs = hs[:ns]
            post_p = _nsa_kv_post(hp, rope_p2, g_ks, g_kw, True)
            kc_p_raw, vc_p_raw, ks_p, vs_p, kw_p, vw_p = [kv4(a, nb) for a in post_p[:6]]
            ks16, vst16, kw16, vwt16 = post_p[6:]
            kv16_p = [ks16.reshape(nb, t, KV_W), vst16, kw16.reshape(nb, t, KV_W), vwt16]
            post_s = _nsa_kv_post(hs, jnp.broadcast_to(rope_s, (3, db, HEAD_DIM)), g_ks, g_kw, False)
            kc_s_raw, vc_s_raw, ks_s, vs_s, kw_s, vw_s = [kv4(a, db) for a in post_s[:6]]
            pages_p = jnp.arange(nb * t // PAGE_SIZE, dtype=jnp.int32).reshape(nb, t // PAGE_SIZE)
            cmp_w = (pos_kc, w1_kc, w2_kc, pos_vc, w1_vc, w2_vc, g_kc)
            kc_p, vc_p = _compress(kc_p_raw.reshape(-1, PAGE_SIZE, NSA_KV, HEAD_DIM),
                                   vc_p_raw.reshape(-1, PAGE_SIZE, NSA_KV, HEAD_DIM), pages_p, *cmp_w)
            kc_s, vc_s = _compress(cache_k_cmp, cache_v_cmp, page_table, *cmp_w)
            wbuf = state_k_win.shape[1]
            n_win = wbuf + ds
            pad = jnp.zeros((db, (-n_win) % SUBLANE, NSA_KV, HEAD_DIM), F32)
            kw_buf = jnp.concatenate([state_k_win, kw_s, pad], axis=1)
            vw_buf = jnp.concatenate([state_v_win, vw_s, pad], axis=1)

        xp, xs = _ffn(xp, xs, g_ffn1, w_ffn1_gate, w_ffn1_up, w_ffn1_down, l)

        mk_p, mv_p = _mem_kv_post(_rms_mm(mem, None, g_mem[l], w_mem_kv16, l), g_mk[l])
        mk_p = mk_p.reshape(nb, MEM_LEN, MEM_HEADS, HEAD_DIM)
        mv_p = mv_p.reshape(nb, MEM_LEN, MEM_HEADS, HEAD_DIM)
        mem_k_list.append(mk_p)
        mem_v_list.append(mv_p)
        mk_s, mv_s = cache_mem_k[l], cache_mem_v[l]

        if l < n_a:
            zp, zs = _rms_mm(xp, xs, g_mix[l], w_in_a16, l)
            zs = zs[:ns]
            mem_blk = 2 * GM_W // MEM_W
            o1_p = _gmlp_prompt(zp, ln_v_g[l], ln_v_b[l], w_spatial[l], b_spatial[l])
            o1_s, v_s = _gmlp_first_row(zs, ln_v_g[l], ln_v_b[l], w_spatial[l], b_spatial[l])
            gm_v_list.append(v_s.reshape(db, ds, GM_W))
            o2_p = _mem_attend(zp.reshape(nb, t, -1), mem_blk, mk_p, mv_p, g_mq[l])
            o2_s = _mem_attend(zs.reshape(db, ds, -1), mem_blk, mk_s, mv_s, g_mq[l])
            w_out, l_out = w_out_a16, l
        else:
            lb = l - n_a
            n_gate = N_GATES * NSA_HEADS
            w_qm = jnp.concatenate([w_in_b[lb][:, :NSA_W], w_in_b[lb][:, NSA_W + n_gate:]], axis=1).astype(BF16)
            w_gate = jnp.pad(w_in_b[lb][:, NSA_W:NSA_W + n_gate], ((0, 0), (0, LANE - n_gate))).astype(BF16)
            zp, zs = _rms_mm(xp, xs, g_mix[l], w_qm)
            gp, gs = _rms_mm(xp, xs, g_mix[l], w_gate)
            zp, zs = zp.reshape(nb, t, -1), zs[:ns].reshape(db, ds, -1)
            gp, gs = gp.reshape(nb, t, LANE), gs[:ns].reshape(db, ds, LANE)
            o1_p = _nsa_prompt(zp, gp, g_q[lb], rope_p, kc_p, vc_p, *kv16_p).reshape(nb * t, NSA_W)
            o1_s = _nsa_sample(zs, gs, g_q[lb], rope_s[:, 0, :], kc_s, vc_s, kw_buf, vw_buf, n_win,
                               cache_k_sel, cache_v_sel, ks_s, vs_s, page_table, pos_s).reshape(db * ds, NSA_W)
            mem_blk = NSA_W // MEM_W
            o2_p = _mem_attend(zp, mem_blk, mk_p, mv_p, g_mq[l])
            o2_s = _mem_attend(zs, mem_blk, mk_s, mv_s, g_mq[l])
            w_out, l_out = w_out_b16, lb
        xp, xs = _out_proj(o1_p, o2_p.reshape(nb * t, MEM_W), side(o1_s), side(o2_s.reshape(ns, MEM_W)),
                           w_out, l_out, xp, xs)

        xp, xs = _ffn(xp, xs, g_ffn2, w_ffn2_gate, w_ffn2_up, w_ffn2_down, l)

    wp = min(WINDOW, t)
    ws = min(WINDOW, past_len + ds)
    return (xp.reshape(nb, t, d), xs[:ns].reshape(db, ds, d), jnp.stack(mem_k_list), jnp.stack(mem_v_list),
            kc_p_raw, vc_p_raw, ks_p, vs_p, kw_p[:, -wp:], vw_p[:, -wp:],
            kc_s_raw, vc_s_raw, ks_s, vs_s, kw_buf[:, n_win - ws:n_win], vw_buf[:, n_win - ws:n_win],
            jnp.stack(gm_v_list))
```

```python
from functools import partial

import numpy as np
import jax
import jax.numpy as jnp
from jax import lax
from jax.experimental import pallas as pl
from jax.experimental.pallas import tpu as pltpu

D_MODEL = 2048
HEAD_DIM = 128
ROT_DIM = HEAD_DIM // 4
ROPE_THETA = 500000.0
MEM_LEN = 256
MEM_HEADS = 4
MEM_W = MEM_HEADS * HEAD_DIM
GM_GROUPS = 12
GM_W = GM_GROUPS * HEAD_DIM
CHUNK = 128
NSA_HEADS = 12
NSA_KV = 4
NSA_REP = NSA_HEADS // NSA_KV
NSA_W = NSA_HEADS * HEAD_DIM
KV_W = NSA_KV * HEAD_DIM
N_GATES = 3
CMP_LEN = 32
CMP_STRIDE = 16
SEL_BLOCK = 64
N_SELECT = 16
WINDOW = 512
QB = 512
PAGE_SIZE = 128
EPS = 1e-6
NEG = -1e30
EXCLUDED = -3e38
FORCED_SCORE = 1e6
SCALE = HEAD_DIM ** -0.5
SCALE_LOG2E = float(SCALE * np.log2(np.e))

LANE = 128
SUBLANE = 8
VMEM_LIMIT = 56 * 1024 * 1024
ROW_TILE = 1024
SIDE_ROWS = 16
PAGES_PER_GROUP = 16
PAGES_PER_STEP = 4
CHUNKS_PER_PAGE = PAGE_SIZE // CMP_STRIDE
SEL_KEY_CHUNK = 512
SEL_CAND_LANES = 3 * LANE
SEL_SHIFT = SEL_BLOCK.bit_length() - 1

BF16 = jnp.bfloat16
F32 = jnp.float32


def _params(*sem):
    return pltpu.CompilerParams(dimension_semantics=sem, vmem_limit_bytes=VMEM_LIMIT)


def _dot(a, b):
    return jnp.dot(a.astype(BF16), b.astype(BF16), preferred_element_type=F32)


def _dot_nt(a, b):
    return lax.dot_general(a.astype(BF16), b.astype(BF16), (((1,), (1,)), ((), ())),
                           preferred_element_type=F32)


def _rms(x, g):
    return x * lax.rsqrt(jnp.mean(x * x, axis=-1, keepdims=True) + EPS) * g


def _gelu(x):
    return 0.5 * x * (1.0 + lax.erf(x * np.float32(np.sqrt(0.5))))


def _rope(x, c, s1, s2):
    half = ROT_DIM // 2
    return x * c + pltpu.roll(x, LANE - half, 1) * s1 + pltpu.roll(x, half, 1) * s2


def _masked_softmax2(s2, m):
    sm = jnp.where(m, s2, NEG)
    mx = jnp.max(sm, axis=-1, keepdims=True)
    e = jnp.where(m, jnp.exp2(sm - mx), 0.0)
    den = jnp.sum(e, axis=-1, keepdims=True)
    return e / jnp.where(den > 0.0, den, 1.0)


def _with_ones(v):
    return jnp.concatenate([v.astype(BF16), jnp.ones(v.shape, BF16)], axis=1)


def _row_tile(m, want):
    return want if m % want == 0 else m


def _resident(shape, index_map):
    return pl.BlockSpec(shape, index_map, pipeline_mode=pl.Buffered(1))


def _ffn_kernel(x_ref, xs_ref, g_ref, wg_ref, wu_ref, wd_ref, o_ref, os_ref, h_ref):
    i, j = pl.program_id(0), pl.program_id(1)
    tm = x_ref.shape[0]

    @pl.when(j == 0)
    def _():
        x = x_ref[...]
        h_ref[:tm] = _rms(x, g_ref[...]).astype(BF16)
        h_ref[tm:] = _rms(xs_ref[...], g_ref[...]).astype(BF16)
        o_ref[...] = x

    @pl.when((i == 0) & (j == 0))
    def _():
        os_ref[...] = xs_ref[...]

    h = h_ref[...]
    a = jnp.dot(h, wg_ref[...].astype(BF16), preferred_element_type=F32)
    b = jnp.dot(h, wu_ref[...].astype(BF16), preferred_element_type=F32)
    t = a * jax.nn.sigmoid(a) * b
    y = 0.5 * _dot(t, wd_ref[...])
    o_ref[...] += y[:tm]

    @pl.when(i == 0)
    def _():
        os_ref[...] += y[tm:]


def _ffn(x, xs, g, wg, wu, wd, l):
    m, d = x.shape
    f = wg.shape[2]
    tm = _row_tile(m, ROW_TILE)
    tf = 512
    return pl.pallas_call(
        _ffn_kernel,
        grid=(m // tm, f // tf),
        in_specs=[
            _resident((tm, d), lambda i, j: (i, 0)),
            pl.BlockSpec((SIDE_ROWS, d), lambda i, j: (0, 0)),
            pl.BlockSpec((1, d), lambda i, j: (0, 0)),
            pl.BlockSpec((None, d, tf), lambda i, j: (l, 0, j)),
            pl.BlockSpec((None, d, tf), lambda i, j: (l, 0, j)),
            pl.BlockSpec((None, tf, d), lambda i, j: (l, j, 0)),
        ],
        out_specs=(_resident((tm, d), lambda i, j: (i, 0)), pl.BlockSpec((SIDE_ROWS, d), lambda i, j: (0, 0))),
        out_shape=(jax.ShapeDtypeStruct((m, d), F32), jax.ShapeDtypeStruct((SIDE_ROWS, d), F32)),
        scratch_shapes=[pltpu.VMEM((tm + SIDE_ROWS, d), BF16)],
        compiler_params=_params("arbitrary", "arbitrary"),
        name="ffn",
    )(x, xs, g[l].reshape(1, d), wg, wu, wd)


def _rms_mm_kernel(*refs, tn, side):
    if side:
        x_ref, xs_ref, g_ref, w_ref, o_ref, os_ref, h_ref = refs
    else:
        x_ref, g_ref, w_ref, o_ref, h_ref = refs
    j = pl.program_id(1)
    tm = x_ref.shape[0]

    @pl.when(j == 0)
    def _():
        h_ref[:tm] = _rms(x_ref[...], g_ref[...]).astype(BF16)
        if side:
            h_ref[tm:] = _rms(xs_ref[...], g_ref[...]).astype(BF16)

    y = jnp.dot(h_ref[...], w_ref[:, pl.ds(pl.multiple_of(j * tn, tn), tn)], preferred_element_type=F32)
    o_ref[...] = y[:tm]
    if side:
        os_ref[...] = y[tm:]


def _whole_weight_spec(w, l):
    if w.ndim == 2:
        return _resident(w.shape, lambda i, j: (0, 0))
    return _resident((None,) + w.shape[1:], lambda i, j: (l, 0, 0))


def _rms_mm(x, xs, g, w, l=None):
    m, d = x.shape
    n = w.shape[-1]
    tm = _row_tile(m, ROW_TILE)
    tn = _row_tile(n, 512)
    side = xs is not None
    extra = SIDE_ROWS if side else 0
    x_spec = pl.BlockSpec((tm, d), lambda i, j: (i, 0))
    g_spec = pl.BlockSpec((1, d), lambda i, j: (0, 0))
    side_in = [pl.BlockSpec((SIDE_ROWS, d), lambda i, j: (0, 0))] if side else []
    out, ospec = jax.ShapeDtypeStruct((m, n), F32), pl.BlockSpec((tm, tn), lambda i, j: (i, j))
    if side:
        out = (out, jax.ShapeDtypeStruct((SIDE_ROWS, n), F32))
        ospec = (ospec, pl.BlockSpec((SIDE_ROWS, tn), lambda i, j: (0, j)))
    return pl.pallas_call(
        partial(_rms_mm_kernel, tn=tn, side=side),
        grid=(m // tm, n // tn),
        in_specs=[x_spec] + side_in + [g_spec, _whole_weight_spec(w, l)],
        out_specs=ospec,
        out_shape=out,
        scratch_shapes=[pltpu.VMEM((tm + extra, d), BF16)],
        compiler_params=_params("arbitrary", "arbitrary"),
        name="rms_mm",
    )(x, *([xs] if side else []), g.reshape(1, d), w)


def _out_proj_kernel(a1_ref, a2_ref, s1_ref, s2_ref, w_ref, x_ref, xs_ref, o_ref, os_ref, l1_scr, l2_scr, *, tn):
    j = pl.program_id(1)
    tm, k1 = a1_ref.shape

    @pl.when(j == 0)
    def _():
        l1_scr[:tm] = a1_ref[...]
        l1_scr[tm:] = s1_ref[...]
        l2_scr[:tm] = a2_ref[...]
        l2_scr[tm:] = s2_ref[...]

    cols = pl.ds(pl.multiple_of(j * tn, tn), tn)
    y = (jnp.dot(l1_scr[...], w_ref[:k1, cols], preferred_element_type=F32)
         + jnp.dot(l2_scr[...], w_ref[k1:, cols], preferred_element_type=F32))
    o_ref[...] = x_ref[...] + y[:tm]
    os_ref[...] = xs_ref[...] + y[tm:]


def _out_proj(a1, a2, s1, s2, w, l, x, xs):
    m, k1 = a1.shape
    k2 = a2.shape[1]
    d = w.shape[-1]
    tm = _row_tile(m, ROW_TILE)
    tn = 512
    tile = pl.BlockSpec((tm, tn), lambda i, j: (i, j))
    side_tile = pl.BlockSpec((SIDE_ROWS, tn), lambda i, j: (0, j))
    return pl.pallas_call(
        partial(_out_proj_kernel, tn=tn),
        grid=(m // tm, d // tn),
        in_specs=[
            pl.BlockSpec((tm, k1), lambda i, j: (i, 0)),
            pl.BlockSpec((tm, k2), lambda i, j: (i, 0)),
            pl.BlockSpec((SIDE_ROWS, k1), lambda i, j: (0, 0)),
            pl.BlockSpec((SIDE_ROWS, k2), lambda i, j: (0, 0)),
            _whole_weight_spec(w, l),
            tile, side_tile,
        ],
        out_specs=(tile, side_tile),
        out_shape=(jax.ShapeDtypeStruct((m, d), F32), jax.ShapeDtypeStruct((SIDE_ROWS, d), F32)),
        scratch_shapes=[pltpu.VMEM((tm + SIDE_ROWS, k1), BF16), pltpu.VMEM((tm + SIDE_ROWS, k2), BF16)],
        compiler_params=_params("arbitrary", "arbitrary"),
        name="out_proj",
    )(a1, a2, s1, s2, w, x, xs)


def _mem_kv_post_kernel(kv_ref, g_ref, k_ref, v_ref):
    g = g_ref[...]
    for h in range(MEM_HEADS):
        k_ref[:, h, :] = _rms(kv_ref[:, h * HEAD_DIM:(h + 1) * HEAD_DIM], g)
        v_ref[:, h, :] = kv_ref[:, MEM_W + h * HEAD_DIM:MEM_W + (h + 1) * HEAD_DIM]


def _mem_kv_post(kv, g_mk):
    m = kv.shape[0]
    out = jax.ShapeDtypeStruct((m, MEM_HEADS, HEAD_DIM), F32)
    return pl.pallas_call(
        _mem_kv_post_kernel,
        out_shape=(out, out),
        compiler_params=pltpu.CompilerParams(vmem_limit_bytes=VMEM_LIMIT),
        name="mem_kv_post",
    )(kv, g_mk.reshape(1, HEAD_DIM))


def _mem_attend_kernel(q_ref, g_ref, k_ref, v_ref, o_ref, *, rows):
    g = g_ref[...]
    for h in range(MEM_HEADS):
        sl = slice(h * HEAD_DIM, (h + 1) * HEAD_DIM)
        q = q_ref[0, :, sl]
        if rows < SUBLANE:
            q = jnp.broadcast_to(q[0:1], (SUBLANE, HEAD_DIM))
        q = _rms(q, g) * SCALE_LOG2E
        s2 = _dot_nt(q, k_ref[0, :, h, :])
        e = jnp.exp2(s2 - jnp.max(s2, axis=-1, keepdims=True))
        p = e / jnp.sum(e, axis=-1, keepdims=True)
        o = _dot(p, v_ref[0, :, h, :])
        o_ref[0, :, sl] = o[:rows].astype(o_ref.dtype)


def _mem_attend(z, col_block, mk, mv, g_mq):
    b, t, _ = z.shape
    tm = _row_tile(t, 512)
    kv_spec = pl.BlockSpec((1, MEM_LEN, MEM_HEADS, HEAD_DIM), lambda i, j: (i, 0, 0, 0))
    return pl.pallas_call(
        partial(_mem_attend_kernel, rows=tm),
        grid=(b, t // tm),
        in_specs=[
            pl.BlockSpec((1, tm, MEM_W), lambda i, j: (i, j, col_block)),
            pl.BlockSpec((1, HEAD_DIM), lambda i, j: (0, 0)),
            kv_spec, kv_spec,
        ],
        out_specs=pl.BlockSpec((1, tm, MEM_W), lambda i, j: (i, j, 0)),
        out_shape=jax.ShapeDtypeStruct((b, t, MEM_W), BF16),
        compiler_params=_params("parallel", "arbitrary"),
        name="mem_attend",
    )(z, g_mq.reshape(1, HEAD_DIM), mk, mv)


def _layer_norm(x, g, b):
    xc = x - jnp.mean(x, axis=-1, keepdims=True)
    var = jnp.mean(xc * xc, axis=-1, keepdims=True)
    return xc * lax.rsqrt(var + EPS) * g + b


def _gmlp_prompt_kernel(z_ref, lg_ref, lb_ref, ws_ref, bs_ref, o_ref):
    v = _layer_norm(_gelu(z_ref[:, GM_W:]), lg_ref[...], lb_ref[...])
    row = lax.broadcasted_iota(jnp.int32, (CHUNK, CHUNK), 0)
    col = lax.broadcasted_iota(jnp.int32, (CHUNK, CHUNK), 1)
    causal = col <= row
    for g in range(GM_GROUPS):
        sl = slice(g * HEAD_DIM, (g + 1) * HEAD_DIM)
        w = jnp.where(causal, ws_ref[g], 0.0)
        sv = _dot(w, v[:, sl]) + bs_ref[:, g:g + 1]
        o_ref[:, sl] = (_gelu(z_ref[:, sl]) * sv).astype(o_ref.dtype)


def _gmlp_prompt(z, ln_g, ln_b, w_s, b_s):
    m = z.shape[0]
    return pl.pallas_call(
        _gmlp_prompt_kernel,
        grid=(m // CHUNK,),
        in_specs=[
            pl.BlockSpec((CHUNK, 2 * GM_W), lambda i: (i, 0)),
            pl.BlockSpec((1, GM_W), lambda i: (0, 0)),
            pl.BlockSpec((1, GM_W), lambda i: (0, 0)),
            pl.BlockSpec((GM_GROUPS, CHUNK, CHUNK), lambda i: (0, 0, 0)),
            pl.BlockSpec((CHUNK, GM_GROUPS), lambda i: (0, 0)),
        ],
        out_specs=pl.BlockSpec((CHUNK, GM_W), lambda i: (i, 0)),
        out_shape=jax.ShapeDtypeStruct((m, GM_W), BF16),
        compiler_params=_params("parallel"),
        name="gmlp_prompt",
    )(z, ln_g.reshape(1, GM_W), ln_b.reshape(1, GM_W), w_s, b_s.T)


def _gmlp_first_row_kernel(z_ref, lg_ref, lb_ref, w0_ref, b0_ref, o_ref, v_ref):
    v = _layer_norm(_gelu(z_ref[:, GM_W:2 * GM_W]), lg_ref[...], lb_ref[...])
    v_ref[...] = v
    o_ref[...] = (_gelu(z_ref[:, :GM_W]) * (v * w0_ref[...] + b0_ref[...])).astype(o_ref.dtype)


def _gmlp_first_row(z, ln_g, ln_b, w_s, b_s):
    m = z.shape[0]
    w0 = jnp.repeat(w_s[:, 0, 0], HEAD_DIM).reshape(1, GM_W)
    b0 = jnp.repeat(b_s[:, 0], HEAD_DIM).reshape(1, GM_W)
    return pl.pallas_call(
        _gmlp_first_row_kernel,
        out_shape=(jax.ShapeDtypeStruct((m, GM_W), BF16), jax.ShapeDtypeStruct((m, GM_W), F32)),
        compiler_params=pltpu.CompilerParams(vmem_limit_bytes=VMEM_LIMIT),
        name="gmlp_first_row",
    )(z, ln_g.reshape(1, GM_W), ln_b.reshape(1, GM_W), w0, b0)


def _rope_tables(pos):
    half = ROT_DIM // 2
    inv = ROPE_THETA ** (-jnp.arange(half, dtype=F32) / half)
    ang = pos.astype(F32)[:, None] * inv[None, :]
    cos, sin = jnp.cos(ang), jnp.sin(ang)
    t = pos.shape[0]
    one = jnp.ones((t, HEAD_DIM - ROT_DIM), F32)
    zero = jnp.zeros((t, HEAD_DIM - ROT_DIM), F32)
    zh = jnp.zeros((t, half), F32)
    c = jnp.concatenate([cos, cos, one], axis=1)
    s1 = jnp.concatenate([-sin, zh, zero], axis=1)
    s2 = jnp.concatenate([zh, sin, zero], axis=1)
    return jnp.stack([c, s1, s2])


def _nsa_kv_post_kernel(h_ref, rope_ref, gs_ref, gw_ref, kc_ref, vc_ref, ks_ref, vs_ref, kw_ref, vw_ref,
                        *mxu_refs):
    c, s1, s2 = rope_ref[0], rope_ref[1], rope_ref[2]

    def part(i, g):
        return h_ref[:, i * KV_W + g * HEAD_DIM:i * KV_W + (g + 1) * HEAD_DIM]

    for g in range(NSA_KV):
        sl = slice(g * HEAD_DIM, (g + 1) * HEAD_DIM)
        kc_ref[:, g, :] = part(0, g)
        vc_ref[:, g, :] = part(1, g)
        ks = _rope(_rms(part(2, g), gs_ref[...]), c, s1, s2)
        kw = _rope(_rms(part(4, g), gw_ref[...]), c, s1, s2)
        ks_ref[:, g, :] = ks
        vs_ref[:, g, :] = part(3, g)
        kw_ref[:, g, :] = kw
        vw_ref[:, g, :] = part(5, g)
        if mxu_refs:
            ks16_ref, vst16_ref, kw16_ref, vwt16_ref = mxu_refs
            ks16_ref[:, sl] = ks.astype(BF16)
            kw16_ref[:, sl] = kw.astype(BF16)
            vst16_ref[sl, :] = part(3, g).T.astype(BF16)
            vwt16_ref[sl, :] = part(5, g).T.astype(BF16)


def _nsa_kv_post(h, rope_tab, g_ks, g_kw, mxu_copies):
    m = h.shape[0]
    tm = _row_tile(m, 512)
    out = jax.ShapeDtypeStruct((m, NSA_KV, HEAD_DIM), F32)
    ospec = pl.BlockSpec((tm, NSA_KV, HEAD_DIM), lambda i: (i, 0, 0))
    outs, ospecs = (out,) * 6, (ospec,) * 6
    if mxu_copies:
        k16, v16 = jax.ShapeDtypeStruct((m, KV_W), BF16), jax.ShapeDtypeStruct((KV_W, m), BF16)
        kspec, vspec = pl.BlockSpec((tm, KV_W), lambda i: (i, 0)), pl.BlockSpec((KV_W, tm), lambda i: (0, i))
        outs, ospecs = outs + (k16, v16, k16, v16), ospecs + (kspec, vspec, kspec, vspec)
    return pl.pallas_call(
        _nsa_kv_post_kernel,
        grid=(m // tm,),
        in_specs=[
            pl.BlockSpec((tm, 6 * KV_W), lambda i: (i, 0)),
            pl.BlockSpec((3, tm, HEAD_DIM), lambda i: (0, i, 0)),
            pl.BlockSpec((1, HEAD_DIM), lambda i: (0, 0)),
            pl.BlockSpec((1, HEAD_DIM), lambda i: (0, 0)),
        ],
        out_specs=ospecs,
        out_shape=outs,
        compiler_params=_params("parallel"),
        name="nsa_kv_post",
    )(h, rope_tab, g_ks.reshape(1, HEAD_DIM), g_kw.reshape(1, HEAD_DIM))


def _compress_kernel(pt_ref, *refs, n_pages):
    n_in = 2 * PAGES_PER_STEP
    page_refs = refs[:n_in]
    w1_refs, pe_refs, w2_refs = refs[n_in:n_in + 2], refs[n_in + 2:n_in + 4], refs[n_in + 4:n_in + 6]
    gk_ref = refs[n_in + 6]
    o_refs = refs[n_in + 7:n_in + 9]
    l_scr, a_scr, b_scr = refs[n_in + 9:]
    j = pl.program_id(1)
    steps_per_group = PAGES_PER_GROUP // PAGES_PER_STEP
    jj = j % steps_per_group
    n_chunks = n_pages * CHUNKS_PER_PAGE
    group_chunks = PAGES_PER_GROUP * CHUNKS_PER_PAGE

    @pl.when(j == 0)
    def _():
        b_scr[:, :, n_chunks:, :] = jnp.zeros((2, NSA_KV, SUBLANE, HEAD_DIM), F32)

    pairs = CMP_STRIDE * NSA_KV // SUBLANE
    for t in range(2):
        for q in range(PAGES_PER_STEP):
            row0 = pl.multiple_of((jj * PAGES_PER_STEP + q) * CHUNKS_PER_PAGE, CHUNKS_PER_PAGE)
            x4 = page_refs[t * PAGES_PER_STEP + q][0].reshape(CHUNKS_PER_PAGE, pairs, SUBLANE, HEAD_DIM)
            for p2 in range(pairs):
                y = jnp.swapaxes(x4[:, p2], 0, 1)
                for s in range(SUBLANE):
                    p, g = (p2 * SUBLANE + s) // NSA_KV, s % NSA_KV
                    l_scr[t, g, pl.ds(row0, CHUNKS_PER_PAGE), p * HEAD_DIM:(p + 1) * HEAD_DIM] = y[s]

    @pl.when(jj == steps_per_group - 1)
    def _():
        base = pl.multiple_of((j // steps_per_group) * group_chunks, group_chunks)
        for t in range(2):
            for g in range(NSA_KV):
                r = _dot(l_scr[t, g], w1_refs[t][g])
                a_scr[t, g, pl.ds(base, group_chunks), :] = r[:, :HEAD_DIM]
                b_scr[t, g, pl.ds(base, group_chunks), :] = r[:, HEAD_DIM:]

    @pl.when(j == pl.num_programs(1) - 1)
    def _():
        for t in range(2):
            for g in range(NSA_KV):
                pw = _dot(pe_refs[t][g], w1_refs[t][g])
                bias = pw[0:1, :HEAD_DIM] + pw[1:2, HEAD_DIM:]
                h = a_scr[t, g] + b_scr[t, g, pl.ds(1, n_chunks), :] + bias
                y = _dot(_gelu(h), w2_refs[t][g])
                if t == 0:
                    y = _rms(y, gk_ref[...])
                o_refs[t][0, :, g * HEAD_DIM:(g + 1) * HEAD_DIM] = y


def _compress(pool_k, pool_v, page_table, pos_k, w1_k, w2_k, pos_v, w1_v, w2_v, g_kc):
    nb, n_pages = page_table.shape
    assert n_pages % PAGES_PER_GROUP == 0
    n_chunks = n_pages * CHUNKS_PER_PAGE
    kdim = CMP_STRIDE * HEAD_DIM

    def w1ab(w1):
        return jnp.concatenate([w1[:, :CMP_STRIDE].reshape(NSA_KV, kdim, HEAD_DIM),
                                w1[:, CMP_STRIDE:].reshape(NSA_KV, kdim, HEAD_DIM)], axis=-1).astype(BF16)

    def rows_view(pool):
        return pool.reshape(pool.shape[0], PAGE_SIZE * NSA_KV, HEAD_DIM)

    def pe8(pos_enc):
        return jnp.pad(pos_enc.reshape(NSA_KV, 2, kdim), ((0, 0), (0, SUBLANE - 2), (0, 0)))

    def page_spec(q):
        return pl.BlockSpec((1, PAGE_SIZE * NSA_KV, HEAD_DIM),
                            lambda b, j, pt: (pt[b, j * PAGES_PER_STEP + q], 0, 0))

    const3 = lambda b, j, pt: (0, 0, 0)
    page_specs = [page_spec(q) for q in range(PAGES_PER_STEP)]
    w1_spec = _resident((NSA_KV, kdim, 2 * HEAD_DIM), const3)
    pe_spec = _resident((NSA_KV, SUBLANE, kdim), const3)
    w2_spec = _resident((NSA_KV, HEAD_DIM, HEAD_DIM), const3)
    out = jax.ShapeDtypeStruct((nb, n_chunks, KV_W), F32)
    ospec = pl.BlockSpec((1, n_chunks, KV_W), lambda b, j, pt: (b, 0, 0))
    grid_spec = pltpu.PrefetchScalarGridSpec(
        num_scalar_prefetch=1,
        grid=(nb, n_pages // PAGES_PER_STEP),
        in_specs=page_specs + page_specs + [w1_spec, w1_spec, pe_spec, pe_spec, w2_spec, w2_spec,
                                             pl.BlockSpec((1, HEAD_DIM), lambda b, j, pt: (0, 0))],
        out_specs=(ospec, ospec),
        scratch_shapes=[
            pltpu.VMEM((2, NSA_KV, PAGES_PER_GROUP * CHUNKS_PER_PAGE, kdim), F32),
            pltpu.VMEM((2, NSA_KV, n_chunks, HEAD_DIM), F32),
            pltpu.VMEM((2, NSA_KV, n_chunks + SUBLANE, HEAD_DIM), F32),
        ],
    )
    return pl.pallas_call(
        partial(_compress_kernel, n_pages=n_pages),
        grid_spec=grid_spec,
        out_shape=(out, out),
        compiler_params=_params("parallel", "arbitrary"),
        name="compress",
    )(page_table, *([rows_view(pool_k)] * PAGES_PER_STEP), *([rows_view(pool_v)] * PAGES_PER_STEP),
      w1ab(w1_k), w1ab(w1_v), pe8(pos_k), pe8(pos_v), w2_k, w2_v, g_kc.reshape(1, HEAD_DIM))


def _gate_columns(sig, g, rows):
    lane = lax.broadcasted_iota(jnp.int32, (rows, LANE), 1)
    base = g * (NSA_REP * N_GATES)
    return [[jnp.sum(jnp.where(lane == base + r * N_GATES + t, sig, 0.0), axis=1, keepdims=True)
             for t in range(N_GATES)] for r in range(NSA_REP)]


def _rejected_t(score_t, avail_t, n_rows):
    groups = n_rows // SUBLANE
    rows = [score_t[r * SUBLANE:(r + 1) * SUBLANE] for r in range(groups)]
    sub = lax.broadcasted_iota(jnp.int32, rows[0].shape, 0)
    cnt = [jnp.zeros(rows[0].shape, F32) for _ in range(groups)]
    for i in range(n_rows):
        si = score_t[i:i + 1, :]
        gi, oi = divmod(i, SUBLANE)
        for r in range(groups):
            if r > gi:
                beats = si >= rows[r]
            elif r < gi:
                beats = si > rows[r]
            else:
                beats = (si > rows[r]) | ((si == rows[r]) & (sub > oi))
            cnt[r] = cnt[r] + jnp.where(beats, 1.0, 0.0)
    cnt = jnp.concatenate(cnt, axis=0)
    return jnp.where((cnt < N_SELECT) & avail_t, 0.0, 1.0)


def _nsa_prompt_kernel(q_ref, gate_ref, gq_ref, rope_ref, kc_ref, vc_ref, ks_ref, vst_ref, kw_ref, vwt_ref,
                       o_ref, m_scr, l_scr, acc_scr, *, n_sel, n_cmp):
    g = pl.program_id(1)
    qb = pl.program_id(2)
    t0 = qb * QB
    rq = NSA_REP * QB
    c, s1, s2 = rope_ref[0], rope_ref[1], rope_ref[2]
    gq = gq_ref[...]
    qn_l, qr_l = [], []
    for r in range(NSA_REP):
        qn_r = _rms(q_ref[0, :, r * HEAD_DIM:(r + 1) * HEAD_DIM], gq)
        qn_l.append(qn_r * SCALE_LOG2E)
        qr_l.append(_rope(qn_r, c, s1, s2) * SCALE_LOG2E)
    qn = jnp.concatenate(qn_l, axis=0).astype(BF16)
    qr = jnp.concatenate(qr_l, axis=0).astype(BF16)
    pos1 = t0 + lax.broadcasted_iota(jnp.int32, (QB, 1), 0)
    pos3 = jnp.concatenate([pos1] * NSA_REP, axis=0)

    ci = lax.broadcasted_iota(jnp.int32, (1, n_cmp), 1)
    p = _masked_softmax2(_dot_nt(qn, kc_ref[0]), ci * CMP_STRIDE + (CMP_LEN - 1) <= pos3)
    o_c = _dot(p, vc_ref[0])
    psum = p[0:QB] + p[QB:2 * QB] + p[2 * QB:3 * QB]

    ci_l = lax.broadcasted_iota(jnp.int32, (n_sel, n_cmp), 1) * CMP_STRIDE
    sj = lax.broadcasted_iota(jnp.int32, (n_sel, n_cmp), 0) * SEL_BLOCK
    cover_t = jnp.where((ci_l < sj + SEL_BLOCK) & (ci_l + CMP_LEN > sj), 1.0, 0.0)
    imp_t = lax.dot_general(cover_t, psum, (((1,), (1,)), ((), ())),
                            precision=lax.Precision.HIGHEST, preferred_element_type=F32)
    post = t0 + lax.broadcasted_iota(jnp.int32, (n_sel, QB), 1)
    jt = lax.broadcasted_iota(jnp.int32, (n_sel, QB), 0)
    cur = post >> SEL_SHIFT
    avail_t = jt * SEL_BLOCK <= post
    forced_t = (jt == 0) | (jt == cur) | (jt == cur - 1)
    score_t = jnp.where(avail_t, jnp.where(forced_t, FORCED_SCORE, imp_t), NEG)
    rej_t = _rejected_t(score_t, avail_t, n_sel)
    rej = jnp.concatenate([rej_t, jnp.zeros((LANE - n_sel, QB), F32)], axis=0).T.astype(BF16)

    pos_q = t0 + lax.broadcasted_iota(jnp.int32, (1, QB), 1)
    per_head = lambda a: jnp.concatenate([a] * NSA_REP, axis=1)
    q_cat = jnp.concatenate([qr, jnp.concatenate([rej] * NSA_REP, axis=0)], axis=1)

    def normalised(acc_t, den):
        o_t = acc_t / den
        return [o_t[:, r * QB:(r + 1) * QB].T for r in range(NSA_REP)]

    m_scr[...] = jnp.full((1, rq), NEG, F32)
    l_scr[...] = jnp.zeros((1, rq), F32)
    acc_scr[...] = jnp.zeros((HEAD_DIM, rq), F32)
    blocks_per_chunk = SEL_KEY_CHUNK // SEL_BLOCK

    def sel_step(ck, causal):
        k0 = pl.multiple_of(ck * SEL_KEY_CHUNK, SEL_KEY_CHUNK)
        kr = lax.broadcasted_iota(jnp.int32, (SEL_KEY_CHUNK, LANE), 0)
        jl = lax.broadcasted_iota(jnp.int32, (SEL_KEY_CHUNK, LANE), 1)
        expand_t = jnp.where((kr >> SEL_SHIFT) + ck * blocks_per_chunk == jl, NEG, 0.0).astype(BF16)
        sc = _dot_nt(jnp.concatenate([ks_ref[0, pl.ds(k0, SEL_KEY_CHUNK), :], expand_t], axis=1), q_cat)
        if causal:
            kpos = k0 + lax.broadcasted_iota(jnp.int32, (SEL_KEY_CHUNK, 1), 0)
            sc = sc + per_head(jnp.where(kpos <= pos_q, 0.0, NEG))
        m_old = m_scr[...]
        m_new = jnp.maximum(m_old, jnp.max(sc, axis=0, keepdims=True))
        e = jnp.exp2(sc - m_new)
        alpha = jnp.exp2(m_old - m_new)
        l_scr[...] = alpha * l_scr[...] + jnp.sum(e, axis=0, keepdims=True)
        acc_scr[...] = alpha * acc_scr[...] + jnp.dot(vst_ref[:, pl.ds(k0, SEL_KEY_CHUNK)], e.astype(BF16),
                                                      preferred_element_type=F32)
        m_scr[...] = m_new

    n_chunks = (t0 + QB + SEL_KEY_CHUNK - 1) // SEL_KEY_CHUNK

    def sel_body(ck, carry):
        sel_step(ck, False)
        return carry

    lax.fori_loop(0, n_chunks - 1, sel_body, 0)
    sel_step(n_chunks - 1, True)
    o_s = normalised(acc_scr[...], l_scr[...])

    band = WINDOW + QB
    k0 = pl.multiple_of(jnp.maximum(t0 - WINDOW, 0), QB)
    ahead = k0 + lax.broadcasted_iota(jnp.int32, (band, 1), 0) - pos_q
    sw = (_dot_nt(kw_ref[0, pl.ds(k0, band), :], qr)
          + per_head(jnp.where((ahead <= 0) & (ahead >= -WINDOW), 0.0, NEG)))
    ew = jnp.exp2(sw - jnp.max(sw, axis=0, keepdims=True))
    o_w = normalised(jnp.dot(vwt_ref[:, pl.ds(k0, band)], ew.astype(BF16), preferred_element_type=F32),
                     jnp.sum(ew, axis=0, keepdims=True))

    gate = _gate_columns(jax.nn.sigmoid(gate_ref[0]), g, QB)
    for r in range(NSA_REP):
        o_ref[0, :, r * HEAD_DIM:(r + 1) * HEAD_DIM] = (
            gate[r][0] * o_c[r * QB:(r + 1) * QB] + gate[r][1] * o_s[r] + gate[r][2] * o_w[r]).astype(o_ref.dtype)


def _nsa_prompt(z, gates, g_q, rope_tab, kc, vc, ks, vst, kw, vwt):
    b, t, _ = z.shape
    n_cmp = kc.shape[1]
    assert t >= WINDOW + QB
    qw = NSA_REP * HEAD_DIM
    k_spec = pl.BlockSpec((1, t, HEAD_DIM), lambda i, g, j: (i, 0, g))
    vt_spec = pl.BlockSpec((HEAD_DIM, t), lambda i, g, j: (g, i))
    cmp_spec = pl.BlockSpec((1, n_cmp, HEAD_DIM), lambda i, g, j: (i, 0, g))
    return pl.pallas_call(
        partial(_nsa_prompt_kernel, n_sel=t // SEL_BLOCK, n_cmp=n_cmp),
        grid=(b, NSA_KV, t // QB),
        in_specs=[
            pl.BlockSpec((1, QB, qw), lambda i, g, j: (i, j, g)),
            pl.BlockSpec((1, QB, LANE), lambda i, g, j: (i, j, 0)),
            pl.BlockSpec((1, HEAD_DIM), lambda i, g, j: (0, 0)),
            pl.BlockSpec((3, QB, HEAD_DIM), lambda i, g, j: (0, j, 0)),
            cmp_spec, cmp_spec, k_spec, vt_spec, k_spec, vt_spec,
        ],
        out_specs=pl.BlockSpec((1, QB, qw), lambda i, g, j: (i, j, g)),
        out_shape=jax.ShapeDtypeStruct((b, t, NSA_W), BF16),
        scratch_shapes=[
            pltpu.VMEM((1, NSA_REP * QB), F32),
            pltpu.VMEM((1, NSA_REP * QB), F32),
            pltpu.VMEM((HEAD_DIM, NSA_REP * QB), F32),
        ],
        compiler_params=_params("parallel", "parallel", "arbitrary"),
        name="nsa_prompt",
    )(z, gates, g_q.reshape(1, HEAD_DIM), rope_tab, kc, vc, ks, vst, kw, vwt)


def _nsa_sample_select_kernel(q_ref, gq_ref, rope_ref, kc_ref, vc_ref, kw_ref, vw_ref,
                              qr_ref, oc_ref, ow_ref, idx_ref, valid_ref, **static):
    for g in range(NSA_KV):
        sl = slice(g * HEAD_DIM, (g + 1) * HEAD_DIM)
        q3 = [q_ref[0, :, (g * NSA_REP + r) * HEAD_DIM:(g * NSA_REP + r + 1) * HEAD_DIM] for r in range(NSA_REP)]
        qr_ref[0, g], oc_ref[0, g], ow_ref[0, g], idx_ref[0, g], valid_ref[0, g] = _select_group(
            q3, gq_ref[...], rope_ref, kc_ref[0, :, sl], vc_ref[0, :, sl], kw_ref[0, :, g, :], vw_ref[0, :, g, :],
            **static)


def _select_group(q3, gq, rope_ref, kc, vc, kw, vw, *, pos, n_sel, n_cmp, n_win):
    n_cand = SEL_CAND_LANES
    c, s1, s2 = rope_ref[0:1], rope_ref[1:2], rope_ref[2:3]
    row = lax.broadcasted_iota(jnp.int32, (SUBLANE, HEAD_DIM), 0)
    q3 = [jnp.broadcast_to(q, (SUBLANE, HEAD_DIM)) for q in q3]
    q8 = jnp.where(row == 0, q3[0], jnp.where(row == 1, q3[1], q3[2]))
    qn = _rms(q8, gq)
    qr = _rope(qn, c, s1, s2) * SCALE_LOG2E

    ci = lax.broadcasted_iota(jnp.int32, (1, n_cmp), 1)
    p = _masked_softmax2(_dot_nt(qn * SCALE_LOG2E, kc), ci * CMP_STRIDE + (CMP_LEN - 1) <= pos)
    o_c = _dot(p, vc)
    psum = p[0:1] + p[1:2] + p[2:3]

    ci_s = lax.broadcasted_iota(jnp.int32, (n_cmp, n_cand), 0) * CMP_STRIDE
    sj = lax.broadcasted_iota(jnp.int32, (n_cmp, n_cand), 1) * SEL_BLOCK
    cover = jnp.where((ci_s < sj + SEL_BLOCK) & (ci_s + CMP_LEN > sj), 1.0, 0.0)
    imp = jnp.dot(jnp.broadcast_to(psum, (SUBLANE, n_cmp)), cover,
                  precision=lax.Precision.HIGHEST, preferred_element_type=F32)[0:1]
    jl = lax.broadcasted_iota(jnp.int32, (1, n_cand), 1)
    cur = pos // SEL_BLOCK
    avail = jl * SEL_BLOCK <= pos
    forced = (jl == 0) | (jl == cur) | (jl == cur - 1)
    score = jnp.where(avail, jnp.where(forced, FORCED_SCORE, imp), NEG)
    score = jnp.where(jl < n_sel, score, EXCLUDED)

    eye = (lax.broadcasted_iota(jnp.int32, (n_cand, n_cand), 0)
           == lax.broadcasted_iota(jnp.int32, (n_cand, n_cand), 1))
    score_b = jnp.broadcast_to(score, (n_cand, n_cand))
    score_c = jnp.sum(jnp.where(eye, score_b, 0.0), axis=1, keepdims=True)
    il = lax.broadcasted_iota(jnp.int32, (n_cand, n_cand), 1)
    jc = lax.broadcasted_iota(jnp.int32, (n_cand, n_cand), 0)
    beats = (score_b > score_c) | ((score_b == score_c) & (il < jc))
    rank = jnp.sum(jnp.where(beats, 1.0, 0.0), axis=1, keepdims=True)
    kl = lax.broadcasted_iota(jnp.int32, (n_cand, LANE), 1).astype(F32)
    hit = rank == kl
    jcol = lax.broadcasted_iota(jnp.int32, (n_cand, LANE), 0)
    idx = jnp.sum(jnp.where(hit, jcol.astype(F32), 0.0), axis=0, keepdims=True).astype(jnp.int32)
    valid = jnp.sum(jnp.where(hit & (jcol * SEL_BLOCK <= pos), 1.0, 0.0),
                    axis=0, keepdims=True).astype(jnp.int32)

    n_buf = kw.shape[0]
    kpos = pos + 1 - n_win + lax.broadcasted_iota(jnp.int32, (1, n_buf), 1)
    pw = _masked_softmax2(_dot_nt(qr, kw), (kpos >= pos - WINDOW) & (kpos <= pos))
    return qr, o_c, _dot(pw, vw), idx, valid


def _nsa_sample_gather_kernel(idx_ref, valid_ref, pt_ref, *refs, pos, n_past):
    n_blk = NSA_KV * N_SELECT
    kp_refs, vp_refs = refs[:n_blk], refs[n_blk:2 * n_blk]
    kn_ref, vn_ref, qr_ref, oc_ref, ow_ref, gate_ref, o_ref = refs[2 * n_blk:]
    b = pl.program_id(0)

    col = lax.broadcasted_iota(jnp.int32, (1, SEL_BLOCK * NSA_KV), 1)
    col_row, col_group = col >> (NSA_KV.bit_length() - 1), col & (NSA_KV - 1)
    sig = jax.nn.sigmoid(gate_ref[0])
    for g in range(NSA_KV):
        scores, masks, values = [], [], []
        for k in range(N_SELECT):
            idx = idx_ref[b, g, k]
            in_past = idx < n_past
            k_blk = jnp.where(in_past, kp_refs[g * N_SELECT + k][0], kn_ref[0])
            values.append(jnp.where(in_past, vp_refs[g * N_SELECT + k][0], vn_ref[0]))
            kpos = idx * SEL_BLOCK + col_row
            msk = (col_group == g) & (kpos <= jnp.where(valid_ref[b, g, k] > 0, pos, -1))
            masks.append(msk)
            scores.append(jnp.where(msk, _dot_nt(qr_ref[0, g], k_blk), NEG))
        mx = scores[0].max(axis=-1, keepdims=True)
        for sc in scores[1:]:
            mx = jnp.maximum(mx, sc.max(axis=-1, keepdims=True))
        den = jnp.zeros((SUBLANE, 1), F32)
        acc = jnp.zeros((SUBLANE, HEAD_DIM), F32)
        for sc, msk, v_blk in zip(scores, masks, values):
            e = jnp.where(msk, jnp.exp2(sc - mx), 0.0)
            den = den + jnp.sum(e, axis=-1, keepdims=True)
            acc = acc + _dot(e, v_blk)
        o_s = acc / den
        gate = _gate_columns(sig, g, 1)
        o_c, o_w = oc_ref[0, g], ow_ref[0, g]
        for r in range(NSA_REP):
            h = g * NSA_REP + r
            o_ref[0, :, h * HEAD_DIM:(h + 1) * HEAD_DIM] = (
                gate[r][0] * o_c[r:r + 1] + gate[r][1] * o_s[r:r + 1] + gate[r][2] * o_w[r:r + 1]
            ).astype(o_ref.dtype)


def _nsa_sample(zq, zgate, g_q, rope_row, kc, vc, kw_buf, vw_buf, n_win, pool_k, pool_v, k_new, v_new,
                page_table, pos):
    db = zq.shape[0]
    n_cmp = kc.shape[1]
    n_buf = kw_buf.shape[1]
    n_past = page_table.shape[1] * (PAGE_SIZE // SEL_BLOCK)
    n_sel = n_past + 1
    assert n_sel <= SEL_CAND_LANES
    qw = NSA_REP * HEAD_DIM
    vec = jax.ShapeDtypeStruct((db, NSA_KV, SUBLANE, HEAD_DIM), F32)
    ivec = jax.ShapeDtypeStruct((db, NSA_KV, 1, LANE), jnp.int32)
    vspec = pl.BlockSpec((1, NSA_KV, SUBLANE, HEAD_DIM), lambda i: (i, 0, 0, 0))
    ispec = pl.BlockSpec((1, NSA_KV, 1, LANE), lambda i: (i, 0, 0, 0))
    cmp_spec = pl.BlockSpec((1, n_cmp, KV_W), lambda i: (i, 0, 0))
    win_spec = pl.BlockSpec((1, n_buf, NSA_KV, HEAD_DIM), lambda i: (i, 0, 0, 0))
    qr, o_c, o_w, idx, valid = pl.pallas_call(
        partial(_nsa_sample_select_kernel, pos=pos, n_sel=n_sel, n_cmp=n_cmp, n_win=n_win),
        grid=(db,),
        in_specs=[
            pl.BlockSpec((1, 1, NSA_W), lambda i: (i, 0, 0)),
            pl.BlockSpec((1, HEAD_DIM), lambda i: (0, 0)),
            pl.BlockSpec((3, HEAD_DIM), lambda i: (0, 0)),
            cmp_spec, cmp_spec, win_spec, win_spec,
        ],
        out_specs=(vspec, vspec, vspec, ispec, ispec),
        out_shape=(vec, vec, vec, ivec, ivec),
        compiler_params=_params("parallel"),
        name="nsa_sample_select",
    )(zq, g_q.reshape(1, HEAD_DIM), rope_row, kc, vc, kw_buf, vw_buf)

    idx = idx[:, :, 0, :N_SELECT]
    valid = valid[:, :, 0, :N_SELECT]
    halves = PAGE_SIZE // SEL_BLOCK

    def pool_spec(g, k):
        def pool_block(i, idx_r, valid_r, pt_r):
            blk = jnp.minimum(idx_r[i, g, k], n_past - 1)
            return (pt_r[i, blk // halves] * halves + blk % halves, 0, 0)
        return pl.BlockSpec((1, SEL_BLOCK * NSA_KV, HEAD_DIM), pool_block)

    n_pool = pool_k.shape[0]
    pool_specs = [pool_spec(g, k) for g in range(NSA_KV) for k in range(N_SELECT)]
    gvec = pl.BlockSpec((1, NSA_KV, SUBLANE, HEAD_DIM), lambda i, *_: (i, 0, 0, 0))
    new_spec = pl.BlockSpec((1, SEL_BLOCK * NSA_KV, HEAD_DIM), lambda i, *_: (i, 0, 0))
    grid_spec = pltpu.PrefetchScalarGridSpec(
        num_scalar_prefetch=3,
        grid=(db,),
        in_specs=pool_specs + pool_specs + [
            new_spec, new_spec, gvec, gvec, gvec,
            pl.BlockSpec((1, 1, LANE), lambda i, *_: (i, 0, 0)),
        ],
        out_specs=pl.BlockSpec((1, 1, NSA_W), lambda i, *_: (i, 0, 0)),
    )
    pool_k2 = pool_k.reshape(n_pool * halves, SEL_BLOCK * NSA_KV, HEAD_DIM)
    pool_v2 = pool_v.reshape(n_pool * halves, SEL_BLOCK * NSA_KV, HEAD_DIM)

    def new_block(a):
        a = jnp.pad(a, ((0, 0), (0, SEL_BLOCK - a.shape[1]), (0, 0), (0, 0)))
        return a.reshape(db, SEL_BLOCK * NSA_KV, HEAD_DIM)

    k_new, v_new = new_block(k_new), new_block(v_new)
    return pl.pallas_call(
        partial(_nsa_sample_gather_kernel, pos=pos, n_past=n_past),
        grid_spec=grid_spec,
        out_shape=jax.ShapeDtypeStruct((db, 1, NSA_W), BF16),
        compiler_params=_params("parallel"),
        name="nsa_sample_gather",
    )(idx, valid, page_table, *([pool_k2] * len(pool_specs)), *([pool_v2] * len(pool_specs)),
      k_new, v_new, qr, o_c, o_w, zgate)


def kernel(x_prompt, x_sample, mem_prompt, cache_mem_k, cache_mem_v, cache_k_cmp, cache_v_cmp, cache_k_sel, cache_v_sel, state_k_win, state_v_win, page_table, g_ffn1, w_ffn1_gate, w_ffn1_up, w_ffn1_down, g_mix, g_ffn2, w_ffn2_gate, w_ffn2_up, w_ffn2_down, g_mem, w_mem_kv, g_mq, g_mk, w_in_a, ln_v_g, ln_v_b, w_spatial, b_spatial, w_out_a, g_kv, w_kv, pos_kc, w1_kc, w2_kc, pos_vc, w1_vc, w2_vc, g_kc, g_ks, g_kw, w_in_b, g_q, w_out_b):
    nb, t, d = x_prompt.shape
    db, ds, _ = x_sample.shape
    depth = g_ffn1.shape[0]
    n_a = w_in_a.shape[0]
    past_len = page_table.shape[1] * PAGE_SIZE
    assert ds == 1 and t % PAGE_SIZE == 0 and db * ds <= SIDE_ROWS
    pos_s = past_len
    ns = db * ds
    side = lambda a: jnp.pad(a, ((0, SIDE_ROWS - ns), (0, 0)))

    xp = x_prompt.reshape(nb * t, d)
    xs = side(x_sample.reshape(ns, d))
    mem = mem_prompt.reshape(nb * MEM_LEN, d)
    w_kv16, w_mem_kv16, w_in_a16 = w_kv.astype(BF16), w_mem_kv.astype(BF16), w_in_a.astype(BF16)
    w_out_a16, w_out_b16 = w_out_a.astype(BF16), w_out_b.astype(BF16)
    mem_k_list, mem_v_list, gm_v_list = [], [], []
    rope_p = _rope_tables(jnp.arange(t, dtype=jnp.int32))
    rope_s = _rope_tables(pos_s + jnp.arange(ds, dtype=jnp.int32))
    rope_p2 = jnp.tile(rope_p, (1, nb, 1))
    kv4 = lambda a, n: a.reshape(n, -1, NSA_KV, HEAD_DIM)

    for l in range(depth):
        if l == n_a:
            hp, hs = _rms_mm(xp, xs, g_kv, w_kv16)
            hs = hs[:ns]
            post_p = _nsa_kv_post(hp, rope_p2, g_ks, g_kw, True)
            kc_p_raw, vc_p_raw, ks_p, vs_p, kw_p, vw_p = [kv4(a, nb) for a in post_p[:6]]
            ks16, vst16, kw16, vwt16 = post_p[6:]
            kv16_p = [ks16.reshape(nb, t, KV_W), vst16, kw16.reshape(nb, t, KV_W), vwt16]
            post_s = _nsa_kv_post(hs, jnp.broadcast_to(rope_s, (3, db, HEAD_DIM)), g_ks, g_kw, False)
            kc_s_raw, vc_s_raw, ks_s, vs_s, kw_s, vw_s = [kv4(a, db) for a in post_s[:6]]
            pages_p = jnp.arange(nb * t // PAGE_SIZE, dtype=jnp.int32).reshape(nb, t // PAGE_SIZE)
            cmp_w = (pos_kc, w1_kc, w2_kc, pos_vc, w1_vc, w2_vc, g_kc)
            kc_p, vc_p = _compress(kc_p_raw.reshape(-1, PAGE_SIZE, NSA_KV, HEAD_DIM),
                                   vc_p_raw.reshape(-1, PAGE_SIZE, NSA_KV, HEAD_DIM), pages_p, *cmp_w)
            kc_s, vc_s = _compress(cache_k_cmp, cache_v_cmp, page_table, *cmp_w)
            wbuf = state_k_win.shape[1]
            n_win = wbuf + ds
            pad = jnp.zeros((db, (-n_win) % SUBLANE, NSA_KV, HEAD_DIM), F32)
            kw_buf = jnp.concatenate([state_k_win, kw_s, pad], axis=1)
            vw_buf = jnp.concatenate([state_v_win, vw_s, pad], axis=1)

        xp, xs = _ffn(xp, xs, g_ffn1, w_ffn1_gate, w_ffn1_up, w_ffn1_down, l)

        mk_p, mv_p = _mem_kv_post(_rms_mm(mem, None, g_mem[l], w_mem_kv16, l), g_mk[l])
        mk_p = mk_p.reshape(nb, MEM_LEN, MEM_HEADS, HEAD_DIM)
        mv_p = mv_p.reshape(nb, MEM_LEN, MEM_HEADS, HEAD_DIM)
        mem_k_list.append(mk_p)
        mem_v_list.append(mv_p)
        mk_s, mv_s = cache_mem_k[l], cache_mem_v[l]

        if l < n_a:
            zp, zs = _rms_mm(xp, xs, g_mix[l], w_in_a16, l)
            zs = zs[:ns]
            mem_blk = 2 * GM_W // MEM_W
            o1_p = _gmlp_prompt(zp, ln_v_g[l], ln_v_b[l], w_spatial[l], b_spatial[l])
            o1_s, v_s = _gmlp_first_row(zs, ln_v_g[l], ln_v_b[l], w_spatial[l], b_spatial[l])
            gm_v_list.append(v_s.reshape(db, ds, GM_W))
            o2_p = _mem_attend(zp.reshape(nb, t, -1), mem_blk, mk_p, mv_p, g_mq[l])
            o2_s = _mem_attend(zs.reshape(db, ds, -1), mem_blk, mk_s, mv_s, g_mq[l])
            w_out, l_out = w_out_a16, l
        else:
            lb = l - n_a
            n_gate = N_GATES * NSA_HEADS
            w_qm = jnp.concatenate([w_in_b[lb][:, :NSA_W], w_in_b[lb][:, NSA_W + n_gate:]], axis=1).astype(BF16)
            w_gate = jnp.pad(w_in_b[lb][:, NSA_W:NSA_W + n_gate], ((0, 0), (0, LANE - n_gate))).astype(BF16)
            zp, zs = _rms_mm(xp, xs, g_mix[l], w_qm)
            gp, gs = _rms_mm(xp, xs, g_mix[l], w_gate)
            zp, zs = zp.reshape(nb, t, -1), zs[:ns].reshape(db, ds, -1)
            gp, gs = gp.reshape(nb, t, LANE), gs[:ns].reshape(db, ds, LANE)
            o1_p = _nsa_prompt(zp, gp, g_q[lb], rope_p, kc_p, vc_p, *kv16_p).reshape(nb * t, NSA_W)
            o1_s = _nsa_sample(zs, gs, g_q[lb], rope_s[:, 0, :], kc_s, vc_s, kw_buf, vw_buf, n_win,
                               cache_k_sel, cache_v_sel, ks_s, vs_s, page_table, pos_s).reshape(db * ds, NSA_W)
            mem_blk = NSA_W // MEM_W
            o2_p = _mem_attend(zp, mem_blk, mk_p, mv_p, g_mq[l])
            o2_s = _mem_attend(zs, mem_blk, mk_s, mv_s, g_mq[l])
            w_out, l_out = w_out_b16, lb
        xp, xs = _out_proj(o1_p, o2_p.reshape(nb * t, MEM_W), side(o1_s), side(o2_s.reshape(ns, MEM_W)),
                           w_out, l_out, xp, xs)

        xp, xs = _ffn(xp, xs, g_ffn2, w_ffn2_gate, w_ffn2_up, w_ffn2_down, l)

    wp = min(WINDOW, t)
    ws = min(WINDOW, past_len + ds)
    return (xp.reshape(nb, t, d), xs[:ns].reshape(db, ds, d), jnp.stack(mem_k_list), jnp.stack(mem_v_list),
            kc_p_raw, vc_p_raw, ks_p, vs_p, kw_p[:, -wp:], vw_p[:, -wp:],
            kc_s_raw, vc_s_raw, ks_s, vs_s, kw_buf[:, n_win - ws:n_win], vw_buf[:, n_win - ws:n_win],
            jnp.stack(gm_v_list))
```

```python
from functools import partial

import numpy as np
import jax
import jax.numpy as jnp
from jax import lax
from jax.experimental import pallas as pl
from jax.experimental.pallas import tpu as pltpu

D_MODEL = 2048
HEAD_DIM = 128
ROT_DIM = HEAD_DIM // 4
ROPE_THETA = 500000.0
MEM_LEN = 256
MEM_HEADS = 4
MEM_W = MEM_HEADS * HEAD_DIM
GM_GROUPS = 12
GM_W = GM_GROUPS * HEAD_DIM
CHUNK = 128
NSA_HEADS = 12
NSA_KV = 4
NSA_REP = NSA_HEADS // NSA_KV
NSA_W = NSA_HEADS * HEAD_DIM
KV_W = NSA_KV * HEAD_DIM
N_GATES = 3
CMP_LEN = 32
CMP_STRIDE = 16
SEL_BLOCK = 64
N_SELECT = 16
WINDOW = 512
QB = 512
PAGE_SIZE = 128
EPS = 1e-6
NEG = -1e30
EXCLUDED = -3e38
FORCED_SCORE = 1e6
SCALE = HEAD_DIM ** -0.5
SCALE_LOG2E = float(SCALE * np.log2(np.e))

LANE = 128
SUBLANE = 8
VMEM_LIMIT = 56 * 1024 * 1024
ROW_TILE = 1024
SIDE_ROWS = 16
PAGES_PER_GROUP = 16
PAGES_PER_STEP = 8
CHUNKS_PER_PAGE = PAGE_SIZE // CMP_STRIDE
SEL_KEY_CHUNK = 512
SEL_CAND_LANES = 3 * LANE
SEL_SHIFT = SEL_BLOCK.bit_length() - 1

BF16 = jnp.bfloat16
F32 = jnp.float32


def _params(*sem):
    return pltpu.CompilerParams(dimension_semantics=sem, vmem_limit_bytes=VMEM_LIMIT)


def _dot(a, b):
    return jnp.dot(a.astype(BF16), b.astype(BF16), preferred_element_type=F32)


def _dot_nt(a, b):
    return lax.dot_general(a.astype(BF16), b.astype(BF16), (((1,), (1,)), ((), ())),
                           preferred_element_type=F32)


def _rms(x, g):
    return x * lax.rsqrt(jnp.mean(x * x, axis=-1, keepdims=True) + EPS) * g


def _gelu(x):
    return 0.5 * x * (1.0 + lax.erf(x * np.float32(np.sqrt(0.5))))


def _rope(x, c, s1, s2):
    half = ROT_DIM // 2
    return x * c + pltpu.roll(x, LANE - half, 1) * s1 + pltpu.roll(x, half, 1) * s2


def _masked_softmax2(s2, m):
    sm = jnp.where(m, s2, NEG)
    mx = jnp.max(sm, axis=-1, keepdims=True)
    e = jnp.where(m, jnp.exp2(sm - mx), 0.0)
    den = jnp.sum(e, axis=-1, keepdims=True)
    return e / jnp.where(den > 0.0, den, 1.0)


def _with_ones(v):
    return jnp.concatenate([v.astype(BF16), jnp.ones(v.shape, BF16)], axis=1)


def _row_tile(m, want):
    return want if m % want == 0 else m


def _resident(shape, index_map):
    return pl.BlockSpec(shape, index_map, pipeline_mode=pl.Buffered(1))


def _ffn_kernel(x_ref, xs_ref, g_ref, wg_ref, wu_ref, wd_ref, o_ref, os_ref, h_ref):
    i, j = pl.program_id(0), pl.program_id(1)
    tm = x_ref.shape[0]

    @pl.when(j == 0)
    def _():
        x = x_ref[...]
        h_ref[:tm] = _rms(x, g_ref[...]).astype(BF16)
        h_ref[tm:] = _rms(xs_ref[...], g_ref[...]).astype(BF16)
        o_ref[...] = x

    @pl.when((i == 0) & (j == 0))
    def _():
        os_ref[...] = xs_ref[...]

    h = h_ref[...]
    a = jnp.dot(h, wg_ref[...].astype(BF16), preferred_element_type=F32)
    b = jnp.dot(h, wu_ref[...].astype(BF16), preferred_element_type=F32)
    t = a * jax.nn.sigmoid(a) * b
    y = 0.5 * _dot(t, wd_ref[...])
    o_ref[...] += y[:tm]

    @pl.when(i == 0)
    def _():
        os_ref[...] += y[tm:]


def _ffn(x, xs, g, wg, wu, wd, l):
    m, d = x.shape
    f = wg.shape[2]
    tm = _row_tile(m, ROW_TILE)
    tf = 512
    return pl.pallas_call(
        _ffn_kernel,
        grid=(m // tm, f // tf),
        in_specs=[
            _resident((tm, d), lambda i, j: (i, 0)),
            pl.BlockSpec((SIDE_ROWS, d), lambda i, j: (0, 0)),
            pl.BlockSpec((1, d), lambda i, j: (0, 0)),
            pl.BlockSpec((None, d, tf), lambda i, j: (l, 0, j)),
            pl.BlockSpec((None, d, tf), lambda i, j: (l, 0, j)),
            pl.BlockSpec((None, tf, d), lambda i, j: (l, j, 0)),
        ],
        out_specs=(_resident((tm, d), lambda i, j: (i, 0)), pl.BlockSpec((SIDE_ROWS, d), lambda i, j: (0, 0))),
        out_shape=(jax.ShapeDtypeStruct((m, d), F32), jax.ShapeDtypeStruct((SIDE_ROWS, d), F32)),
        scratch_shapes=[pltpu.VMEM((tm + SIDE_ROWS, d), BF16)],
        compiler_params=_params("arbitrary", "arbitrary"),
        name="ffn",
    )(x, xs, g[l].reshape(1, d), wg, wu, wd)


def _rms_mm_kernel(*refs, tn, side):
    if side:
        x_ref, xs_ref, g_ref, w_ref, o_ref, os_ref, h_ref = refs
    else:
        x_ref, g_ref, w_ref, o_ref, h_ref = refs
    j = pl.program_id(1)
    tm = x_ref.shape[0]

    @pl.when(j == 0)
    def _():
        h_ref[:tm] = _rms(x_ref[...], g_ref[...]).astype(BF16)
        if side:
            h_ref[tm:] = _rms(xs_ref[...], g_ref[...]).astype(BF16)

    y = jnp.dot(h_ref[...], w_ref[:, pl.ds(pl.multiple_of(j * tn, tn), tn)], preferred_element_type=F32)
    o_ref[...] = y[:tm]
    if side:
        os_ref[...] = y[tm:]


def _whole_weight_spec(w, l):
    if w.ndim == 2:
        return _resident(w.shape, lambda i, j: (0, 0))
    return _resident((None,) + w.shape[1:], lambda i, j: (l, 0, 0))


def _rms_mm(x, xs, g, w, l=None):
    m, d = x.shape
    n = w.shape[-1]
    tm = _row_tile(m, ROW_TILE)
    tn = _row_tile(n, 512)
    side = xs is not None
    extra = SIDE_ROWS if side else 0
    x_spec = pl.BlockSpec((tm, d), lambda i, j: (i, 0))
    g_spec = pl.BlockSpec((1, d), lambda i, j: (0, 0))
    side_in = [pl.BlockSpec((SIDE_ROWS, d), lambda i, j: (0, 0))] if side else []
    out, ospec = jax.ShapeDtypeStruct((m, n), F32), pl.BlockSpec((tm, tn), lambda i, j: (i, j))
    if side:
        out = (out, jax.ShapeDtypeStruct((m // tm, SIDE_ROWS, n), F32))
        ospec = (ospec, pl.BlockSpec((None, SIDE_ROWS, tn), lambda i, j: (i, 0, j)))
    res = pl.pallas_call(
        partial(_rms_mm_kernel, tn=tn, side=side),
        grid=(m // tm, n // tn),
        in_specs=[x_spec] + side_in + [g_spec, _whole_weight_spec(w, l)],
        out_specs=ospec,
        out_shape=out,
        scratch_shapes=[pltpu.VMEM((tm + extra, d), BF16)],
        compiler_params=_params("arbitrary", "arbitrary"),
        name="rms_mm",
    )(x, *([xs] if side else []), g.reshape(1, d), w)
    return (res[0], res[1][0]) if side else res


def _out_proj_kernel(a1_ref, a2_ref, s1_ref, s2_ref, w_ref, x_ref, xs_ref, o_ref, os_ref, l1_scr, l2_scr, *, tn):
    j = pl.program_id(1)
    tm, k1 = a1_ref.shape

    @pl.when(j == 0)
    def _():
        l1_scr[:tm] = a1_ref[...]
        l1_scr[tm:] = s1_ref[...]
        l2_scr[:tm] = a2_ref[...]
        l2_scr[tm:] = s2_ref[...]

    cols = pl.ds(pl.multiple_of(j * tn, tn), tn)
    y = (jnp.dot(l1_scr[...], w_ref[:k1, cols], preferred_element_type=F32)
         + jnp.dot(l2_scr[...], w_ref[k1:, cols], preferred_element_type=F32))
    o_ref[...] = x_ref[...] + y[:tm]
    os_ref[...] = xs_ref[...] + y[tm:]


def _out_proj(a1, a2, s1, s2, w, l, x, xs):
    m, k1 = a1.shape
    k2 = a2.shape[1]
    d = w.shape[-1]
    tm = _row_tile(m, ROW_TILE)
    tn = 512
    tile = pl.BlockSpec((tm, tn), lambda i, j: (i, j))
    side_tile = pl.BlockSpec((SIDE_ROWS, tn), lambda i, j: (0, j))
    side_out = pl.BlockSpec((None, SIDE_ROWS, tn), lambda i, j: (i, 0, j))
    res, side_res = pl.pallas_call(
        partial(_out_proj_kernel, tn=tn),
        grid=(m // tm, d // tn),
        in_specs=[
            pl.BlockSpec((tm, k1), lambda i, j: (i, 0)),
            pl.BlockSpec((tm, k2), lambda i, j: (i, 0)),
            pl.BlockSpec((SIDE_ROWS, k1), lambda i, j: (0, 0)),
            pl.BlockSpec((SIDE_ROWS, k2), lambda i, j: (0, 0)),
            _whole_weight_spec(w, l),
            tile, side_tile,
        ],
        out_specs=(tile, side_out),
        out_shape=(jax.ShapeDtypeStruct((m, d), F32), jax.ShapeDtypeStruct((m // tm, SIDE_ROWS, d), F32)),
        scratch_shapes=[pltpu.VMEM((tm + SIDE_ROWS, k1), BF16), pltpu.VMEM((tm + SIDE_ROWS, k2), BF16)],
        compiler_params=_params("arbitrary", "arbitrary"),
        name="out_proj",
    )(a1, a2, s1, s2, w, x, xs)
    return res, side_res[0]


def _mem_kv_post_kernel(kv_ref, g_ref, k_ref, v_ref):
    g = g_ref[...]
    for h in range(MEM_HEADS):
        k_ref[:, h, :] = _rms(kv_ref[:, h * HEAD_DIM:(h + 1) * HEAD_DIM], g)
        v_ref[:, h, :] = kv_ref[:, MEM_W + h * HEAD_DIM:MEM_W + (h + 1) * HEAD_DIM]


def _mem_kv_post(kv, g_mk):
    m = kv.shape[0]
    out = jax.ShapeDtypeStruct((m, MEM_HEADS, HEAD_DIM), F32)
    return pl.pallas_call(
        _mem_kv_post_kernel,
        out_shape=(out, out),
        compiler_params=pltpu.CompilerParams(vmem_limit_bytes=VMEM_LIMIT),
        name="mem_kv_post",
    )(kv, g_mk.reshape(1, HEAD_DIM))


def _mem_attend_kernel(q_ref, g_ref, k_ref, v_ref, o_ref, *, rows):
    g = g_ref[...]
    for h in range(MEM_HEADS):
        sl = slice(h * HEAD_DIM, (h + 1) * HEAD_DIM)
        q = q_ref[0, :, sl]
        if rows < SUBLANE:
            q = jnp.broadcast_to(q[0:1], (SUBLANE, HEAD_DIM))
        q = _rms(q, g) * SCALE_LOG2E
        s2 = _dot_nt(q, k_ref[0, :, h, :])
        e = jnp.exp2(s2 - jnp.max(s2, axis=-1, keepdims=True))
        p = e / jnp.sum(e, axis=-1, keepdims=True)
        o = _dot(p, v_ref[0, :, h, :])
        o_ref[0, :, sl] = o[:rows].astype(o_ref.dtype)


def _mem_attend(z, col_block, mk, mv, g_mq):
    b, t, _ = z.shape
    tm = _row_tile(t, 512)
    kv_spec = pl.BlockSpec((1, MEM_LEN, MEM_HEADS, HEAD_DIM), lambda i, j: (i, 0, 0, 0))
    return pl.pallas_call(
        partial(_mem_attend_kernel, rows=tm),
        grid=(b, t // tm),
        in_specs=[
            pl.BlockSpec((1, tm, MEM_W), lambda i, j: (i, j, col_block)),
            pl.BlockSpec((1, HEAD_DIM), lambda i, j: (0, 0)),
            kv_spec, kv_spec,
        ],
        out_specs=pl.BlockSpec((1, tm, MEM_W), lambda i, j: (i, j, 0)),
        out_shape=jax.ShapeDtypeStruct((b, t, MEM_W), BF16),
        compiler_params=_params("parallel", "arbitrary"),
        name="mem_attend",
    )(z, g_mq.reshape(1, HEAD_DIM), mk, mv)


def _layer_norm(x, g, b):
    xc = x - jnp.mean(x, axis=-1, keepdims=True)
    var = jnp.mean(xc * xc, axis=-1, keepdims=True)
    return xc * lax.rsqrt(var + EPS) * g + b


def _gmlp_prompt_kernel(z_ref, lg_ref, lb_ref, ws_ref, bs_ref, o_ref):
    v = _layer_norm(_gelu(z_ref[:, GM_W:]), lg_ref[...], lb_ref[...])
    row = lax.broadcasted_iota(jnp.int32, (CHUNK, CHUNK), 0)
    col = lax.broadcasted_iota(jnp.int32, (CHUNK, CHUNK), 1)
    causal = col <= row
    for g in range(GM_GROUPS):
        sl = slice(g * HEAD_DIM, (g + 1) * HEAD_DIM)
        w = jnp.where(causal, ws_ref[g], 0.0)
        sv = _dot(w, v[:, sl]) + bs_ref[:, g:g + 1]
        o_ref[:, sl] = (_gelu(z_ref[:, sl]) * sv).astype(o_ref.dtype)


def _gmlp_prompt(z, ln_g, ln_b, w_s, b_s):
    m = z.shape[0]
    return pl.pallas_call(
        _gmlp_prompt_kernel,
        grid=(m // CHUNK,),
        in_specs=[
            pl.BlockSpec((CHUNK, 2 * GM_W), lambda i: (i, 0)),
            pl.BlockSpec((1, GM_W), lambda i: (0, 0)),
            pl.BlockSpec((1, GM_W), lambda i: (0, 0)),
            pl.BlockSpec((GM_GROUPS, CHUNK, CHUNK), lambda i: (0, 0, 0)),
            pl.BlockSpec((CHUNK, GM_GROUPS), lambda i: (0, 0)),
        ],
        out_specs=pl.BlockSpec((CHUNK, GM_W), lambda i: (i, 0)),
        out_shape=jax.ShapeDtypeStruct((m, GM_W), BF16),
        compiler_params=_params("parallel"),
        name="gmlp_prompt",
    )(z, ln_g.reshape(1, GM_W), ln_b.reshape(1, GM_W), w_s, b_s.T)


def _gmlp_first_row_kernel(z_ref, lg_ref, lb_ref, w0_ref, b0_ref, o_ref, v_ref):
    v = _layer_norm(_gelu(z_ref[:, GM_W:2 * GM_W]), lg_ref[...], lb_ref[...])
    v_ref[...] = v
    o_ref[...] = (_gelu(z_ref[:, :GM_W]) * (v * w0_ref[...] + b0_ref[...])).astype(o_ref.dtype)


def _gmlp_first_row(z, ln_g, ln_b, w_s, b_s):
    m = z.shape[0]
    w0 = jnp.repeat(w_s[:, 0, 0], HEAD_DIM).reshape(1, GM_W)
    b0 = jnp.repeat(b_s[:, 0], HEAD_DIM).reshape(1, GM_W)
    return pl.pallas_call(
        _gmlp_first_row_kernel,
        out_shape=(jax.ShapeDtypeStruct((m, GM_W), BF16), jax.ShapeDtypeStruct((m, GM_W), F32)),
        compiler_params=pltpu.CompilerParams(vmem_limit_bytes=VMEM_LIMIT),
        name="gmlp_first_row",
    )(z, ln_g.reshape(1, GM_W), ln_b.reshape(1, GM_W), w0, b0)


def _rope_tables(pos):
    half = ROT_DIM // 2
    inv = ROPE_THETA ** (-jnp.arange(half, dtype=F32) / half)
    ang = pos.astype(F32)[:, None] * inv[None, :]
    cos, sin = jnp.cos(ang), jnp.sin(ang)
    t = pos.shape[0]
    one = jnp.ones((t, HEAD_DIM - ROT_DIM), F32)
    zero = jnp.zeros((t, HEAD_DIM - ROT_DIM), F32)
    zh = jnp.zeros((t, half), F32)
    c = jnp.concatenate([cos, cos, one], axis=1)
    s1 = jnp.concatenate([-sin, zh, zero], axis=1)
    s2 = jnp.concatenate([zh, sin, zero], axis=1)
    return jnp.stack([c, s1, s2])


def _nsa_kv_post_kernel(h_ref, rope_ref, gs_ref, gw_ref, kc_ref, vc_ref, ks_ref, vs_ref, kw_ref, vw_ref,
                        *mxu_refs):
    c, s1, s2 = rope_ref[0], rope_ref[1], rope_ref[2]

    def part(i, g):
        return h_ref[:, i * KV_W + g * HEAD_DIM:i * KV_W + (g + 1) * HEAD_DIM]

    for g in range(NSA_KV):
        sl = slice(g * HEAD_DIM, (g + 1) * HEAD_DIM)
        kc_ref[:, g, :] = part(0, g)
        vc_ref[:, g, :] = part(1, g)
        ks = _rope(_rms(part(2, g), gs_ref[...]), c, s1, s2)
        kw = _rope(_rms(part(4, g), gw_ref[...]), c, s1, s2)
        ks_ref[:, g, :] = ks
        vs_ref[:, g, :] = part(3, g)
        kw_ref[:, g, :] = kw
        vw_ref[:, g, :] = part(5, g)
        if mxu_refs:
            ks16_ref, vst16_ref, kw16_ref, vwt16_ref = mxu_refs
            ks16_ref[:, sl] = ks.astype(BF16)
            kw16_ref[:, sl] = kw.astype(BF16)
            vst16_ref[sl, :] = part(3, g).T.astype(BF16)
            vwt16_ref[sl, :] = part(5, g).T.astype(BF16)


def _nsa_kv_post(h, rope_tab, g_ks, g_kw, mxu_copies):
    m = h.shape[0]
    tm = _row_tile(m, 512)
    out = jax.ShapeDtypeStruct((m, NSA_KV, HEAD_DIM), F32)
    ospec = pl.BlockSpec((tm, NSA_KV, HEAD_DIM), lambda i: (i, 0, 0))
    outs, ospecs = (out,) * 6, (ospec,) * 6
    if mxu_copies:
        k16, v16 = jax.ShapeDtypeStruct((m, KV_W), BF16), jax.ShapeDtypeStruct((KV_W, m), BF16)
        kspec, vspec = pl.BlockSpec((tm, KV_W), lambda i: (i, 0)), pl.BlockSpec((KV_W, tm), lambda i: (0, i))
        outs, ospecs = outs + (k16, v16, k16, v16), ospecs + (kspec, vspec, kspec, vspec)
    return pl.pallas_call(
        _nsa_kv_post_kernel,
        grid=(m // tm,),
        in_specs=[
            pl.BlockSpec((tm, 6 * KV_W), lambda i: (i, 0)),
            pl.BlockSpec((3, tm, HEAD_DIM), lambda i: (0, i, 0)),
            pl.BlockSpec((1, HEAD_DIM), lambda i: (0, 0)),
            pl.BlockSpec((1, HEAD_DIM), lambda i: (0, 0)),
        ],
        out_specs=ospecs,
        out_shape=outs,
        compiler_params=_params("parallel"),
        name="nsa_kv_post",
    )(h, rope_tab, g_ks.reshape(1, HEAD_DIM), g_kw.reshape(1, HEAD_DIM))


def _compress_kernel(pt_ref, *refs, n_pages):
    n_in = 2 * PAGES_PER_STEP
    page_refs = refs[:n_in]
    w1_refs, pe_refs, w2_refs = refs[n_in:n_in + 2], refs[n_in + 2:n_in + 4], refs[n_in + 4:n_in + 6]
    gk_ref = refs[n_in + 6]
    o_refs = refs[n_in + 7:n_in + 9]
    l_scr, a_scr, b_scr = refs[n_in + 9:]
    j = pl.program_id(1)
    steps_per_group = PAGES_PER_GROUP // PAGES_PER_STEP
    jj = j % steps_per_group
    n_chunks = n_pages * CHUNKS_PER_PAGE
    group_chunks = PAGES_PER_GROUP * CHUNKS_PER_PAGE

    @pl.when(j == 0)
    def _():
        b_scr[:, :, n_chunks:, :] = jnp.zeros((2, NSA_KV, SUBLANE, HEAD_DIM), F32)

    pairs = CMP_STRIDE * NSA_KV // SUBLANE
    for t in range(2):
        for q in range(PAGES_PER_STEP):
            row0 = pl.multiple_of((jj * PAGES_PER_STEP + q) * CHUNKS_PER_PAGE, CHUNKS_PER_PAGE)
            x4 = page_refs[t * PAGES_PER_STEP + q][0].reshape(CHUNKS_PER_PAGE, pairs, SUBLANE, HEAD_DIM)
            for p2 in range(pairs):
                y = jnp.swapaxes(x4[:, p2], 0, 1)
                for s in range(SUBLANE):
                    p, g = (p2 * SUBLANE + s) // NSA_KV, s % NSA_KV
                    l_scr[t, g, pl.ds(row0, CHUNKS_PER_PAGE), p * HEAD_DIM:(p + 1) * HEAD_DIM] = y[s]

    @pl.when(jj == steps_per_group - 1)
    def _():
        base = pl.multiple_of((j // steps_per_group) * group_chunks, group_chunks)
        for t in range(2):
            for g in range(NSA_KV):
                r = _dot(l_scr[t, g], w1_refs[t][g])
                a_scr[t, g, pl.ds(base, group_chunks), :] = r[:, :HEAD_DIM]
                b_scr[t, g, pl.ds(base, group_chunks), :] = r[:, HEAD_DIM:]

    @pl.when(j == pl.num_programs(1) - 1)
    def _():
        for t in range(2):
            for g in range(NSA_KV):
                pw = _dot(pe_refs[t][g], w1_refs[t][g])
                bias = pw[0:1, :HEAD_DIM] + pw[1:2, HEAD_DIM:]
                h = a_scr[t, g] + b_scr[t, g, pl.ds(1, n_chunks), :] + bias
                y = _dot(_gelu(h), w2_refs[t][g])
                if t == 0:
                    y = _rms(y, gk_ref[...])
                o_refs[t][0, :, g * HEAD_DIM:(g + 1) * HEAD_DIM] = y


def _compress(pool_k, pool_v, page_table, pos_k, w1_k, w2_k, pos_v, w1_v, w2_v, g_kc):
    nb, n_pages = page_table.shape
    assert n_pages % PAGES_PER_GROUP == 0
    n_chunks = n_pages * CHUNKS_PER_PAGE
    kdim = CMP_STRIDE * HEAD_DIM

    def w1ab(w1):
        return jnp.concatenate([w1[:, :CMP_STRIDE].reshape(NSA_KV, kdim, HEAD_DIM),
                                w1[:, CMP_STRIDE:].reshape(NSA_KV, kdim, HEAD_DIM)], axis=-1).astype(BF16)

    def rows_view(pool):
        return pool.reshape(pool.shape[0], PAGE_SIZE * NSA_KV, HEAD_DIM)

    def pe8(pos_enc):
        return jnp.pad(pos_enc.reshape(NSA_KV, 2, kdim), ((0, 0), (0, SUBLANE - 2), (0, 0)))

    def page_spec(q):
        return pl.BlockSpec((1, PAGE_SIZE * NSA_KV, HEAD_DIM),
                            lambda b, j, pt: (pt[b, j * PAGES_PER_STEP + q], 0, 0))

    const3 = lambda b, j, pt: (0, 0, 0)
    page_specs = [page_spec(q) for q in range(PAGES_PER_STEP)]
    w1_spec = _resident((NSA_KV, kdim, 2 * HEAD_DIM), const3)
    pe_spec = _resident((NSA_KV, SUBLANE, kdim), const3)
    w2_spec = _resident((NSA_KV, HEAD_DIM, HEAD_DIM), const3)
    out = jax.ShapeDtypeStruct((nb, n_chunks, KV_W), F32)
    ospec = pl.BlockSpec((1, n_chunks, KV_W), lambda b, j, pt: (b, 0, 0))
    grid_spec = pltpu.PrefetchScalarGridSpec(
        num_scalar_prefetch=1,
        grid=(nb, n_pages // PAGES_PER_STEP),
        in_specs=page_specs + page_specs + [w1_spec, w1_spec, pe_spec, pe_spec, w2_spec, w2_spec,
                                             pl.BlockSpec((1, HEAD_DIM), lambda b, j, pt: (0, 0))],
        out_specs=(ospec, ospec),
        scratch_shapes=[
            pltpu.VMEM((2, NSA_KV, PAGES_PER_GROUP * CHUNKS_PER_PAGE, kdim), F32),
            pltpu.VMEM((2, NSA_KV, n_chunks, HEAD_DIM), F32),
            pltpu.VMEM((2, NSA_KV, n_chunks + SUBLANE, HEAD_DIM), F32),
        ],
    )
    return pl.pallas_call(
        partial(_compress_kernel, n_pages=n_pages),
        grid_spec=grid_spec,
        out_shape=(out, out),
        compiler_params=_params("parallel", "arbitrary"),
        name="compress",
    )(page_table, *([rows_view(pool_k)] * PAGES_PER_STEP), *([rows_view(pool_v)] * PAGES_PER_STEP),
      w1ab(w1_k), w1ab(w1_v), pe8(pos_k), pe8(pos_v), w2_k, w2_v, g_kc.reshape(1, HEAD_DIM))


def _gate_columns(sig, g, rows):
    lane = lax.broadcasted_iota(jnp.int32, (rows, LANE), 1)
    base = g * (NSA_REP * N_GATES)
    return [[jnp.sum(jnp.where(lane == base + r * N_GATES + t, sig, 0.0), axis=1, keepdims=True)
             for t in range(N_GATES)] for r in range(NSA_REP)]


def _rejected_t(score_t, avail_t, n_rows):
    groups = n_rows // SUBLANE
    rows = [score_t[r * SUBLANE:(r + 1) * SUBLANE] for r in range(groups)]
    sub = lax.broadcasted_iota(jnp.int32, rows[0].shape, 0)
    cnt = [jnp.zeros(rows[0].shape, F32) for _ in range(groups)]
    for i in range(n_rows):
        si = score_t[i:i + 1, :]
        gi, oi = divmod(i, SUBLANE)
        for r in range(groups):
            if r > gi:
                beats = si >= rows[r]
            elif r < gi:
                beats = si > rows[r]
            else:
                beats = (si > rows[r]) | ((si == rows[r]) & (sub > oi))
            cnt[r] = cnt[r] + jnp.where(beats, 1.0, 0.0)
    cnt = jnp.concatenate(cnt, axis=0)
    return jnp.where((cnt < N_SELECT) & avail_t, 0.0, 1.0)


def _nsa_prompt_kernel(q_ref, gate_ref, gq_ref, rope_ref, kc_ref, vc_ref, ks_ref, vst_ref, kw_ref, vwt_ref,
                       o_ref, m_scr, l_scr, acc_scr, *, n_sel, n_cmp):
    g = pl.program_id(1)
    qb = pl.program_id(2)
    t0 = qb * QB
    rq = NSA_REP * QB
    c, s1, s2 = rope_ref[0], rope_ref[1], rope_ref[2]
    gq = gq_ref[...]
    qn_l, qr_l = [], []
    for r in range(NSA_REP):
        qn_r = _rms(q_ref[0, :, r * HEAD_DIM:(r + 1) * HEAD_DIM], gq)
        qn_l.append(qn_r * SCALE_LOG2E)
        qr_l.append(_rope(qn_r, c, s1, s2) * SCALE_LOG2E)
    qn = jnp.concatenate(qn_l, axis=0).astype(BF16)
    qr = jnp.concatenate(qr_l, axis=0).astype(BF16)
    pos1 = t0 + lax.broadcasted_iota(jnp.int32, (QB, 1), 0)
    pos3 = jnp.concatenate([pos1] * NSA_REP, axis=0)

    ci = lax.broadcasted_iota(jnp.int32, (1, n_cmp), 1)
    p = _masked_softmax2(_dot_nt(qn, kc_ref[0]), ci * CMP_STRIDE + (CMP_LEN - 1) <= pos3)
    o_c = _dot(p, vc_ref[0])
    psum = p[0:QB] + p[QB:2 * QB] + p[2 * QB:3 * QB]

    ci_l = lax.broadcasted_iota(jnp.int32, (n_sel, n_cmp), 1) * CMP_STRIDE
    sj = lax.broadcasted_iota(jnp.int32, (n_sel, n_cmp), 0) * SEL_BLOCK
    cover_t = jnp.where((ci_l < sj + SEL_BLOCK) & (ci_l + CMP_LEN > sj), 1.0, 0.0)
    imp_t = lax.dot_general(cover_t, psum, (((1,), (1,)), ((), ())),
                            precision=lax.Precision.HIGHEST, preferred_element_type=F32)
    post = t0 + lax.broadcasted_iota(jnp.int32, (n_sel, QB), 1)
    jt = lax.broadcasted_iota(jnp.int32, (n_sel, QB), 0)
    cur = post >> SEL_SHIFT
    avail_t = jt * SEL_BLOCK <= post
    forced_t = (jt == 0) | (jt == cur) | (jt == cur - 1)
    score_t = jnp.where(avail_t, jnp.where(forced_t, FORCED_SCORE, imp_t), NEG)
    rej_t = _rejected_t(score_t, avail_t, n_sel)
    rej = jnp.concatenate([rej_t, jnp.zeros((LANE - n_sel, QB), F32)], axis=0).T.astype(BF16)

    pos_q = t0 + lax.broadcasted_iota(jnp.int32, (1, QB), 1)
    per_head = lambda a: jnp.concatenate([a] * NSA_REP, axis=1)
    q_cat = jnp.concatenate([qr, jnp.concatenate([rej] * NSA_REP, axis=0)], axis=1)

    def normalised(acc_t, den):
        o_t = acc_t / den
        return [o_t[:, r * QB:(r + 1) * QB].T for r in range(NSA_REP)]

    m_scr[...] = jnp.full((1, rq), NEG, F32)
    l_scr[...] = jnp.zeros((1, rq), F32)
    acc_scr[...] = jnp.zeros((HEAD_DIM, rq), F32)
    blocks_per_chunk = SEL_KEY_CHUNK // SEL_BLOCK

    def sel_step(ck, causal):
        k0 = pl.multiple_of(ck * SEL_KEY_CHUNK, SEL_KEY_CHUNK)
        kr = lax.broadcasted_iota(jnp.int32, (SEL_KEY_CHUNK, LANE), 0)
        jl = lax.broadcasted_iota(jnp.int32, (SEL_KEY_CHUNK, LANE), 1)
        expand_t = jnp.where((kr >> SEL_SHIFT) + ck * blocks_per_chunk == jl, NEG, 0.0).astype(BF16)
        sc = _dot_nt(jnp.concatenate([ks_ref[0, pl.ds(k0, SEL_KEY_CHUNK), :], expand_t], axis=1), q_cat)
        if causal:
            kpos = k0 + lax.broadcasted_iota(jnp.int32, (SEL_KEY_CHUNK, 1), 0)
            sc = sc + per_head(jnp.where(kpos <= pos_q, 0.0, NEG))
        m_old = m_scr[...]
        m_new = jnp.maximum(m_old, jnp.max(sc, axis=0, keepdims=True))
        e = jnp.exp2(sc - m_new)
        alpha = jnp.exp2(m_old - m_new)
        l_scr[...] = alpha * l_scr[...] + jnp.sum(e, axis=0, keepdims=True)
        acc_scr[...] = alpha * acc_scr[...] + jnp.dot(vst_ref[:, pl.ds(k0, SEL_KEY_CHUNK)], e.astype(BF16),
                                                      preferred_element_type=F32)
        m_scr[...] = m_new

    n_chunks = (t0 + QB + SEL_KEY_CHUNK - 1) // SEL_KEY_CHUNK

    def sel_body(ck, carry):
        sel_step(ck, False)
        return carry

    lax.fori_loop(0, n_chunks - 1, sel_body, 0)
    sel_step(n_chunks - 1, True)
    o_s = normalised(acc_scr[...], l_scr[...])

    band = WINDOW + QB
    k0 = pl.multiple_of(jnp.maximum(t0 - WINDOW, 0), QB)
    ahead = k0 + lax.broadcasted_iota(jnp.int32, (band, 1), 0) - pos_q
    sw = (_dot_nt(kw_ref[0, pl.ds(k0, band), :], qr)
          + per_head(jnp.where((ahead <= 0) & (ahead >= -WINDOW), 0.0, NEG)))
    ew = jnp.exp2(sw - jnp.max(sw, axis=0, keepdims=True))
    o_w = normalised(jnp.dot(vwt_ref[:, pl.ds(k0, band)], ew.astype(BF16), preferred_element_type=F32),
                     jnp.sum(ew, axis=0, keepdims=True))

    gate = _gate_columns(jax.nn.sigmoid(gate_ref[0]), g, QB)
    for r in range(NSA_REP):
        o_ref[0, :, r * HEAD_DIM:(r + 1) * HEAD_DIM] = (
            gate[r][0] * o_c[r * QB:(r + 1) * QB] + gate[r][1] * o_s[r] + gate[r][2] * o_w[r]).astype(o_ref.dtype)


def _nsa_prompt(z, gates, g_q, rope_tab, kc, vc, ks, vst, kw, vwt):
    b, t, _ = z.shape
    n_cmp = kc.shape[1]
    assert t >= WINDOW + QB
    qw = NSA_REP * HEAD_DIM
    k_spec = pl.BlockSpec((1, t, HEAD_DIM), lambda i, g, j: (i, 0, g))
    vt_spec = pl.BlockSpec((HEAD_DIM, t), lambda i, g, j: (g, i))
    cmp_spec = pl.BlockSpec((1, n_cmp, HEAD_DIM), lambda i, g, j: (i, 0, g))
    return pl.pallas_call(
        partial(_nsa_prompt_kernel, n_sel=t // SEL_BLOCK, n_cmp=n_cmp),
        grid=(b, NSA_KV, t // QB),
        in_specs=[
            pl.BlockSpec((1, QB, qw), lambda i, g, j: (i, j, g)),
            pl.BlockSpec((1, QB, LANE), lambda i, g, j: (i, j, 0)),
            pl.BlockSpec((1, HEAD_DIM), lambda i, g, j: (0, 0)),
            pl.BlockSpec((3, QB, HEAD_DIM), lambda i, g, j: (0, j, 0)),
            cmp_spec, cmp_spec, k_spec, vt_spec, k_spec, vt_spec,
        ],
        out_specs=pl.BlockSpec((1, QB, qw), lambda i, g, j: (i, j, g)),
        out_shape=jax.ShapeDtypeStruct((b, t, NSA_W), BF16),
        scratch_shapes=[
            pltpu.VMEM((1, NSA_REP * QB), F32),
            pltpu.VMEM((1, NSA_REP * QB), F32),
            pltpu.VMEM((HEAD_DIM, NSA_REP * QB), F32),
        ],
        compiler_params=_params("parallel", "parallel", "arbitrary"),
        name="nsa_prompt",
    )(z, gates, g_q.reshape(1, HEAD_DIM), rope_tab, kc, vc, ks, vst, kw, vwt)


def _nsa_sample_select_kernel(q_ref, gq_ref, rope_ref, kc_ref, vc_ref, kw_ref, vw_ref,
                              qr_ref, oc_ref, ow_ref, idx_ref, valid_ref, **static):
    for g in range(NSA_KV):
        sl = slice(g * HEAD_DIM, (g + 1) * HEAD_DIM)
        q3 = [q_ref[0, :, (g * NSA_REP + r) * HEAD_DIM:(g * NSA_REP + r + 1) * HEAD_DIM] for r in range(NSA_REP)]
        qr_ref[0, g], oc_ref[0, g], ow_ref[0, g], idx_ref[0, g], valid_ref[0, g] = _select_group(
            q3, gq_ref[...], rope_ref, kc_ref[0, :, sl], vc_ref[0, :, sl], kw_ref[0, :, g, :], vw_ref[0, :, g, :],
            **static)


def _select_group(q3, gq, rope_ref, kc, vc, kw, vw, *, pos, n_sel, n_cmp, n_win):
    n_cand = SEL_CAND_LANES
    c, s1, s2 = rope_ref[0:1], rope_ref[1:2], rope_ref[2:3]
    row = lax.broadcasted_iota(jnp.int32, (SUBLANE, HEAD_DIM), 0)
    q3 = [jnp.broadcast_to(q, (SUBLANE, HEAD_DIM)) for q in q3]
    q8 = jnp.where(row == 0, q3[0], jnp.where(row == 1, q3[1], q3[2]))
    qn = _rms(q8, gq)
    qr = _rope(qn, c, s1, s2) * SCALE_LOG2E

    ci = lax.broadcasted_iota(jnp.int32, (1, n_cmp), 1)
    p = _masked_softmax2(_dot_nt(qn * SCALE_LOG2E, kc), ci * CMP_STRIDE + (CMP_LEN - 1) <= pos)
    o_c = _dot(p, vc)
    psum = p[0:1] + p[1:2] + p[2:3]

    ci_s = lax.broadcasted_iota(jnp.int32, (n_cmp, n_cand), 0) * CMP_STRIDE
    sj = lax.broadcasted_iota(jnp.int32, (n_cmp, n_cand), 1) * SEL_BLOCK
    cover = jnp.where((ci_s < sj + SEL_BLOCK) & (ci_s + CMP_LEN > sj), 1.0, 0.0)
    imp = jnp.dot(jnp.broadcast_to(psum, (SUBLANE, n_cmp)), cover,
                  precision=lax.Precision.HIGHEST, preferred_element_type=F32)[0:1]
    jl = lax.broadcasted_iota(jnp.int32, (1, n_cand), 1)
    cur = pos // SEL_BLOCK
    avail = jl * SEL_BLOCK <= pos
    forced = (jl == 0) | (jl == cur) | (jl == cur - 1)
    score = jnp.where(avail, jnp.where(forced, FORCED_SCORE, imp), NEG)
    score = jnp.where(jl < n_sel, score, EXCLUDED)

    eye = (lax.broadcasted_iota(jnp.int32, (n_cand, n_cand), 0)
           == lax.broadcasted_iota(jnp.int32, (n_cand, n_cand), 1))
    score_b = jnp.broadcast_to(score, (n_cand, n_cand))
    score_c = jnp.sum(jnp.where(eye, score_b, 0.0), axis=1, keepdims=True)
    il = lax.broadcasted_iota(jnp.int32, (n_cand, n_cand), 1)
    jc = lax.broadcasted_iota(jnp.int32, (n_cand, n_cand), 0)
    beats = (score_b > score_c) | ((score_b == score_c) & (il < jc))
    rank = jnp.sum(jnp.where(beats, 1.0, 0.0), axis=1, keepdims=True)
    kl = lax.broadcasted_iota(jnp.int32, (n_cand, LANE), 1).astype(F32)
    hit = rank == kl
    jcol = lax.broadcasted_iota(jnp.int32, (n_cand, LANE), 0)
    idx = jnp.sum(jnp.where(hit, jcol.astype(F32), 0.0), axis=0, keepdims=True).astype(jnp.int32)
    valid = jnp.sum(jnp.where(hit & (jcol * SEL_BLOCK <= pos), 1.0, 0.0),
                    axis=0, keepdims=True).astype(jnp.int32)

    n_buf = kw.shape[0]
    kpos = pos + 1 - n_win + lax.broadcasted_iota(jnp.int32, (1, n_buf), 1)
    pw = _masked_softmax2(_dot_nt(qr, kw), (kpos >= pos - WINDOW) & (kpos <= pos))
    return qr, o_c, _dot(pw, vw), idx, valid


def _nsa_sample_gather_kernel(idx_ref, valid_ref, pt_ref, kpool_ref, vpool_ref, kn_ref, vn_ref, qr_ref, oc_ref,
                              ow_ref, gate_ref, o_ref, kbuf, vbuf, sems, *, pos, n_past):
    b = pl.program_id(0)
    halves = PAGE_SIZE // SEL_BLOCK

    def block_copies(g, k):
        blk = jnp.minimum(idx_ref[b, g, k], n_past - 1)
        src = pt_ref[b, blk // halves] * halves + blk % halves
        slot = g * N_SELECT + k
        return (pltpu.make_async_copy(kpool_ref.at[src], kbuf.at[slot], sems.at[0, slot]),
                pltpu.make_async_copy(vpool_ref.at[src], vbuf.at[slot], sems.at[1, slot]))

    for g in range(NSA_KV):
        for k in range(N_SELECT):
            for cp in block_copies(g, k):
                cp.start()
    for g in range(NSA_KV):
        for k in range(N_SELECT):
            for cp in block_copies(g, k):
                cp.wait()

    col = lax.broadcasted_iota(jnp.int32, (1, SEL_BLOCK * NSA_KV), 1)
    col_row, col_group = col >> (NSA_KV.bit_length() - 1), col & (NSA_KV - 1)
    sig = jax.nn.sigmoid(gate_ref[0])
    for g in range(NSA_KV):
        scores, masks, values = [], [], []
        for k in range(N_SELECT):
            idx = idx_ref[b, g, k]
            in_past = idx < n_past
            k_blk = jnp.where(in_past, kbuf[g * N_SELECT + k], kn_ref[0])
            values.append(jnp.where(in_past, vbuf[g * N_SELECT + k], vn_ref[0]))
            kpos = idx * SEL_BLOCK + col_row
            msk = (col_group == g) & (kpos <= jnp.where(valid_ref[b, g, k] > 0, pos, -1))
            masks.append(msk)
            scores.append(jnp.where(msk, _dot_nt(qr_ref[0, g], k_blk), NEG))
        mx = scores[0].max(axis=-1, keepdims=True)
        for sc in scores[1:]:
            mx = jnp.maximum(mx, sc.max(axis=-1, keepdims=True))
        den = jnp.zeros((SUBLANE, 1), F32)
        acc = jnp.zeros((SUBLANE, HEAD_DIM), F32)
        for sc, msk, v_blk in zip(scores, masks, values):
            e = jnp.where(msk, jnp.exp2(sc - mx), 0.0)
            den = den + jnp.sum(e, axis=-1, keepdims=True)
            acc = acc + _dot(e, v_blk)
        o_s = acc / den
        gate = _gate_columns(sig, g, 1)
        o_c, o_w = oc_ref[0, g], ow_ref[0, g]
        for r in range(NSA_REP):
            h = g * NSA_REP + r
            o_ref[0, :, h * HEAD_DIM:(h + 1) * HEAD_DIM] = (
                gate[r][0] * o_c[r:r + 1] + gate[r][1] * o_s[r:r + 1] + gate[r][2] * o_w[r:r + 1]
            ).astype(o_ref.dtype)


def _nsa_sample(zq, zgate, g_q, rope_row, kc, vc, kw_buf, vw_buf, n_win, pool_k, pool_v, k_new, v_new,
                page_table, pos):
    db = zq.shape[0]
    n_cmp = kc.shape[1]
    n_buf = kw_buf.shape[1]
    n_past = page_table.shape[1] * (PAGE_SIZE // SEL_BLOCK)
    n_sel = n_past + 1
    assert n_sel <= SEL_CAND_LANES
    qw = NSA_REP * HEAD_DIM
    vec = jax.ShapeDtypeStruct((db, NSA_KV, SUBLANE, HEAD_DIM), F32)
    ivec = jax.ShapeDtypeStruct((db, NSA_KV, 1, LANE), jnp.int32)
    vspec = pl.BlockSpec((1, NSA_KV, SUBLANE, HEAD_DIM), lambda i: (i, 0, 0, 0))
    ispec = pl.BlockSpec((1, NSA_KV, 1, LANE), lambda i: (i, 0, 0, 0))
    cmp_spec = pl.BlockSpec((1, n_cmp, KV_W), lambda i: (i, 0, 0))
    win_spec = pl.BlockSpec((1, n_buf, NSA_KV, HEAD_DIM), lambda i: (i, 0, 0, 0))
    qr, o_c, o_w, idx, valid = pl.pallas_call(
        partial(_nsa_sample_select_kernel, pos=pos, n_sel=n_sel, n_cmp=n_cmp, n_win=n_win),
        grid=(db,),
        in_specs=[
            pl.BlockSpec((1, 1, NSA_W), lambda i: (i, 0, 0)),
            pl.BlockSpec((1, HEAD_DIM), lambda i: (0, 0)),
            pl.BlockSpec((3, HEAD_DIM), lambda i: (0, 0)),
            cmp_spec, cmp_spec, win_spec, win_spec,
        ],
        out_specs=(vspec, vspec, vspec, ispec, ispec),
        out_shape=(vec, vec, vec, ivec, ivec),
        compiler_params=_params("parallel"),
        name="nsa_sample_select",
    )(zq, g_q.reshape(1, HEAD_DIM), rope_row, kc, vc, kw_buf, vw_buf)

    idx = idx[:, :, 0, :N_SELECT]
    valid = valid[:, :, 0, :N_SELECT]
    halves = PAGE_SIZE // SEL_BLOCK

    n_pool = pool_k.shape[0]
    n_blk = NSA_KV * N_SELECT
    blk_shape = (SEL_BLOCK * NSA_KV, HEAD_DIM)
    pool_spec = pl.BlockSpec(memory_space=pl.ANY)
    gvec = pl.BlockSpec((1, NSA_KV, SUBLANE, HEAD_DIM), lambda i, *_: (i, 0, 0, 0))
    new_spec = pl.BlockSpec((1,) + blk_shape, lambda i, *_: (i, 0, 0))
    grid_spec = pltpu.PrefetchScalarGridSpec(
        num_scalar_prefetch=3,
        grid=(db,),
        in_specs=[
            pool_spec, pool_spec, new_spec, new_spec, gvec, gvec, gvec,
            pl.BlockSpec((1, 1, LANE), lambda i, *_: (i, 0, 0)),
        ],
        out_specs=pl.BlockSpec((1, 1, NSA_W), lambda i, *_: (i, 0, 0)),
        scratch_shapes=[
            pltpu.VMEM((n_blk,) + blk_shape, F32),
            pltpu.VMEM((n_blk,) + blk_shape, F32),
            pltpu.SemaphoreType.DMA((2, n_blk)),
        ],
    )
    pool_k2 = pool_k.reshape(n_pool * halves, SEL_BLOCK * NSA_KV, HEAD_DIM)
    pool_v2 = pool_v.reshape(n_pool * halves, SEL_BLOCK * NSA_KV, HEAD_DIM)

    def new_block(a):
        a = jnp.pad(a, ((0, 0), (0, SEL_BLOCK - a.shape[1]), (0, 0), (0, 0)))
        return a.reshape(db, SEL_BLOCK * NSA_KV, HEAD_DIM)

    k_new, v_new = new_block(k_new), new_block(v_new)
    return pl.pallas_call(
        partial(_nsa_sample_gather_kernel, pos=pos, n_past=n_past),
        grid_spec=grid_spec,
        out_shape=jax.ShapeDtypeStruct((db, 1, NSA_W), BF16),
        compiler_params=_params("parallel"),
        name="nsa_sample_gather",
    )(idx, valid, page_table, pool_k2, pool_v2, k_new, v_new, qr, o_c, o_w, zgate)


def kernel(x_prompt, x_sample, mem_prompt, cache_mem_k, cache_mem_v, cache_k_cmp, cache_v_cmp, cache_k_sel, cache_v_sel, state_k_win, state_v_win, page_table, g_ffn1, w_ffn1_gate, w_ffn1_up, w_ffn1_down, g_mix, g_ffn2, w_ffn2_gate, w_ffn2_up, w_ffn2_down, g_mem, w_mem_kv, g_mq, g_mk, w_in_a, ln_v_g, ln_v_b, w_spatial, b_spatial, w_out_a, g_kv, w_kv, pos_kc, w1_kc, w2_kc, pos_vc, w1_vc, w2_vc, g_kc, g_ks, g_kw, w_in_b, g_q, w_out_b):
    nb, t, d = x_prompt.shape
    db, ds, _ = x_sample.shape
    depth = g_ffn1.shape[0]
    n_a = w_in_a.shape[0]
    past_len = page_table.shape[1] * PAGE_SIZE
    assert ds == 1 and t % PAGE_SIZE == 0 and db * ds <= SIDE_ROWS
    pos_s = past_len
    ns = db * ds
    side = lambda a: jnp.pad(a, ((0, SIDE_ROWS - ns), (0, 0)))

    xp = x_prompt.reshape(nb * t, d)
    xs = side(x_sample.reshape(ns, d))
    mem = mem_prompt.reshape(nb * MEM_LEN, d)
    w_kv16, w_mem_kv16, w_in_a16 = w_kv.astype(BF16), w_mem_kv.astype(BF16), w_in_a.astype(BF16)
    w_out_a16, w_out_b16 = w_out_a.astype(BF16), w_out_b.astype(BF16)
    mem_k_list, mem_v_list, gm_v_list = [], [], []
    rope_p = _rope_tables(jnp.arange(t, dtype=jnp.int32))
    rope_s = _rope_tables(pos_s + jnp.arange(ds, dtype=jnp.int32))
    rope_p2 = jnp.tile(rope_p, (1, nb, 1))
    kv4 = lambda a, n: a.reshape(n, -1, NSA_KV, HEAD_DIM)

    for l in range(depth):
        if l == n_a:
            hp, hs = _rms_mm(xp, xs, g_kv, w_kv16)
            hs = hs[:ns]
            post_p = _nsa_kv_post(hp, rope_p2, g_ks, g_kw, True)
            kc_p_raw, vc_p_raw, ks_p, vs_p, kw_p, vw_p = [kv4(a, nb) for a in post_p[:6]]
            ks16, vst16, kw16, vwt16 = post_p[6:]
            kv16_p = [ks16.reshape(nb, t, KV_W), vst16, kw16.reshape(nb, t, KV_W), vwt16]
            post_s = _nsa_kv_post(hs, jnp.broadcast_to(rope_s, (3, db, HEAD_DIM)), g_ks, g_kw, False)
            kc_s_raw, vc_s_raw, ks_s, vs_s, kw_s, vw_s = [kv4(a, db) for a in post_s[:6]]
            pages_p = jnp.arange(nb * t // PAGE_SIZE, dtype=jnp.int32).reshape(nb, t // PAGE_SIZE)
            cmp_w = (pos_kc, w1_kc, w2_kc, pos_vc, w1_vc, w2_vc, g_kc)
            kc_p, vc_p = _compress(kc_p_raw.reshape(-1, PAGE_SIZE, NSA_KV, HEAD_DIM),
                                   vc_p_raw.reshape(-1, PAGE_SIZE, NSA_KV, HEAD_DIM), pages_p, *cmp_w)
            kc_s, vc_s = _compress(cache_k_cmp, cache_v_cmp, page_table, *cmp_w)
            wbuf = state_k_win.shape[1]
            n_win = wbuf + ds
            pad = jnp.zeros((db, (-n_win) % SUBLANE, NSA_KV, HEAD_DIM), F32)
            kw_buf = jnp.concatenate([state_k_win, kw_s, pad], axis=1)
            vw_buf = jnp.concatenate([state_v_win, vw_s, pad], axis=1)

        xp, xs = _ffn(xp, xs, g_ffn1, w_ffn1_gate, w_ffn1_up, w_ffn1_down, l)

        mk_p, mv_p = _mem_kv_post(_rms_mm(mem, None, g_mem[l], w_mem_kv16, l), g_mk[l])
        mk_p = mk_p.reshape(nb, MEM_LEN, MEM_HEADS, HEAD_DIM)
        mv_p = mv_p.reshape(nb, MEM_LEN, MEM_HEADS, HEAD_DIM)
        mem_k_list.append(mk_p)
        mem_v_list.append(mv_p)
        mk_s, mv_s = cache_mem_k[l], cache_mem_v[l]

        if l < n_a:
            zp, zs = _rms_mm(xp, xs, g_mix[l], w_in_a16, l)
            zs = zs[:ns]
            mem_blk = 2 * GM_W // MEM_W
            o1_p = _gmlp_prompt(zp, ln_v_g[l], ln_v_b[l], w_spatial[l], b_spatial[l])
            o1_s, v_s = _gmlp_first_row(zs, ln_v_g[l], ln_v_b[l], w_spatial[l], b_spatial[l])
            gm_v_list.append(v_s.reshape(db, ds, GM_W))
            o2_p = _mem_attend(zp.reshape(nb, t, -1), mem_blk, mk_p, mv_p, g_mq[l])
            o2_s = _mem_attend(zs.reshape(db, ds, -1), mem_blk, mk_s, mv_s, g_mq[l])
            w_out, l_out = w_out_a16, l
        else:
            lb = l - n_a
            n_gate = N_GATES * NSA_HEADS
            w_qm = jnp.concatenate([w_in_b[lb][:, :NSA_W], w_in_b[lb][:, NSA_W + n_gate:]], axis=1).astype(BF16)
            w_gate = jnp.pad(w_in_b[lb][:, NSA_W:NSA_W + n_gate], ((0, 0), (0, LANE - n_gate))).astype(BF16)
            zp, zs = _rms_mm(xp, xs, g_mix[l], w_qm)
            gp, gs = _rms_mm(xp, xs, g_mix[l], w_gate)
            zp, zs = zp.reshape(nb, t, -1), zs[:ns].reshape(db, ds, -1)
            gp, gs = gp.reshape(nb, t, LANE), gs[:ns].reshape(db, ds, LANE)
            o1_p = _nsa_prompt(zp, gp, g_q[lb], rope_p, kc_p, vc_p, *kv16_p).reshape(nb * t, NSA_W)
            o1_s = _nsa_sample(zs, gs, g_q[lb], rope_s[:, 0, :], kc_s, vc_s, kw_buf, vw_buf, n_win,
                               cache_k_sel, cache_v_sel, ks_s, vs_s, page_table, pos_s).reshape(db * ds, NSA_W)
            mem_blk = NSA_W // MEM_W
            o2_p = _mem_attend(zp, mem_blk, mk_p, mv_p, g_mq[l])
            o2_s = _mem_attend(zs, mem_blk, mk_s, mv_s, g_mq[l])
            w_out, l_out = w_out_b16, lb
        xp, xs = _out_proj(o1_p, o2_p.reshape(nb * t, MEM_W), side(o1_s), side(o2_s.reshape(ns, MEM_W)),
                           w_out, l_out, xp, xs)

        xp, xs = _ffn(xp, xs, g_ffn2, w_ffn2_gate, w_ffn2_up, w_ffn2_down, l)

    wp = min(WINDOW, t)
    ws = min(WINDOW, past_len + ds)
    return (xp.reshape(nb, t, d), xs[:ns].reshape(db, ds, d), jnp.stack(mem_k_list), jnp.stack(mem_v_list),
            kc_p_raw, vc_p_raw, ks_p, vs_p, kw_p[:, -wp:], vw_p[:, -wp:],
            kc_s_raw, vc_s_raw, ks_s, vs_s, kw_buf[:, n_win - ws:n_win], vw_buf[:, n_win - ws:n_win],
            jnp.stack(gm_v_list))
```
